```python
import math
import jax, jax.numpy as jnp
from jax import lax
import numpy as np

D_MODEL = 1024
BATCH = 8
SEQ = 2048
DEPTH = 4
DEC_BATCH = 128
DEC_SEQ = 1
PAST_LEN = 16384
PAGE_SIZE = 128

D_MIX = D_MODEL
ML_HEADS = 4
ML_DK = 64
ML_DV = 64
ML_W = ML_HEADS * ML_DV
ML_CHUNK = 64
SSD_HEADS = 8
SSD_P = 64
SSD_N = 64
SSD_GROUPS = 2
SSD_W = SSD_HEADS * SSD_P
SSD_CONV = 4
SSD_CONV_DIM = SSD_W + 2 * SSD_GROUPS * SSD_N
SSD_CHUNK = 64
GLA_HEADS = 4
GLA_DK = 32
GLA_DV = 64
GLA_W = GLA_HEADS * GLA_DV
GLA_RANK = 16
GLA_TAU = 16.0
GLA_CHUNK = 16
D_FF = 2816
FFN_CONV = 3
EPS = 1e-6

IN_SIZES = (ML_HEADS * ML_DK, ML_HEADS * ML_DK, ML_W, ML_W, ML_HEADS, ML_HEADS,
            SSD_W, SSD_CONV_DIM, SSD_HEADS,
            GLA_HEADS * GLA_DK, GLA_HEADS * GLA_DK, GLA_W, GLA_W, GLA_RANK)
IN_COLS = sum(IN_SIZES)

kernel_name = "hybrid_mlstm_ssd_gla_convffn_step"


def _split_points():
    pts, acc = [], 0
    for s in IN_SIZES[:-1]:
        acc += s
        pts.append(acc)
    return pts


def rmsnorm(x, g):
    xf = x.astype(jnp.float32)
    y = xf * lax.rsqrt(jnp.mean(xf * xf, axis=-1, keepdims=True) + EPS)
    return (y * g.astype(jnp.float32)).astype(x.dtype)


def head_rmsnorm(x, g, n_groups):
    shp = x.shape
    xg = x.astype(jnp.float32).reshape(shp[:-1] + (n_groups, shp[-1] // n_groups))
    y = xg * lax.rsqrt(jnp.mean(xg * xg, axis=-1, keepdims=True) + EPS)
    return y.reshape(shp) * g.astype(jnp.float32)


def causal_dwconv(x, buf, w, b):
    K = w.shape[0]
    L = x.shape[1]
    xp = jnp.concatenate([buf.astype(x.dtype), x], axis=1)
    y = b + sum(xp[:, j:j + L] * w[j] for j in range(K))
    return y, xp[:, L:]


def to_chunks(a, n_chunks, chunk):
    a = a.reshape((a.shape[0], n_chunks, chunk) + a.shape[2:])
    return jnp.moveaxis(a, 1, 0)


def from_chunks(a):
    a = jnp.moveaxis(a, 0, 1)
    return a.reshape((a.shape[0], a.shape[1] * a.shape[2]) + a.shape[3:])


def mlstm_mixer(q, k, v, log_i, log_f, c0, n0, m0):
    L = q.shape[1]
    T = math.gcd(L, ML_CHUNK)
    nC = L // T
    causal = jnp.tril(jnp.ones((T, T), dtype=bool))[None, :, :, None]

    def step(carry, inp):
        C, n, m = carry
        qc, kc, vc, li, lf = inp
        b = jnp.cumsum(lf, axis=1)
        a = b + m[:, None, :]
        D = b[:, :, None, :] - b[:, None, :, :] + li[:, None, :, :]
        D = jnp.where(causal, D, -jnp.inf)
        m_t = jnp.maximum(a, jnp.max(D, axis=2))
        w_inter = jnp.exp(a - m_t)
        qk = jnp.einsum('bthd,bshd->btsh', qc, kc) * jnp.exp(D - m_t[:, :, None, :])
        num = jnp.einsum('btsh,bshe->bthe', qk, vc) + w_inter[..., None] * jnp.einsum('bthd,bhde->bthe', qc, C)
        den = jnp.sum(qk, axis=2) + w_inter * jnp.einsum('bthd,bhd->bth', qc, n)
        h = num / jnp.maximum(jnp.abs(den), jnp.exp(-m_t))[..., None]
        bL = b[:, -1]
        m_new = m_t[:, -1]
        g_state = jnp.exp(bL + m - m_new)
        g_s = jnp.exp(bL[:, None] - b + li - m_new[:, None])
        C_new = g_state[..., None, None] * C + jnp.einsum('bsh,bshd,bshe->bhde', g_s, kc, vc)
        n_new = g_state[..., None] * n + jnp.einsum('bsh,bshd->bhd', g_s, kc)
        return (C_new, n_new, m_new), h

    xs = tuple(to_chunks(a, nC, T) for a in (q, k, v, log_i, log_f))
    (c1, n1, m1), h = lax.scan(step, (c0, n0, m0), xs)
    return from_chunks(h), c1, n1, m1


def ssd_mixer(x, dt, A, Bm, Cm, s0):
    bsz, L = x.shape[:2]
    R = SSD_HEADS // SSD_GROUPS
    T = math.gcd(L, SSD_CHUNK)
    nC = L // T
    causal = jnp.tril(jnp.ones((T, T), dtype=bool))[None, :, :, None, None]
    xg = x.reshape(bsz, L, SSD_GROUPS, R, SSD_P)
    dtg = dt.reshape(bsz, L, SSD_GROUPS, R)
    Ag = A.reshape(SSD_GROUPS, R)
    S0 = s0.reshape(bsz, SSD_GROUPS, R, SSD_P, SSD_N)

    def step(S, inp):
        xc, dtc, Bc, Cc = inp
        cs = jnp.cumsum(dtc * Ag, axis=1)
        seg = cs[:, :, None] - cs[:, None]
        Lmat = jnp.exp(jnp.where(causal, seg, -jnp.inf))
        CB = jnp.einsum('btgn,bsgn->btsg', Cc, Bc)
        scores = CB[..., None] * Lmat * dtc[:, None]
        y = jnp.einsum('btsgr,bsgrp->btgrp', scores, xc) + jnp.exp(cs)[..., None] * jnp.einsum('btgn,bgrpn->btgrp', Cc, S)
        csL = cs[:, -1]
        w = jnp.exp(csL[:, None] - cs) * dtc
        S_new = jnp.exp(csL)[..., None, None] * S + jnp.einsum('bsgr,bsgrp,bsgn->bgrpn', w, xc, Bc)
        return S_new, y

    xs = tuple(to_chunks(a, nC, T) for a in (xg, dtg, Bm, Cm))
    S1, y = lax.scan(step, S0, xs)
    y = from_chunks(y).reshape(bsz, L, SSD_HEADS, SSD_P)
    return y, S1.reshape(bsz, SSD_HEADS, SSD_P, SSD_N)


def gla_mixer(q, k, v, log_a, s0):
    L = q.shape[1]
    T = math.gcd(L, GLA_CHUNK)
    nC = L // T
    causal = jnp.tril(jnp.ones((T, T), dtype=bool))[None, :, :, None, None]

    def step(S, inp):
        qc, kc, vc, la = inp
        bc = jnp.cumsum(la, axis=1)
        decay = jnp.exp(jnp.where(causal, bc[:, :, None] - bc[:, None], -jnp.inf))
        att = jnp.einsum('bthk,bshk,btshk->btsh', qc, kc, decay)
        o = jnp.einsum('btsh,bshv->bthv', att, vc) + jnp.einsum('bthk,bhkv->bthv', qc * jnp.exp(bc), S)
        bL = bc[:, -1]
        S_new = jnp.exp(bL)[..., None] * S + jnp.einsum('bshk,bshv->bhkv', kc * jnp.exp(bL[:, None] - bc), vc)
        return S_new, o

    xs = tuple(to_chunks(a, nC, T) for a in (q, k, v, log_a))
    S1, o = lax.scan(step, s0, xs)
    return from_chunks(o), S1


def zero_state(bsz, dtype):
    f32 = jnp.float32
    return (jnp.zeros((bsz, ML_HEADS, ML_DK, ML_DV), f32), jnp.zeros((bsz, ML_HEADS, ML_DK), f32),
            jnp.zeros((bsz, ML_HEADS), f32), jnp.zeros((bsz, SSD_HEADS, SSD_P, SSD_N), f32),
            jnp.zeros((bsz, SSD_CONV - 1, SSD_CONV_DIM), dtype), jnp.zeros((bsz, GLA_HEADS, GLA_DK, GLA_DV), f32),
            jnp.zeros((bsz, FFN_CONV - 1, 2 * D_FF), dtype))


def layer(x, state, lp):
    (g1, w_in, b_i, b_f, ml_g, cw, cb, dt_bias, a_log, d_skip, ssd_g, wf2, bf, gla_g,
     w_out, g2, w_up, fcw, fcb, w_down) = lp
    c0, n0, m0, s0, sconv0, gs0, fconv0 = state
    f32 = jnp.float32
    bsz, L, _ = x.shape
    h = rmsnorm(x, g1)
    proj = h @ w_in
    (mq, mk, mv, mo, mi, mf, sz, sxbc, sdt, gq, gk, gv, gg, gf) = jnp.split(proj, _split_points(), axis=-1)
    q = mq.reshape(bsz, L, ML_HEADS, ML_DK).astype(f32) * ML_DK ** -0.5
    k = mk.reshape(bsz, L, ML_HEADS, ML_DK).astype(f32)
    v = mv.reshape(bsz, L, ML_HEADS, ML_DV).astype(f32)
    log_i = (mi + b_i).astype(f32)
    log_f = jax.nn.log_sigmoid((mf + b_f).astype(f32))
    h_ml, c1, n1, m1 = mlstm_mixer(q, k, v, log_i, log_f, c0.astype(f32), n0.astype(f32), m0.astype(f32))
    y_ml = jax.nn.sigmoid(mo.astype(f32)) * head_rmsnorm(h_ml.reshape(bsz, L, ML_W), ml_g, ML_HEADS)
    xbc, sconv1 = causal_dwconv(sxbc, sconv0, cw, cb)
    xbc = jax.nn.silu(xbc.astype(f32))
    sx, sB, sC = jnp.split(xbc, [SSD_W, SSD_W + SSD_GROUPS * SSD_N], axis=-1)
    dt = jax.nn.softplus((sdt + dt_bias).astype(f32))
    A = -jnp.exp(a_log.astype(f32))
    xh = sx.reshape(bsz, L, SSD_HEADS, SSD_P)
    y_s, s1 = ssd_mixer(xh, dt, A, sB.reshape(bsz, L, SSD_GROUPS, SSD_N), sC.reshape(bsz, L, SSD_GROUPS, SSD_N), s0.astype(f32))
    y_s = y_s + d_skip.astype(f32)[:, None] * xh
    y_ssd = head_rmsnorm(y_s.reshape(bsz, L, SSD_W) * jax.nn.silu(sz.astype(f32)), ssd_g, SSD_GROUPS)
    gq_ = gq.reshape(bsz, L, GLA_HEADS, GLA_DK).astype(f32) * GLA_DK ** -0.5
    gk_ = gk.reshape(bsz, L, GLA_HEADS, GLA_DK).astype(f32)
    gv_ = gv.reshape(bsz, L, GLA_HEADS, GLA_DV).astype(f32)
    log_a = jax.nn.log_sigmoid((gf @ wf2 + bf).astype(f32)) / GLA_TAU
    o, gs1 = gla_mixer(gq_, gk_, gv_, log_a.reshape(bsz, L, GLA_HEADS, GLA_DK), gs0.astype(f32))
    y_gla = head_rmsnorm(o.reshape(bsz, L, GLA_W), gla_g, GLA_HEADS) * jax.nn.silu(gg.astype(f32))
    mix = jnp.concatenate([y_ml, y_ssd, y_gla], axis=-1).astype(x.dtype)
    x = x + mix @ w_out
    h2 = rmsnorm(x, g2)
    up = h2 @ w_up
    upc, fconv1 = causal_dwconv(up, fconv0, fcw, fcb)
    u, gate = jnp.split(upc, 2, axis=-1)
    x = x + (jax.nn.silu(gate) * u) @ w_down
    return x, (c1, n1, m1, s1, sconv1, gs1, fconv1)


def setup_inputs(seed: int = 0) -> dict:
    key = jax.random.key(seed)
    ks = jax.random.split(key, 32)
    f32 = jnp.float32

    def nrm(k, shape, s):
        return s * jax.random.normal(k, shape, f32)

    dt0 = jnp.exp(jax.random.uniform(ks[16], (DEPTH, SSD_HEADS), f32, math.log(1e-3), math.log(1e-1)))
    return {
        "x_prompt": nrm(ks[0], (BATCH, SEQ, D_MODEL), 1.0),
        "x_sample": nrm(ks[1], (DEC_BATCH, DEC_SEQ, D_MODEL), 1.0),
        "state_mlstm_c": nrm(ks[2], (DEPTH, DEC_BATCH, ML_HEADS, ML_DK, ML_DV), 0.5),
        "state_mlstm_n": nrm(ks[3], (DEPTH, DEC_BATCH, ML_HEADS, ML_DK), 0.5),
        "state_mlstm_m": nrm(ks[4], (DEPTH, DEC_BATCH, ML_HEADS), 0.5),
        "state_ssd": nrm(ks[5], (DEPTH, DEC_BATCH, SSD_HEADS, SSD_P, SSD_N), 0.3),
        "state_ssd_conv": nrm(ks[6], (DEPTH, DEC_BATCH, SSD_CONV - 1, SSD_CONV_DIM), 1.0),
        "state_gla": nrm(ks[7], (DEPTH, DEC_BATCH, GLA_HEADS, GLA_DK, GLA_DV), 0.3),
        "state_ffn_conv": nrm(ks[8], (DEPTH, DEC_BATCH, FFN_CONV - 1, 2 * D_FF), 1.0),
        "norm1_g": 1.0 + nrm(ks[9], (DEPTH, D_MODEL), 0.02),
        "w_in": nrm(ks[10], (DEPTH, D_MODEL, IN_COLS), D_MODEL ** -0.5),
        "mlstm_b_i": nrm(ks[11], (DEPTH, ML_HEADS), 0.1),
        "mlstm_b_f": jnp.linspace(3.0, 6.0, ML_HEADS, dtype=f32)[None, :] + nrm(ks[12], (DEPTH, ML_HEADS), 0.1),
        "mlstm_norm_g": 1.0 + nrm(ks[13], (DEPTH, ML_W), 0.02),
        "ssd_conv_w": nrm(ks[14], (DEPTH, SSD_CONV, SSD_CONV_DIM), SSD_CONV ** -0.5),
        "ssd_conv_b": nrm(ks[15], (DEPTH, SSD_CONV_DIM), 0.02),
        "ssd_dt_bias": dt0 + jnp.log(-jnp.expm1(-dt0)),
        "ssd_a_log": jnp.log(jax.random.uniform(ks[17], (DEPTH, SSD_HEADS), f32, 1.0, 16.0)),
        "ssd_d": 1.0 + nrm(ks[18], (DEPTH, SSD_HEADS), 0.1),
        "ssd_norm_g": 1.0 + nrm(ks[19], (DEPTH, SSD_W), 0.02),
        "gla_w_f2": nrm(ks[20], (DEPTH, GLA_RANK, GLA_HEADS * GLA_DK), GLA_RANK ** -0.5),
        "gla_b_f": nrm(ks[21], (DEPTH, GLA_HEADS * GLA_DK), 0.1),
        "gla_norm_g": 1.0 + nrm(ks[22], (DEPTH, GLA_W), 0.02),
        "w_out": nrm(ks[23], (DEPTH, D_MIX, D_MODEL), D_MIX ** -0.5),
        "norm2_g": 1.0 + nrm(ks[24], (DEPTH, D_MODEL), 0.02),
        "w_up": nrm(ks[25], (DEPTH, D_MODEL, 2 * D_FF), D_MODEL ** -0.5),
        "ffn_conv_w": nrm(ks[26], (DEPTH, FFN_CONV, 2 * D_FF), FFN_CONV ** -0.5),
        "ffn_conv_b": nrm(ks[27], (DEPTH, 2 * D_FF), 0.02),
        "w_down": nrm(ks[28], (DEPTH, D_FF, D_MODEL), D_FF ** -0.5),
        "final_norm_g": 1.0 + nrm(ks[29], (D_MODEL,), 0.02),
    }


def reference(x_prompt, x_sample, state_mlstm_c, state_mlstm_n, state_mlstm_m, state_ssd, state_ssd_conv,
              state_gla, state_ffn_conv, norm1_g, w_in, mlstm_b_i, mlstm_b_f, mlstm_norm_g, ssd_conv_w,
              ssd_conv_b, ssd_dt_bias, ssd_a_log, ssd_d, ssd_norm_g, gla_w_f2, gla_b_f, gla_norm_g, w_out,
              norm2_g, w_up, ffn_conv_w, ffn_conv_b, w_down, final_norm_g):
    xp, xs = x_prompt, x_sample
    p_states, s_states = [], []
    for i in range(DEPTH):
        lp = (norm1_g[i], w_in[i], mlstm_b_i[i], mlstm_b_f[i], mlstm_norm_g[i], ssd_conv_w[i], ssd_conv_b[i],
              ssd_dt_bias[i], ssd_a_log[i], ssd_d[i], ssd_norm_g[i], gla_w_f2[i], gla_b_f[i], gla_norm_g[i],
              w_out[i], norm2_g[i], w_up[i], ffn_conv_w[i], ffn_conv_b[i], w_down[i])
        xp, st_p = layer(xp, zero_state(xp.shape[0], xp.dtype), lp)
        s_in = (state_mlstm_c[i], state_mlstm_n[i], state_mlstm_m[i], state_ssd[i], state_ssd_conv[i],
                state_gla[i], state_ffn_conv[i])
        xs, st_s = layer(xs, s_in, lp)
        p_states.append(st_p)
        s_states.append(st_s)
    y_prompt = rmsnorm(xp, final_norm_g)
    y_sample = rmsnorm(xs, final_norm_g)

    def stk(sts, j):
        return jnp.stack([s[j] for s in sts], axis=0)

    return (y_prompt, y_sample,
            stk(p_states, 0), stk(p_states, 1), stk(p_states, 2), stk(p_states, 3), stk(p_states, 4), stk(p_states, 5), stk(p_states, 6),
            stk(s_states, 0), stk(s_states, 1), stk(s_states, 2), stk(s_states, 3), stk(s_states, 4), stk(s_states, 5), stk(s_states, 6))
```

```python
import functools

import jax
import jax.numpy as jnp
from jax import lax
from jax.experimental import pallas as pl
from jax.experimental.pallas import tpu as pltpu

F32 = jnp.float32
BF16 = jnp.bfloat16
I32 = jnp.int32

D_MODEL = 1024
DEPTH = 4
ML_HEADS = 4
ML_DK = 64
SSD_HEADS = 8
SSD_W = 512
SSD_CONV = 4
SSD_CONV_DIM = 768
GLA_HEADS = 4
GLA_DK = 32
GLA_RANK = 16
GLA_TAU = 16.0
D_FF = 2816
FFN_CONV = 3
EPS = 1e-6
IN_SIZES = (256, 256, 256, 256, 4, 4, 512, 768, 8, 128, 128, 256, 256, 16)

MQ, MK, MV, MO, SZ, SXBC, GQ, GK, GV, GG, GT, NP = (
    0, 256, 512, 1024, 1280, 1792, 2560, 2688, 2816, 3072, 3328, 3456)
GATE_LI, GATE_LF, GATE_DT, GATE_GF = 0, 4, 8, 16

LANE = 128
SUBLANE = 8
MIX_T = 128
GLA_C = 16
FF_CHUNK = 256
N_FF_CHUNK = D_FF // FF_CHUNK
VMEM_LIMIT = 56 * 1024 * 1024


def _cparams(*sem):
    return pltpu.CompilerParams(dimension_semantics=sem if sem else None,
                                vmem_limit_bytes=VMEM_LIMIT)


def _const_spec(shape):
    nd = len(shape)
    return pl.BlockSpec(shape, lambda *_: (0,) * nd)


def _sigmoid(x):
    return 1.0 / (1.0 + jnp.exp(-x))


def _softplus(x):
    return jnp.maximum(x, 0.0) + jnp.log1p(jnp.exp(-jnp.abs(x)))


def _log_sigmoid(x):
    return -_softplus(-x)


def _rmsnorm_rows(x, g):
    ms = jnp.mean(x * x, axis=-1, keepdims=True)
    return x * lax.rsqrt(ms + EPS) * g


def _dot(a, b):
    return jnp.dot(a, b, preferred_element_type=F32)


def _dot_nt(a, b):
    return lax.dot_general(a, b, (((1,), (1,)), ((), ())), preferred_element_type=F32)


def _dot_tn(a, b):
    return lax.dot_general(a, b, (((0,), (0,)), ((), ())), preferred_element_type=F32)


def _split3(x):
    x1 = x.astype(BF16)
    r1 = x - x1.astype(F32)
    x2 = r1.astype(BF16)
    x3 = (r1 - x2.astype(F32)).astype(BF16)
    return x1, x2, x3


def _cumsum_rows(x, tri):
    c = _dot(tri, jnp.concatenate(_split3(x), axis=1))
    return c[:, :LANE] + c[:, LANE:2 * LANE] + c[:, 2 * LANE:]


def _shift_rows(x, k, carry):
    r = pltpu.roll(x, k, 0)
    c = pltpu.roll(carry, k, 0)
    row = lax.broadcasted_iota(I32, c.shape, 0)
    head = jnp.where(row < k, c, r[:SUBLANE])
    return jnp.concatenate([head, r[SUBLANE:]], axis=0)


def _seg_rmsnorm(x, seg):
    w = x.shape[1]
    if seg % LANE == 0:
        parts = []
        for s0 in range(0, w, seg):
            xs = x[:, s0:s0 + seg]
            parts.append(xs * lax.rsqrt(jnp.mean(xs * xs, axis=-1, keepdims=True) + EPS))
        return jnp.concatenate(parts, axis=1)
    parts = []
    lane = lax.broadcasted_iota(I32, (x.shape[0], LANE), 1)
    for s0 in range(0, w, LANE):
        xs = x[:, s0:s0 + LANE]
        sq = xs * xs
        lo = jnp.sum(jnp.where(lane < seg, sq, 0.0), axis=-1, keepdims=True)
        hi = jnp.sum(jnp.where(lane >= seg, sq, 0.0), axis=-1, keepdims=True)
        r = jnp.where(lane < seg, lax.rsqrt(lo / seg + EPS), lax.rsqrt(hi / seg + EPS))
        parts.append(xs * r)
    return jnp.concatenate(parts, axis=1)


def _norm_proj_kernel(x_ref, g_ref, w_ref, o_ref):
    h = _rmsnorm_rows(x_ref[...], g_ref[...]).astype(BF16)
    for n0 in range(0, NP, 512):
        n1 = min(n0 + 512, NP)
        o_ref[:, n0:n1] = _dot(h, w_ref[:, n0:n1])


def _norm_proj(x2d, g, w):
    m = x2d.shape[0]
    tm = min(512, m)
    return pl.pallas_call(
        _norm_proj_kernel,
        grid=(m // tm,),
        in_specs=[pl.BlockSpec((tm, D_MODEL), lambda i: (i, 0)),
                  _const_spec((1, D_MODEL)),
                  _const_spec((D_MODEL, NP))],
        out_specs=pl.BlockSpec((tm, NP), lambda i: (i, 0)),
        out_shape=jax.ShapeDtypeStruct((m, NP), F32),
        compiler_params=_cparams("arbitrary"),
        name="norm_proj",
    )(x2d, g, w)


def _mixer_kernel(proj_ref, bias_ref, alog_ref, mlg_ref, cw_ref, dskip_ref, ssdg_ref, wf2_ref,
                  bf_ref, glag_ref,
                  mix_ref, caug_ref, m_ref, ssd_ref, gla_ref,
                  conv_s, cum_s):
    T = MIX_T
    t_idx = pl.program_id(1)

    @pl.when(t_idx == 0)
    def _():
        caug_ref[...] = jnp.zeros_like(caug_ref)
        m_ref[...] = jnp.zeros_like(m_ref)
        ssd_ref[...] = jnp.zeros_like(ssd_ref)
        gla_ref[...] = jnp.zeros_like(gla_ref)
        conv_s[...] = jnp.zeros_like(conv_s)

    lane = lax.broadcasted_iota(I32, (T, LANE), 1)
    rowi = lax.broadcasted_iota(I32, (T, T), 0)
    coli = lax.broadcasted_iota(I32, (T, T), 1)
    causal = rowi >= coli
    tri = jnp.where(causal, 1.0, 0.0).astype(BF16)
    row128 = lax.broadcasted_iota(I32, (LANE, LANE), 0)
    neg_inf = jnp.float32(-jnp.inf)

    g_blk = proj_ref[:, GT:GT + LANE] + bias_ref[...]
    lfs = _log_sigmoid(g_blk)
    dt = _softplus(g_blk)
    a_row = -jnp.exp(alog_ref[...])
    is_lf = (lane >= GATE_LF) & (lane < GATE_DT)
    is_dt = (lane >= GATE_DT) & (lane < GATE_GF)
    cum_g = _cumsum_rows(jnp.where(is_lf, lfs, jnp.where(is_dt, dt * a_row, 0.0)), tri)
    p1_t = jnp.where(lane < GATE_LF, g_blk, cum_g).T
    dt_t = dt.T

    e64 = jnp.where(lane == 64, 1.0, 0.0)
    m_row = m_ref[...]
    m_new_row = m_row
    ml_parts = []
    for h in range(ML_HEADS):
        pair, lo = h // 2, 64 * (h % 2)
        hm = (lane >= lo) & (lane < lo + 64)
        qm = jnp.where(hm, proj_ref[:, MQ + LANE * pair:MQ + LANE * (pair + 1)] * 0.125, 0.0).astype(BF16)
        km = jnp.where(hm, proj_ref[:, MK + LANE * pair:MK + LANE * (pair + 1)], 0.0)
        vaug = (proj_ref[:, MV + LANE * h:MV + LANE * (h + 1)] + e64).astype(BF16)
        b_col = cum_g[:, GATE_LF + h:GATE_LF + h + 1]
        b_row = p1_t[GATE_LF + h:GATE_LF + h + 1, :]
        li_row = p1_t[GATE_LI + h:GATE_LI + h + 1, :]
        li_col = g_blk[:, GATE_LI + h:GATE_LI + h + 1]
        m_prev = m_row[:, h:h + 1]
        d_mat = jnp.where(causal, b_col - b_row + li_row, neg_inf)
        a_col = b_col + m_prev
        mt = jnp.maximum(a_col, jnp.max(d_mat, axis=1, keepdims=True))
        s_mat = (_dot_nt(qm, km.astype(BF16)) * jnp.exp(d_mat - mt)).astype(BF16)
        c_pair = caug_ref[LANE * pair:LANE * (pair + 1), :]
        numaug = _dot(s_mat, vaug) + jnp.exp(a_col - mt) * _dot(qm, c_pair.astype(BF16))
        den = numaug[:, 64:65]
        hn = numaug / jnp.maximum(jnp.abs(den), jnp.exp(-mt))
        hv = jnp.where(lane < 64, hn, 0.0)
        ml_parts.append(hv * lax.rsqrt(jnp.sum(hv * hv, axis=-1, keepdims=True) / 64.0 + EPS))
        b_last = b_col[T - 1:T, :]
        m_new = mt[T - 1:T, :]
        g_state = jnp.exp(b_last + m_prev - m_new)
        g_s = jnp.exp(b_last - b_col + li_col - m_new)
        u = _dot_tn((km * g_s).astype(BF16), vaug)
        in_head = (row128 >= lo) & (row128 < lo + 64)
        caug_ref[LANE * pair:LANE * (pair + 1), :] = jnp.where(in_head, g_state * c_pair + u, c_pair)
        m_new_row = jnp.where(lax.broadcasted_iota(I32, m_row.shape, 1) == h, m_new, m_new_row)
    m_ref[...] = m_new_row
    y_ml = jnp.concatenate(
        [jnp.where(lane < 64, ml_parts[2 * p], pltpu.roll(ml_parts[2 * p + 1], 64, 1)) for p in range(2)], axis=1)
    mix_ref[:, 0:256] = (_sigmoid(proj_ref[:, MO:MO + 256]) * (y_ml * mlg_ref[...])).astype(BF16)

    raw = proj_ref[:, SXBC:SXBC + SSD_CONV_DIM]
    carry = conv_s[...]
    conv = (cw_ref[4:5, :] + _shift_rows(raw, 3, carry) * cw_ref[0:1, :] + _shift_rows(raw, 2, carry) * cw_ref[1:2, :]
            + _shift_rows(raw, 1, carry) * cw_ref[2:3, :] + raw * cw_ref[3:4, :])
    conv_s[...] = raw[T - SUBLANE:T, :]
    xbc = conv * _sigmoid(conv)
    b_p = xbc[:, SSD_W:SSD_W + LANE]
    c_p = xbc[:, SSD_W + LANE:SSD_W + 2 * LANE]
    c_pb = c_p.astype(BF16)
    cb = []
    for g in range(2):
        in_g = (lane >= 64 * g) & (lane < 64 * (g + 1))
        cb.append(_dot_nt(jnp.where(in_g, c_p, 0.0).astype(BF16), b_p.astype(BF16)))
    ys_parts = []
    for j in range(SSD_HEADS // 2):
        g = j // 2
        x_pair = xbc[:, LANE * j:LANE * (j + 1)]
        s_pair = ssd_ref[LANE * j:LANE * (j + 1), :]
        y_pair = None
        ecs, wcol, dec = [], [], []
        for hh in range(2):
            h = 2 * j + hh
            cs_col = cum_g[:, GATE_DT + h:GATE_DT + h + 1]
            cs_row = p1_t[GATE_DT + h:GATE_DT + h + 1, :]
            dt_row = dt_t[GATE_DT + h:GATE_DT + h + 1, :]
            dt_col = dt[:, GATE_DT + h:GATE_DT + h + 1]
            l_mat = jnp.exp(jnp.where(causal, cs_col - cs_row, neg_inf))
            sc = (cb[g] * l_mat * dt_row).astype(BF16)
            in_h = (lane >= 64 * hh) & (lane < 64 * (hh + 1))
            part = _dot(sc, jnp.where(in_h, x_pair, 0.0).astype(BF16))
            y_pair = part if y_pair is None else y_pair + part
            cs_last = cs_col[T - 1:T, :]
            ecs.append(jnp.exp(cs_col))
            wcol.append(jnp.exp(cs_last - cs_col) * dt_col)
            dec.append(jnp.exp(cs_last))
        y_pair = y_pair + jnp.where(lane < 64, ecs[0], ecs[1]) * _dot_nt(c_pb, s_pair.astype(BF16))
        ys_parts.append(y_pair)
        wx = (x_pair * jnp.where(lane < 64, wcol[0], wcol[1])).astype(BF16)
        in_g = (lane >= 64 * g) & (lane < 64 * (g + 1))
        u = _dot_tn(wx, jnp.where(in_g, b_p, 0.0).astype(BF16))
        ssd_ref[LANE * j:LANE * (j + 1), :] = jnp.where(row128 < 64, dec[0], dec[1]) * s_pair + u
    y_s = jnp.concatenate(ys_parts, axis=1) + dskip_ref[...] * xbc[:, :SSD_W]
    z = proj_ref[:, SZ:SZ + SSD_W]
    y_ssd = _seg_rmsnorm(y_s * (z * _sigmoid(z)), 256) * ssdg_ref[...]
    mix_ref[:, 256:768] = y_ssd.astype(BF16)

    la = _log_sigmoid(_dot(g_blk.astype(BF16), wf2_ref[...]) + bf_ref[...]) / GLA_TAU
    cum = _cumsum_rows(la, tri)
    cum_s[...] = cum
    cum_t = cum.T
    q = proj_ref[:, GQ:GQ + LANE] * (GLA_DK ** -0.5)
    k = proj_ref[:, GK:GK + LANE]
    v = proj_ref[:, GV:GV + 256]
    vb = v.astype(BF16)
    s_gla = gla_ref[...]
    o = _dot((q * jnp.exp(cum)).astype(BF16), s_gla.astype(BF16))
    u = _dot_tn((k * jnp.exp(cum[T - 1:T, :] - cum)).astype(BF16), vb)
    r_s = lax.broadcasted_iota(I32, (LANE, 256), 0)
    c_s = lax.broadcasted_iota(I32, (LANE, 256), 1)
    same_head = (r_s >> 5) == (c_s >> 6)
    gla_ref[...] = jnp.exp(cum_t[:, T - 1:T]) * s_gla + jnp.where(same_head, u, 0.0)
    tpos = lax.broadcasted_iota(I32, (T, LANE), 0)
    att = [None] * GLA_HEADS
    w = GLA_C
    while w < T:
        is_t = (tpos & (2 * w - 1)) >= w
        mid = jnp.concatenate(
            [jnp.broadcast_to(cum_s[2 * w * blk + w - 1:2 * w * blk + w, :], (2 * w, LANE))
             for blk in range(T // (2 * w))], axis=0)
        e = jnp.exp(jnp.where(is_t, cum - mid, mid - cum))
        ql = jnp.where(is_t, q * e, 0.0)
        kl = jnp.where(is_t, 0.0, k * e).astype(BF16)
        shift = (2 * w).bit_length() - 1
        same_blk = (rowi >> shift) == (coli >> shift)
        for h in range(GLA_HEADS):
            in_h = (lane >= GLA_DK * h) & (lane < GLA_DK * (h + 1))
            a = jnp.where(same_blk, _dot_nt(jnp.where(in_h, ql, 0.0).astype(BF16), kl), 0.0)
            att[h] = a if att[h] is None else att[h] + a
        w *= 2
    o_parts = []
    for p in range(2):
        v_pair = v[:, LANE * p:LANE * (p + 1)]
        o_parts.append(_dot(att[2 * p].astype(BF16), jnp.where(lane < 64, v_pair, 0.0).astype(BF16))
                       + _dot(att[2 * p + 1].astype(BF16), jnp.where(lane >= 64, v_pair, 0.0).astype(BF16)))
    o = o + jnp.concatenate(o_parts, axis=1)
    expand = jnp.where(same_head, 1.0, 0.0).astype(BF16)
    nblk = T // GLA_C
    for jj in range(GLA_C):
        def rows(ref, c0, c1):
            return jnp.concatenate(
                [jnp.broadcast_to(ref[GLA_C * i + jj:GLA_C * i + jj + 1, c0:c1], (GLA_C, c1 - c0))
                 for i in range(nblk)], axis=0)
        k_s = rows(proj_ref, GK, GK + LANE)
        c_srow = rows(cum_s, 0, LANE)
        v_s = rows(proj_ref, GV, GV + 256)
        valid = (tpos & (GLA_C - 1)) >= jj
        e = jnp.exp(jnp.where(valid, cum - c_srow, neg_inf))
        o = o + _dot((q * k_s * e).astype(BF16), expand) * v_s
    gg = proj_ref[:, GG:GG + 256]
    mix_ref[:, 768:1024] = (_seg_rmsnorm(o, 64) * glag_ref[...] * (gg * _sigmoid(gg))).astype(BF16)


def _mixer(proj, bsz, seq, bias_row, alog_row, mlg, cwb, dskip, ssdg, wf2p, bfr, glag):
    nt = seq // MIX_T
    row_spec = lambda w: _const_spec((1, w))
    return pl.pallas_call(
        _mixer_kernel,
        grid=(bsz, nt),
        in_specs=[pl.BlockSpec((MIX_T, NP), lambda b, t: (b * nt + t, 0)),
                  row_spec(LANE), row_spec(LANE), row_spec(256),
                  _const_spec((SUBLANE, SSD_CONV_DIM)), row_spec(SSD_W), row_spec(SSD_W),
                  _const_spec((LANE, LANE)), row_spec(LANE), row_spec(256)],
        out_specs=[pl.BlockSpec((MIX_T, D_MODEL), lambda b, t: (b * nt + t, 0)),
                   pl.BlockSpec((None, 256, LANE), lambda b, t: (b, 0, 0)),
                   pl.BlockSpec((None, 1, LANE), lambda b, t: (b, 0, 0)),
                   pl.BlockSpec((None, 512, LANE), lambda b, t: (b, 0, 0)),
                   pl.BlockSpec((None, LANE, 256), lambda b, t: (b, 0, 0))],
        out_shape=[jax.ShapeDtypeStruct((bsz * seq, D_MODEL), BF16),
                   jax.ShapeDtypeStruct((bsz, 256, LANE), F32),
                   jax.ShapeDtypeStruct((bsz, 1, LANE), F32),
                   jax.ShapeDtypeStruct((bsz, 512, LANE), F32),
                   jax.ShapeDtypeStruct((bsz, LANE, 256), F32)],
        scratch_shapes=[pltpu.VMEM((SUBLANE, SSD_CONV_DIM), F32), pltpu.VMEM((MIX_T, LANE), F32)],
        compiler_params=_cparams("arbitrary", "arbitrary"),
        name="mixer",
    )(proj, bias_row, alog_row, mlg, cwb, dskip, ssdg, wf2p, bfr, glag)


def _ffn_chunks(h2_s, wup_ref, cwb_ref, wd_ref, o_ref, prev_fn, keep_fn):
    def chunk(c, carry):
        h2 = h2_s[...]
        halves = []
        for ug in range(2):
            up = _dot(h2, wup_ref[ug, c])
            x1, x2 = prev_fn(ug, c, up)
            keep_fn(ug, c, up)
            cwb = cwb_ref[ug, c]
            halves.append(cwb[3:4, :] + x2 * cwb[0:1, :] + x1 * cwb[1:2, :] + up * cwb[2:3, :])
        u, gate = halves
        act = (gate * _sigmoid(gate) * u).astype(BF16)
        o_ref[...] += _dot(act, wd_ref[c])
        return carry
    lax.fori_loop(0, N_FF_CHUNK, chunk, 0)


def _out_ffn_seq_kernel(x_ref, mix_ref, wo_ref, g2_ref, wup_ref, cwb_ref, wd_ref, o_ref, st_ref, h2_s):
    @pl.when(pl.program_id(1) == 0)
    def _():
        st_ref[...] = jnp.zeros_like(st_ref)

    xn = x_ref[...] + _dot(mix_ref[...], wo_ref[...])
    o_ref[...] = xn
    h2_s[...] = _rmsnorm_rows(xn, g2_ref[...]).astype(BF16)
    tm = x_ref.shape[0]

    def prev_fn(ug, c, up):
        carry = st_ref[ug, c]
        return _shift_rows(up, 1, carry), _shift_rows(up, 2, carry)

    def keep_fn(ug, c, up):
        st_ref[ug, c] = up[tm - SUBLANE:tm, :]

    _ffn_chunks(h2_s, wup_ref, cwb_ref, wd_ref, o_ref, prev_fn, keep_fn)


def _out_ffn_step_kernel(x_ref, mix_ref, wo_ref, g2_ref, wup_ref, cwb_ref, wd_ref, prev_ref, o_ref, up_ref, h2_s):
    xn = x_ref[...] + _dot(mix_ref[...], wo_ref[...])
    o_ref[...] = xn
    h2_s[...] = _rmsnorm_rows(xn, g2_ref[...]).astype(BF16)

    def prev_fn(ug, c, up):
        return prev_ref[1, ug, c], prev_ref[0, ug, c]

    def keep_fn(ug, c, up):
        up_ref[ug, c] = up

    _ffn_chunks(h2_s, wup_ref, cwb_ref, wd_ref, o_ref, prev_fn, keep_fn)


_FFN_W_SPECS = [_const_spec((D_MODEL, D_MODEL)), _const_spec((1, D_MODEL)),
                _const_spec((2, N_FF_CHUNK, D_MODEL, FF_CHUNK)),
                _const_spec((2, N_FF_CHUNK, SUBLANE, FF_CHUNK)),
                _const_spec((N_FF_CHUNK, FF_CHUNK, D_MODEL))]


def _out_ffn_seq(x2d, mix, bsz, seq, wo, g2, wup, cwb, wd):
    tm = min(512, seq)
    nt = seq // tm
    return pl.pallas_call(
        _out_ffn_seq_kernel,
        grid=(bsz, nt),
        in_specs=[pl.BlockSpec((tm, D_MODEL), lambda b, t: (b * nt + t, 0)),
                  pl.BlockSpec((tm, D_MODEL), lambda b, t: (b * nt + t, 0))] + _FFN_W_SPECS,
        out_specs=[pl.BlockSpec((tm, D_MODEL), lambda b, t: (b * nt + t, 0)),
                   pl.BlockSpec((None, 2, N_FF_CHUNK, SUBLANE, FF_CHUNK), lambda b, t: (b, 0, 0, 0, 0))],
        out_shape=[jax.ShapeDtypeStruct((bsz * seq, D_MODEL), F32),
                   jax.ShapeDtypeStruct((bsz, 2, N_FF_CHUNK, SUBLANE, FF_CHUNK), F32)],
        scratch_shapes=[pltpu.VMEM((tm, D_MODEL), BF16)],
        compiler_params=_cparams("arbitrary", "arbitrary"),
        name="out_ffn_seq",
    )(x2d, mix, wo, g2, wup, cwb, wd)


def _out_ffn_step(x2d, mix, wo, g2, wup, cwb, wd, prev):
    m = x2d.shape[0]
    return pl.pallas_call(
        _out_ffn_step_kernel,
        grid=(1,),
        in_specs=[_const_spec((m, D_MODEL)), _const_spec((m, D_MODEL))] + _FFN_W_SPECS
        + [_const_spec((2, 2, N_FF_CHUNK, m, FF_CHUNK))],
        out_specs=[_const_spec((m, D_MODEL)), _const_spec((2, N_FF_CHUNK, m, FF_CHUNK))],
        out_shape=[jax.ShapeDtypeStruct((m, D_MODEL), F32),
                   jax.ShapeDtypeStruct((2, N_FF_CHUNK, m, FF_CHUNK), F32)],
        scratch_shapes=[pltpu.VMEM((m, D_MODEL), BF16)],
        compiler_params=_cparams("arbitrary"),
        name="out_ffn_step",
    )(x2d, mix, wo, g2, wup, cwb, wd, prev)


def _final_norm_kernel(x_ref, g_ref, o_ref):
    o_ref[...] = _rmsnorm_rows(x_ref[...], g_ref[...])


def _final_norm(x2d, g):
    m = x2d.shape[0]
    tm = min(1024, m)
    return pl.pallas_call(
        _final_norm_kernel,
        grid=(m // tm,),
        in_specs=[pl.BlockSpec((tm, D_MODEL), lambda i: (i, 0)), _const_spec((1, D_MODEL))],
        out_specs=pl.BlockSpec((tm, D_MODEL), lambda i: (i, 0)),
        out_shape=jax.ShapeDtypeStruct((m, D_MODEL), F32),
        compiler_params=_cparams("arbitrary"),
        name="final_norm",
    )(x2d, g)


def _hi_lo(x):
    hi = x.astype(BF16)
    return hi, (x - hi.astype(F32)).astype(BF16)


def _repeat_matrix(src0, per):
    r = lax.broadcasted_iota(I32, (LANE, LANE), 0)
    c = lax.broadcasted_iota(I32, (LANE, LANE), 1)
    return jnp.where(r == src0 + (c >> (per.bit_length() - 1)), 1.0, 0.0).astype(BF16)


def _step_mlstm_kernel(proj_ref, bias_ref, mlg_ref, c_ref, n_ref, m_ref, c_out, n_out, m_out, y_ref):
    bsz = proj_ref.shape[0]
    lane = lax.broadcasted_iota(I32, (bsz, LANE), 1)
    g_blk = proj_ref[:, GT:GT + LANE] + bias_ref[...]
    m_all = m_ref[...]
    m_new_all = m_all
    y_parts, n_parts = [], []
    for h in range(ML_HEADS):
        pair, lo = h // 2, 64 * (h % 2)
        hm = (lane >= lo) & (lane < lo + 64)
        li = g_blk[:, GATE_LI + h:GATE_LI + h + 1]
        lf = _log_sigmoid(g_blk[:, GATE_LF + h:GATE_LF + h + 1])
        m0 = m_all[:, h:h + 1]
        mt = jnp.maximum(lf + m0, li)
        w_old = jnp.exp(lf + m0 - mt)
        w_in = jnp.exp(li - mt)
        qp = proj_ref[:, MQ + LANE * pair:MQ + LANE * (pair + 1)] * 0.125
        kp = proj_ref[:, MK + LANE * pair:MK + LANE * (pair + 1)]
        vs = proj_ref[:, MV + LANE * h:MV + LANE * (h + 1)]
        vt = vs + pltpu.roll(vs, 64, 1)
        stack = jnp.concatenate(_hi_lo(qp) + _hi_lo(kp), axis=0)
        acc = jnp.zeros((bsz, LANE), F32)
        for j in range(ML_DK // 2):
            ex = _dot(stack, _repeat_matrix(lo + 2 * j, 64))
            qe = ex[0:bsz] + ex[bsz:2 * bsz]
            ke = ex[2 * bsz:3 * bsz] + ex[3 * bsz:4 * bsz]
            c0 = h * ML_DK * 64 + LANE * j
            cn = w_old * c_ref[:, c0:c0 + LANE] + (w_in * ke) * vt
            c_out[:, c0:c0 + LANE] = cn
            acc = acc + qe * cn
        num = acc + pltpu.roll(acc, 64, 1)
        nn = w_old * n_ref[:, LANE * pair:LANE * (pair + 1)] + w_in * kp
        den = jnp.sum(jnp.where(hm, qp * nn, 0.0), axis=-1, keepdims=True)
        hn = num / jnp.maximum(jnp.abs(den), jnp.exp(-mt))
        hv = jnp.where(lane < 64, hn, 0.0)
        y_parts.append(hv * lax.rsqrt(jnp.sum(hv * hv, axis=-1, keepdims=True) / 64.0 + EPS))
        n_parts.append(jnp.where(hm, nn, 0.0))
        m_new_all = jnp.where(lane == h, mt, m_new_all)
    m_out[...] = m_new_all
    n_out[...] = jnp.concatenate([n_parts[0] + n_parts[1], n_parts[2] + n_parts[3]], axis=1)
    y_ml = jnp.concatenate(
        [jnp.where(lane < 64, y_parts[2 * p], pltpu.roll(y_parts[2 * p + 1], 64, 1)) for p in range(2)], axis=1)
    y_ref[...] = _sigmoid(proj_ref[:, MO:MO + 256]) * (y_ml * mlg_ref[...])


def _step_mlstm(proj, bias_row, mlg, c2, n2, mpad):
    bsz = proj.shape[0]
    shapes = [(bsz, ML_HEADS * ML_DK * 64), (bsz, 256), (bsz, LANE), (bsz, 256)]
    return pl.pallas_call(
        _step_mlstm_kernel,
        grid=(1,),
        in_specs=[_const_spec((bsz, NP)), _const_spec((1, LANE)), _const_spec((1, 256)),
                  _const_spec(shapes[0]), _const_spec(shapes[1]), _const_spec(shapes[2])],
        out_specs=[_const_spec(s) for s in shapes],
        out_shape=[jax.ShapeDtypeStruct(s, F32) for s in shapes],
        compiler_params=_cparams("arbitrary"),
        name="step_mlstm",
    )(proj, bias_row, mlg, c2, n2, mpad)


def _step_ssd_kernel(gate_ref, x_ref, z_ref, bc_ref, buf_x_ref, buf_bc_ref, bias_ref, alog_ref,
                     cwx_ref, cwbc_ref, dskip_ref, ssdg_ref, s_ref, s_out, y_ref):
    g = pl.program_id(0)
    bsz = x_ref.shape[0]
    lane = lax.broadcasted_iota(I32, (bsz, LANE), 1)
    lane1 = lax.broadcasted_iota(I32, (1, LANE), 1)
    g_blk = gate_ref[...] + bias_ref[...]
    dt_all = _softplus(g_blk)
    a_row = -jnp.exp(alog_ref[...])

    def conv4(cur, buf_ref, cw_ref):
        wdt = cur.shape[1]
        acc = cw_ref[4:5, :] + cur * cw_ref[3:4, :]
        for j in range(SSD_CONV - 1):
            acc = acc + buf_ref[:, wdt * j:wdt * (j + 1)] * cw_ref[j:j + 1, :]
        return acc * _sigmoid(acc)

    xc = conv4(x_ref[...], buf_x_ref, cwx_ref)
    bc = conv4(bc_ref[...], buf_bc_ref, cwbc_ref)
    b_p, c_p = bc[:, :LANE], bc[:, LANE:]
    b_al = jnp.where(g == 0, b_p, pltpu.roll(b_p, 64, 1))
    c_al = jnp.where(g == 0, c_p, pltpu.roll(c_p, 64, 1))
    b_t = jnp.where(lane < 64, b_al, pltpu.roll(b_al, 64, 1))
    c_t = jnp.where(lane < 64, c_al, pltpu.roll(c_al, 64, 1))
    rr = lax.broadcasted_iota(I32, (LANE, LANE), 0)
    cc = lax.broadcasted_iota(I32, (LANE, LANE), 1)
    ys = []
    for p in range(2):
        x_pair = xc[:, LANE * p:LANE * (p + 1)]
        y_pair = jnp.zeros((bsz, LANE), F32)
        for hh2 in range(2):
            hh = 2 * p + hh2
            lo = 64 * hh2
            sel = lane == GATE_DT + 4 * g + hh
            dt_col = jnp.sum(jnp.where(sel, dt_all, 0.0), axis=-1, keepdims=True)
            a_h = jnp.sum(jnp.where(lane1 == GATE_DT + 4 * g + hh, a_row, 0.0), axis=-1, keepdims=True)
            d_a = jnp.exp(dt_col * a_h)
            stack = jnp.concatenate(_hi_lo(dt_col * x_pair), axis=0)
            for j in range(32):
                ue = _dot(stack, _repeat_matrix(lo + 2 * j, 64))
                c0 = hh * 4096 + LANE * j
                sn = d_a * s_ref[:, c0:c0 + LANE] + (ue[0:bsz] + ue[bsz:2 * bsz]) * b_t
                s_out[:, c0:c0 + LANE] = sn
                place = jnp.where(cc == lo + 2 * j + (rr >> 6), 1.0, 0.0).astype(BF16)
                yy = _dot(jnp.concatenate(_hi_lo(sn * c_t), axis=0), place)
                y_pair = y_pair + yy[0:bsz] + yy[bsz:2 * bsz]
        ys.append(y_pair)
    y_s = jnp.concatenate(ys, axis=1) + dskip_ref[...] * xc
    z = z_ref[...]
    y_ref[...] = _seg_rmsnorm(y_s * (z * _sigmoid(z)), 256) * ssdg_ref[...]


def _step_ssd(proj, buf2d, bias_row, alog_row, cwb, dskip, ssdg, s2):
    bsz = proj.shape[0]
    gw = 256
    return pl.pallas_call(
        _step_ssd_kernel,
        grid=(2,),
        in_specs=[pl.BlockSpec((bsz, LANE), lambda g: (0, GT // LANE)),
                  pl.BlockSpec((bsz, gw), lambda g: (0, SXBC // gw + g)),
                  pl.BlockSpec((bsz, gw), lambda g: (0, SZ // gw + g)),
                  pl.BlockSpec((bsz, gw), lambda g: (0, (SXBC + SSD_W) // gw)),
                  pl.BlockSpec((bsz, 3 * gw), lambda g: (0, g)),
                  pl.BlockSpec((bsz, 3 * gw), lambda g: (0, 2)),
                  _const_spec((1, LANE)), _const_spec((1, LANE)),
                  pl.BlockSpec((SUBLANE, gw), lambda g: (0, g)),
                  pl.BlockSpec((SUBLANE, gw), lambda g: (0, 2)),
                  pl.BlockSpec((1, gw), lambda g: (0, g)),
                  pl.BlockSpec((1, gw), lambda g: (0, g)),
                  pl.BlockSpec((bsz, 4 * 4096), lambda g: (0, g))],
        out_specs=[pl.BlockSpec((bsz, 4 * 4096), lambda g: (0, g)),
                   pl.BlockSpec((bsz, gw), lambda g: (0, g))],
        out_shape=[jax.ShapeDtypeStruct((bsz, SSD_HEADS * 4096), F32),
                   jax.ShapeDtypeStruct((bsz, SSD_W), F32)],
        compiler_params=_cparams("arbitrary"),
        name="step_ssd",
    )(proj, proj, proj, proj, buf2d, buf2d, bias_row, alog_row, cwb, cwb, dskip, ssdg, s2)


def _step_gla_kernel(proj_ref, wf2_ref, bf_ref, glag_ref, s_ref, s_out, y_ref):
    bsz = proj_ref.shape[0]
    lane = lax.broadcasted_iota(I32, (bsz, LANE), 1)
    la = _log_sigmoid(_dot(proj_ref[:, GT:GT + LANE].astype(BF16), wf2_ref[...]) + bf_ref[...]) / GLA_TAU
    q = proj_ref[:, GQ:GQ + LANE] * (GLA_DK ** -0.5)
    k = proj_ref[:, GK:GK + LANE]
    stack = jnp.concatenate(_hi_lo(q) + _hi_lo(k) + _hi_lo(jnp.exp(la)), axis=0)
    o_parts = []
    for h in range(GLA_HEADS):
        v_pair = proj_ref[:, GV + LANE * (h // 2):GV + LANE * (h // 2 + 1)]
        v_al = v_pair if h % 2 == 0 else pltpu.roll(v_pair, 64, 1)
        v_t = jnp.where(lane < 64, v_al, pltpu.roll(v_al, 64, 1))
        acc = jnp.zeros((bsz, LANE), F32)
        for j in range(GLA_DK // 2):
            ex = _dot(stack, _repeat_matrix(GLA_DK * h + 2 * j, 64))
            qe = ex[0:bsz] + ex[bsz:2 * bsz]
            ke = ex[2 * bsz:3 * bsz] + ex[3 * bsz:4 * bsz]
            de = ex[4 * bsz:5 * bsz] + ex[5 * bsz:6 * bsz]
            c0 = h * GLA_DK * 64 + LANE * j
            sn = de * s_ref[:, c0:c0 + LANE] + ke * v_t
            s_out[:, c0:c0 + LANE] = sn
            acc = acc + qe * sn
        o_parts.append(jnp.where(lane < 64, acc + pltpu.roll(acc, 64, 1), 0.0))
    o = jnp.concatenate(
        [jnp.where(lane < 64, o_parts[2 * p], pltpu.roll(o_parts[2 * p + 1], 64, 1)) for p in range(2)], axis=1)
    gg = proj_ref[:, GG:GG + 256]
    y_ref[...] = _seg_rmsnorm(o, 64) * glag_ref[...] * (gg * _sigmoid(gg))


def _step_gla(proj, wf2p, bfr, glag, s2):
    bsz = proj.shape[0]
    sw = GLA_HEADS * GLA_DK * 64
    return pl.pallas_call(
        _step_gla_kernel,
        grid=(1,),
        in_specs=[_const_spec((bsz, NP)), _const_spec((LANE, LANE)), _const_spec((1, LANE)),
                  _const_spec((1, 256)), _const_spec((bsz, sw))],
        out_specs=[_const_spec((bsz, sw)), _const_spec((bsz, 256))],
        out_shape=[jax.ShapeDtypeStruct((bsz, sw), F32), jax.ShapeDtypeStruct((bsz, 256), F32)],
        compiler_params=_cparams("arbitrary"),
        name="step_gla",
    )(proj, wf2p, bfr, glag, s2)


def _prep_w_in(w_in):
    d = w_in.shape[0]
    parts, acc = [], 0
    for s in IN_SIZES:
        parts.append(w_in[..., acc:acc + s])
        acc += s
    mq, mk, mv, mo, mi, mf, sz, sxbc, sdt, gq, gk, gv, gg, gf = parts
    lead = w_in.shape[:-1]
    mv_pad = jnp.pad(mv.reshape(lead + (ML_HEADS, 64)), [(0, 0)] * len(lead) + [(0, 0), (0, 64)])
    mv_pad = mv_pad.reshape(lead + (ML_HEADS * LANE,))
    gates = jnp.concatenate([mi, mf, sdt, gf, jnp.zeros(lead + (LANE - 32,), w_in.dtype)], axis=-1)
    del d
    return jnp.concatenate([mq, mk, mv_pad, mo, sz, sxbc, gq, gk, gv, gg, gates], axis=-1).astype(BF16)


def _pad_row(x, width):
    return jnp.pad(x, [(0, 0), (0, width - x.shape[-1])])


def kernel(x_prompt, x_sample, state_mlstm_c, state_mlstm_n, state_mlstm_m, state_ssd, state_ssd_conv,
           state_gla, state_ffn_conv, norm1_g, w_in, mlstm_b_i, mlstm_b_f, mlstm_norm_g, ssd_conv_w,
           ssd_conv_b, ssd_dt_bias, ssd_a_log, ssd_d, ssd_norm_g, gla_w_f2, gla_b_f, gla_norm_g, w_out,
           norm2_g, w_up, ffn_conv_w, ffn_conv_b, w_down, final_norm_g):
    bsz, seq, _ = x_prompt.shape
    dbs = x_sample.shape[0]
    depth = w_in.shape[0]

    w_in_p = _prep_w_in(w_in)
    w_out_b = w_out.astype(BF16)
    w_up_p = w_up.reshape(depth, D_MODEL, 2, N_FF_CHUNK, FF_CHUNK).transpose(0, 2, 3, 1, 4).astype(BF16)
    w_down_p = w_down.reshape(depth, N_FF_CHUNK, FF_CHUNK, D_MODEL).astype(BF16)
    ffn_cwb = jnp.concatenate([ffn_conv_w, ffn_conv_b[:, None, :],
                               jnp.zeros((depth, SUBLANE - FFN_CONV - 1, 2 * D_FF), F32)], axis=1)
    ffn_cwb = ffn_cwb.reshape(depth, SUBLANE, 2, N_FF_CHUNK, FF_CHUNK).transpose(0, 2, 3, 1, 4)
    zeros8 = jnp.zeros((depth, 8), F32)
    bias_rows = _pad_row(jnp.concatenate([mlstm_b_i, mlstm_b_f, ssd_dt_bias], axis=-1), LANE)[:, None, :]
    alog_rows = _pad_row(jnp.concatenate([zeros8, ssd_a_log], axis=-1), LANE)[:, None, :]
    ssd_cwb = jnp.concatenate([ssd_conv_w, ssd_conv_b[:, None, :],
                               jnp.zeros((depth, SUBLANE - SSD_CONV - 1, SSD_CONV_DIM), F32)], axis=1)
    dskip_rows = jnp.repeat(ssd_d, 64, axis=-1)[:, None, :]
    wf2_p = jnp.pad(gla_w_f2, [(0, 0), (GATE_GF, LANE - GATE_GF - GLA_RANK), (0, 0)]).astype(BF16)

    xp = x_prompt.reshape(bsz * seq, D_MODEL)
    xs = x_sample.reshape(dbs, D_MODEL)
    p_states, s_states = [], []
    for i in range(depth):
        g1 = norm1_g[i][None, :]
        g2 = norm2_g[i][None, :]
        mlg = mlstm_norm_g[i][None, :]
        ssdg = ssd_norm_g[i][None, :]
        glag = gla_norm_g[i][None, :]
        bfr = gla_b_f[i][None, :]

        proj = _norm_proj(xp, g1, w_in_p[i])
        mix, caug, m_o, ssd_o, gla_o = _mixer(proj, bsz, seq, bias_rows[i], alog_rows[i], mlg, ssd_cwb[i],
                                              dskip_rows[i], ssdg, wf2_p[i], bfr, glag)
        xp, ffn_st = _out_ffn_seq(xp, mix, bsz, seq, w_out_b[i], g2, w_up_p[i], ffn_cwb[i], w_down_p[i])
        caug4 = caug.reshape(bsz, ML_HEADS, ML_DK, LANE)
        ssd4 = ssd_o.reshape(bsz, SSD_HEADS, 64, LANE)
        gla4 = gla_o.reshape(bsz, GLA_HEADS, GLA_DK, 256)
        p_states.append((
            caug4[..., :64], caug4[..., 64], m_o[:, 0, :ML_HEADS],
            jnp.where(jnp.arange(SSD_HEADS)[None, :, None, None] < 4, ssd4[..., :64], ssd4[..., 64:]),
            proj.reshape(bsz, seq, NP)[:, seq - (SSD_CONV - 1):, SXBC:SXBC + SSD_CONV_DIM],
            jnp.stack([gla4[:, h, :, 64 * h:64 * (h + 1)] for h in range(GLA_HEADS)], axis=1),
            ffn_st[:, :, :, SUBLANE - (FFN_CONV - 1):, :].transpose(0, 3, 1, 2, 4).reshape(bsz, FFN_CONV - 1, 2 * D_FF),
        ))

        proj_s = _norm_proj(xs, g1, w_in_p[i])
        c_n, n_n, m_n, y_ml = _step_mlstm(proj_s, bias_rows[i], mlg,
                                          state_mlstm_c[i].reshape(dbs, -1), state_mlstm_n[i].reshape(dbs, -1),
                                          _pad_row(state_mlstm_m[i], LANE))
        buf2d = state_ssd_conv[i].reshape(dbs, SSD_CONV - 1, 3, 256).transpose(0, 2, 1, 3).reshape(dbs, -1)
        s_n, y_ssd = _step_ssd(proj_s, buf2d, bias_rows[i], alog_rows[i], ssd_cwb[i],
                               dskip_rows[i], ssdg, state_ssd[i].reshape(dbs, -1))
        g_n, y_gla = _step_gla(proj_s, wf2_p[i], bfr, glag, state_gla[i].reshape(dbs, -1))
        mix_s = jnp.concatenate([y_ml, y_ssd, y_gla], axis=-1).astype(BF16)
        prev = state_ffn_conv[i].reshape(dbs, FFN_CONV - 1, 2, N_FF_CHUNK, FF_CHUNK).transpose(1, 2, 3, 0, 4)
        xs, up_s = _out_ffn_step(xs, mix_s, w_out_b[i], g2, w_up_p[i], ffn_cwb[i], w_down_p[i], prev)
        s_states.append((
            c_n.reshape(state_mlstm_c.shape[1:]), n_n.reshape(state_mlstm_n.shape[1:]), m_n[:, :ML_HEADS],
            s_n.reshape(state_ssd.shape[1:]),
            jnp.concatenate([state_ssd_conv[i][:, 1:], proj_s[:, None, SXBC:SXBC + SSD_CONV_DIM]], axis=1),
            g_n.reshape(state_gla.shape[1:]),
            jnp.concatenate([state_ffn_conv[i][:, 1:], up_s.transpose(2, 0, 1, 3).reshape(dbs, 1, 2 * D_FF)], axis=1),
        ))

    fg = final_norm_g[None, :]
    y_prompt = _final_norm(xp, fg).reshape(bsz, seq, D_MODEL)
    y_sample = _final_norm(xs, fg).reshape(dbs, 1, D_MODEL)

    def stk(sts, j):
        return jnp.stack([s[j] for s in sts], axis=0)

    return ((y_prompt, y_sample) + tuple(stk(p_states, j) for j in range(7))
            + tuple(stk(s_states, j) for j in range(7)))
```

```python
import functools

import jax
import jax.numpy as jnp
from jax import lax
from jax.experimental import pallas as pl
from jax.experimental.pallas import tpu as pltpu

F32 = jnp.float32
BF16 = jnp.bfloat16
I32 = jnp.int32

D_MODEL = 1024
DEPTH = 4
ML_HEADS = 4
ML_DK = 64
SSD_HEADS = 8
SSD_W = 512
SSD_CONV = 4
SSD_CONV_DIM = 768
GLA_HEADS = 4
GLA_DK = 32
GLA_RANK = 16
GLA_TAU = 16.0
D_FF = 2816
FFN_CONV = 3
EPS = 1e-6
IN_SIZES = (256, 256, 256, 256, 4, 4, 512, 768, 8, 128, 128, 256, 256, 16)

MQ, MK, MV, MO, SZ, SXBC, GQ, GK, GV, GG, GT, NP = (
    0, 256, 512, 1024, 1280, 1792, 2560, 2688, 2816, 3072, 3328, 3456)
GATE_LI, GATE_LF, GATE_DT, GATE_GF = 0, 4, 8, 16

LANE = 128
SUBLANE = 8
MIX_T = 128
GLA_C = 16
FF_CHUNK = 256
N_FF_CHUNK = D_FF // FF_CHUNK
VMEM_LIMIT = 56 * 1024 * 1024


def _cparams(*sem):
    return pltpu.CompilerParams(dimension_semantics=sem if sem else None,
                                vmem_limit_bytes=VMEM_LIMIT)


def _const_spec(shape):
    nd = len(shape)
    return pl.BlockSpec(shape, lambda *_: (0,) * nd)


def _sigmoid(x):
    return 1.0 / (1.0 + jnp.exp(-x))


def _softplus(x):
    return jnp.maximum(x, 0.0) + jnp.log(1.0 + jnp.exp(-jnp.abs(x)))


def _log_sigmoid(x):
    return -_softplus(-x)


def _rmsnorm_rows(x, g):
    ms = jnp.mean(x * x, axis=-1, keepdims=True)
    return x * lax.rsqrt(ms + EPS) * g


def _dot(a, b):
    return jnp.dot(a, b, preferred_element_type=F32)


def _dot_nt(a, b):
    return lax.dot_general(a, b, (((1,), (1,)), ((), ())), preferred_element_type=F32)


def _dot_tn(a, b):
    return lax.dot_general(a, b, (((0,), (0,)), ((), ())), preferred_element_type=F32)


def _split3(x):
    x1 = x.astype(BF16)
    r1 = x - x1.astype(F32)
    x2 = r1.astype(BF16)
    x3 = (r1 - x2.astype(F32)).astype(BF16)
    return x1, x2, x3


def _cumsum_rows(x, tri):
    c = _dot(tri, jnp.concatenate(_split3(x), axis=1))
    return c[:, :LANE] + c[:, LANE:2 * LANE] + c[:, 2 * LANE:]


def _shift_rows(x, k, carry):
    r = pltpu.roll(x, k, 0)
    c = pltpu.roll(carry, k, 0)
    row = lax.broadcasted_iota(I32, c.shape, 0)
    head = jnp.where(row < k, c, r[:SUBLANE])
    return jnp.concatenate([head, r[SUBLANE:]], axis=0)


def _seg_rmsnorm(x, seg):
    w = x.shape[1]
    if seg % LANE == 0:
        parts = []
        for s0 in range(0, w, seg):
            xs = x[:, s0:s0 + seg]
            parts.append(xs * lax.rsqrt(jnp.mean(xs * xs, axis=-1, keepdims=True) + EPS))
        return jnp.concatenate(parts, axis=1)
    parts = []
    lane = lax.broadcasted_iota(I32, (x.shape[0], LANE), 1)
    for s0 in range(0, w, LANE):
        xs = x[:, s0:s0 + LANE]
        sq = xs * xs
        lo = jnp.sum(jnp.where(lane < seg, sq, 0.0), axis=-1, keepdims=True)
        hi = jnp.sum(jnp.where(lane >= seg, sq, 0.0), axis=-1, keepdims=True)
        r = jnp.where(lane < seg, lax.rsqrt(lo / seg + EPS), lax.rsqrt(hi / seg + EPS))
        parts.append(xs * r)
    return jnp.concatenate(parts, axis=1)


def _norm_proj_kernel(x_ref, g_ref, w_ref, o_ref):
    h = _rmsnorm_rows(x_ref[...], g_ref[...]).astype(BF16)
    for n0 in range(0, NP, 512):
        n1 = min(n0 + 512, NP)
        o_ref[:, n0:n1] = _dot(h, w_ref[:, n0:n1])


def _norm_proj(x2d, g, w):
    m = x2d.shape[0]
    tm = min(512, m)
    return pl.pallas_call(
        _norm_proj_kernel,
        grid=(m // tm,),
        in_specs=[pl.BlockSpec((tm, D_MODEL), lambda i: (i, 0)),
                  _const_spec((1, D_MODEL)),
                  _const_spec((D_MODEL, NP))],
        out_specs=pl.BlockSpec((tm, NP), lambda i: (i, 0)),
        out_shape=jax.ShapeDtypeStruct((m, NP), F32),
        compiler_params=_cparams("arbitrary"),
        name="norm_proj",
    )(x2d, g, w)


def _mixer_kernel(proj_ref, bias_ref, alog_ref, mlg_ref, cw_ref, dskip_ref, ssdg_ref, wf2_ref,
                  bf_ref, glag_ref,
                  mix_ref, caug_ref, m_ref, ssd_ref, gla_ref,
                  conv_s, cum_s):
    T = MIX_T
    t_idx = pl.program_id(1)

    @pl.when(t_idx == 0)
    def _():
        caug_ref[...] = jnp.zeros_like(caug_ref)
        m_ref[...] = jnp.zeros_like(m_ref)
        ssd_ref[...] = jnp.zeros_like(ssd_ref)
        gla_ref[...] = jnp.zeros_like(gla_ref)
        conv_s[...] = jnp.zeros_like(conv_s)

    lane = lax.broadcasted_iota(I32, (T, LANE), 1)
    rowi = lax.broadcasted_iota(I32, (T, T), 0)
    coli = lax.broadcasted_iota(I32, (T, T), 1)
    causal = rowi >= coli
    tri = jnp.where(causal, 1.0, 0.0).astype(BF16)
    row128 = lax.broadcasted_iota(I32, (LANE, LANE), 0)
    neg_inf = jnp.float32(-jnp.inf)

    g_blk = proj_ref[:, GT:GT + LANE] + bias_ref[...]
    is_lf = (lane >= GATE_LF) & (lane < GATE_DT)
    is_dt = (lane >= GATE_DT) & (lane < GATE_GF)
    dt = _softplus(jnp.where(is_lf, -g_blk, g_blk))
    a_row = -jnp.exp(alog_ref[...])
    cum_g = _cumsum_rows(jnp.where(is_lf, -dt, jnp.where(is_dt, dt * a_row, 0.0)), tri)
    p1_t = jnp.where(lane < GATE_LF, g_blk, cum_g).T
    dt_t = dt.T

    e64 = jnp.where(lane == 64, 1.0, 0.0)
    m_row = m_ref[...]
    m_new_row = m_row
    ml_parts = []
    for h in range(ML_HEADS):
        pair, lo = h // 2, 64 * (h % 2)
        hm = (lane >= lo) & (lane < lo + 64)
        qm = jnp.where(hm, proj_ref[:, MQ + LANE * pair:MQ + LANE * (pair + 1)] * 0.125, 0.0).astype(BF16)
        km = jnp.where(hm, proj_ref[:, MK + LANE * pair:MK + LANE * (pair + 1)], 0.0)
        vaug = (proj_ref[:, MV + LANE * h:MV + LANE * (h + 1)] + e64).astype(BF16)
        b_col = cum_g[:, GATE_LF + h:GATE_LF + h + 1]
        b_row = p1_t[GATE_LF + h:GATE_LF + h + 1, :]
        li_row = p1_t[GATE_LI + h:GATE_LI + h + 1, :]
        li_col = g_blk[:, GATE_LI + h:GATE_LI + h + 1]
        m_prev = m_row[:, h:h + 1]
        d_mat = jnp.where(causal, b_col - b_row + li_row, neg_inf)
        a_col = b_col + m_prev
        mt = jnp.maximum(a_col, jnp.max(d_mat, axis=1, keepdims=True))
        s_mat = (_dot_nt(qm, km.astype(BF16)) * jnp.exp(d_mat - mt)).astype(BF16)
        c_pair = caug_ref[LANE * pair:LANE * (pair + 1), :]
        numaug = _dot(s_mat, vaug) + jnp.exp(a_col - mt) * _dot(qm, c_pair.astype(BF16))
        den = numaug[:, 64:65]
        hn = numaug / jnp.maximum(jnp.abs(den), jnp.exp(-mt))
        hv = jnp.where(lane < 64, hn, 0.0)
        ml_parts.append(hv * lax.rsqrt(jnp.sum(hv * hv, axis=-1, keepdims=True) / 64.0 + EPS))
        b_last = b_col[T - 1:T, :]
        m_new = mt[T - 1:T, :]
        g_state = jnp.exp(b_last + m_prev - m_new)
        g_s = jnp.exp(b_last - b_col + li_col - m_new)
        u = _dot_tn((km * g_s).astype(BF16), vaug)
        in_head = (row128 >= lo) & (row128 < lo + 64)
        caug_ref[LANE * pair:LANE * (pair + 1), :] = jnp.where(in_head, g_state * c_pair + u, c_pair)
        m_new_row = jnp.where(lax.broadcasted_iota(I32, m_row.shape, 1) == h, m_new, m_new_row)
    m_ref[...] = m_new_row
    y_ml = jnp.concatenate(
        [jnp.where(lane < 64, ml_parts[2 * p], pltpu.roll(ml_parts[2 * p + 1], 64, 1)) for p in range(2)], axis=1)
    mix_ref[:, 0:256] = (_sigmoid(proj_ref[:, MO:MO + 256]) * (y_ml * mlg_ref[...])).astype(BF16)

    raw = proj_ref[:, SXBC:SXBC + SSD_CONV_DIM]
    carry = conv_s[...]
    conv = (cw_ref[4:5, :] + _shift_rows(raw, 3, carry) * cw_ref[0:1, :] + _shift_rows(raw, 2, carry) * cw_ref[1:2, :]
            + _shift_rows(raw, 1, carry) * cw_ref[2:3, :] + raw * cw_ref[3:4, :])
    conv_s[...] = raw[T - SUBLANE:T, :]
    xbc = conv * _sigmoid(conv)
    b_p = xbc[:, SSD_W:SSD_W + LANE]
    c_p = xbc[:, SSD_W + LANE:SSD_W + 2 * LANE]
    c_pb = c_p.astype(BF16)
    cb = []
    for g in range(2):
        in_g = (lane >= 64 * g) & (lane < 64 * (g + 1))
        cb.append(_dot_nt(jnp.where(in_g, c_p, 0.0).astype(BF16), b_p.astype(BF16)))
    ys_parts = []
    for j in range(SSD_HEADS // 2):
        g = j // 2
        x_pair = xbc[:, LANE * j:LANE * (j + 1)]
        s_pair = ssd_ref[LANE * j:LANE * (j + 1), :]
        y_pair = None
        ecs, wcol, dec = [], [], []
        for hh in range(2):
            h = 2 * j + hh
            cs_col = cum_g[:, GATE_DT + h:GATE_DT + h + 1]
            cs_row = p1_t[GATE_DT + h:GATE_DT + h + 1, :]
            dt_row = dt_t[GATE_DT + h:GATE_DT + h + 1, :]
            dt_col = dt[:, GATE_DT + h:GATE_DT + h + 1]
            l_mat = jnp.exp(jnp.where(causal, cs_col - cs_row, neg_inf))
            sc = (cb[g] * l_mat * dt_row).astype(BF16)
            in_h = (lane >= 64 * hh) & (lane < 64 * (hh + 1))
            part = _dot(sc, jnp.where(in_h, x_pair, 0.0).astype(BF16))
            y_pair = part if y_pair is None else y_pair + part
            cs_last = cs_col[T - 1:T, :]
            ecs.append(jnp.exp(cs_col))
            wcol.append(jnp.exp(cs_last - cs_col) * dt_col)
            dec.append(jnp.exp(cs_last))
        y_pair = y_pair + jnp.where(lane < 64, ecs[0], ecs[1]) * _dot_nt(c_pb, s_pair.astype(BF16))
        ys_parts.append(y_pair)
        wx = (x_pair * jnp.where(lane < 64, wcol[0], wcol[1])).astype(BF16)
        in_g = (lane >= 64 * g) & (lane < 64 * (g + 1))
        u = _dot_tn(wx, jnp.where(in_g, b_p, 0.0).astype(BF16))
        ssd_ref[LANE * j:LANE * (j + 1), :] = jnp.where(row128 < 64, dec[0], dec[1]) * s_pair + u
    y_s = jnp.concatenate(ys_parts, axis=1) + dskip_ref[...] * xbc[:, :SSD_W]
    z = proj_ref[:, SZ:SZ + SSD_W]
    y_ssd = _seg_rmsnorm(y_s * (z * _sigmoid(z)), 256) * ssdg_ref[...]
    mix_ref[:, 256:768] = y_ssd.astype(BF16)

    la = _log_sigmoid(_dot(g_blk.astype(BF16), wf2_ref[...]) + bf_ref[...]) / GLA_TAU
    cum = _cumsum_rows(la, tri)
    cum_s[...] = cum
    cum_t = cum.T
    q = proj_ref[:, GQ:GQ + LANE] * (GLA_DK ** -0.5)
    k = proj_ref[:, GK:GK + LANE]
    v = proj_ref[:, GV:GV + 256]
    vb = v.astype(BF16)
    s_gla = gla_ref[...]
    o = _dot((q * jnp.exp(cum)).astype(BF16), s_gla.astype(BF16))
    u = _dot_tn((k * jnp.exp(cum[T - 1:T, :] - cum)).astype(BF16), vb)
    r_s = lax.broadcasted_iota(I32, (LANE, 256), 0)
    c_s = lax.broadcasted_iota(I32, (LANE, 256), 1)
    same_head = (r_s >> 5) == (c_s >> 6)
    gla_ref[...] = jnp.exp(cum_t[:, T - 1:T]) * s_gla + jnp.where(same_head, u, 0.0)
    tpos = lax.broadcasted_iota(I32, (T, LANE), 0)
    att = [None] * GLA_HEADS
    w = GLA_C
    while w < T:
        is_t = (tpos & (2 * w - 1)) >= w
        mid = jnp.concatenate(
            [jnp.broadcast_to(cum_s[2 * w * blk + w - 1:2 * w * blk + w, :], (2 * w, LANE))
             for blk in range(T // (2 * w))], axis=0)
        e = jnp.exp(jnp.where(is_t, cum - mid, mid - cum))
        ql = jnp.where(is_t, q * e, 0.0)
        kl = jnp.where(is_t, 0.0, k * e).astype(BF16)
        shift = (2 * w).bit_length() - 1
        same_blk = (rowi >> shift) == (coli >> shift)
        for h in range(GLA_HEADS):
            in_h = (lane >= GLA_DK * h) & (lane < GLA_DK * (h + 1))
            a = jnp.where(same_blk, _dot_nt(jnp.where(in_h, ql, 0.0).astype(BF16), kl), 0.0)
            att[h] = a if att[h] is None else att[h] + a
        w *= 2
    o_parts = []
    for p in range(2):
        v_pair = v[:, LANE * p:LANE * (p + 1)]
        o_parts.append(_dot(att[2 * p].astype(BF16), jnp.where(lane < 64, v_pair, 0.0).astype(BF16))
                       + _dot(att[2 * p + 1].astype(BF16), jnp.where(lane >= 64, v_pair, 0.0).astype(BF16)))
    o = o + jnp.concatenate(o_parts, axis=1)
    expand = jnp.where(same_head, 1.0, 0.0).astype(BF16)
    nblk = T // GLA_C
    for jj in range(GLA_C):
        def rows(ref, c0, c1):
            return jnp.concatenate(
                [jnp.broadcast_to(ref[GLA_C * i + jj:GLA_C * i + jj + 1, c0:c1], (GLA_C, c1 - c0))
                 for i in range(nblk)], axis=0)
        k_s = rows(proj_ref, GK, GK + LANE)
        c_srow = rows(cum_s, 0, LANE)
        v_s = rows(proj_ref, GV, GV + 256)
        valid = (tpos & (GLA_C - 1)) >= jj
        e = jnp.exp(jnp.where(valid, cum - c_srow, neg_inf))
        o = o + _dot((q * k_s * e).astype(BF16), expand) * v_s
    gg = proj_ref[:, GG:GG + 256]
    mix_ref[:, 768:1024] = (_seg_rmsnorm(o, 64) * glag_ref[...] * (gg * _sigmoid(gg))).astype(BF16)


def _mixer(proj, bsz, seq, bias_row, alog_row, mlg, cwb, dskip, ssdg, wf2p, bfr, glag):
    nt = seq // MIX_T
    row_spec = lambda w: _const_spec((1, w))
    return pl.pallas_call(
        _mixer_kernel,
        grid=(bsz, nt),
        in_specs=[pl.BlockSpec((MIX_T, NP), lambda b, t: (b * nt + t, 0)),
                  row_spec(LANE), row_spec(LANE), row_spec(256),
                  _const_spec((SUBLANE, SSD_CONV_DIM)), row_spec(SSD_W), row_spec(SSD_W),
                  _const_spec((LANE, LANE)), row_spec(LANE), row_spec(256)],
        out_specs=[pl.BlockSpec((MIX_T, D_MODEL), lambda b, t: (b * nt + t, 0)),
                   pl.BlockSpec((None, 256, LANE), lambda b, t: (b, 0, 0)),
                   pl.BlockSpec((None, 1, LANE), lambda b, t: (b, 0, 0)),
                   pl.BlockSpec((None, 512, LANE), lambda b, t: (b, 0, 0)),
                   pl.BlockSpec((None, LANE, 256), lambda b, t: (b, 0, 0))],
        out_shape=[jax.ShapeDtypeStruct((bsz * seq, D_MODEL), BF16),
                   jax.ShapeDtypeStruct((bsz, 256, LANE), F32),
                   jax.ShapeDtypeStruct((bsz, 1, LANE), F32),
                   jax.ShapeDtypeStruct((bsz, 512, LANE), F32),
                   jax.ShapeDtypeStruct((bsz, LANE, 256), F32)],
        scratch_shapes=[pltpu.VMEM((SUBLANE, SSD_CONV_DIM), F32), pltpu.VMEM((MIX_T, LANE), F32)],
        compiler_params=_cparams("arbitrary", "arbitrary"),
        name="mixer",
    )(proj, bias_row, alog_row, mlg, cwb, dskip, ssdg, wf2p, bfr, glag)


def _ffn_chunks(h2_s, act_s, wup_ref, cwb_ref, wd_ref, o_ref, prev_fn, keep_fn):
    for c in range(N_FF_CHUNK):
        h2 = h2_s[...]
        halves = []
        for ug in range(2):
            col = slice(ug * D_FF + c * FF_CHUNK, ug * D_FF + (c + 1) * FF_CHUNK)
            up = _dot(h2, wup_ref[:, col])
            x1, x2 = prev_fn(col, up)
            keep_fn(col, up)
            halves.append(cwb_ref[3:4, col] + x2 * cwb_ref[0:1, col] + x1 * cwb_ref[1:2, col]
                          + up * cwb_ref[2:3, col])
        u, gate = halves
        act_s[:, c * FF_CHUNK:(c + 1) * FF_CHUNK] = (gate * _sigmoid(gate) * u).astype(BF16)
    o_ref[...] += _dot(act_s[...], wd_ref[...])


def _out_ffn_seq_kernel(x_ref, mix_ref, wo_ref, g2_ref, wup_ref, cwb_ref, wd_ref, o_ref, st_ref, h2_s, act_s):
    @pl.when(pl.program_id(1) == 0)
    def _():
        st_ref[...] = jnp.zeros_like(st_ref)

    xn = x_ref[...] + _dot(mix_ref[...], wo_ref[...])
    o_ref[...] = xn
    h2_s[...] = _rmsnorm_rows(xn, g2_ref[...]).astype(BF16)
    tm = x_ref.shape[0]

    def prev_fn(col, up):
        carry = st_ref[:, col]
        return _shift_rows(up, 1, carry), _shift_rows(up, 2, carry)

    def keep_fn(col, up):
        st_ref[:, col] = up[tm - SUBLANE:tm, :]

    _ffn_chunks(h2_s, act_s, wup_ref, cwb_ref, wd_ref, o_ref, prev_fn, keep_fn)


def _out_ffn_step_kernel(x_ref, mix_ref, wo_ref, g2_ref, wup_ref, cwb_ref, wd_ref, prev_ref, o_ref, up_ref,
                         h2_s, act_s):
    xn = x_ref[...] + _dot(mix_ref[...], wo_ref[...])
    o_ref[...] = xn
    h2_s[...] = _rmsnorm_rows(xn, g2_ref[...]).astype(BF16)

    def prev_fn(col, up):
        return prev_ref[1, :, col], prev_ref[0, :, col]

    def keep_fn(col, up):
        up_ref[:, col] = up

    _ffn_chunks(h2_s, act_s, wup_ref, cwb_ref, wd_ref, o_ref, prev_fn, keep_fn)


def _resident_spec(shape):
    nd = len(shape)
    return pl.BlockSpec(shape, lambda *_: (0,) * nd, pipeline_mode=pl.Buffered(1))


_FFN_W_SPECS = [_resident_spec((D_MODEL, D_MODEL)), _const_spec((1, D_MODEL)),
                _resident_spec((D_MODEL, 2 * D_FF)),
                _const_spec((SUBLANE, 2 * D_FF)),
                _resident_spec((D_FF, D_MODEL))]


def _out_ffn_seq(x2d, mix, bsz, seq, wo, g2, wup, cwb, wd):
    tm = min(512, seq)
    nt = seq // tm
    return pl.pallas_call(
        _out_ffn_seq_kernel,
        grid=(bsz, nt),
        in_specs=[pl.BlockSpec((tm, D_MODEL), lambda b, t: (b * nt + t, 0)),
                  pl.BlockSpec((tm, D_MODEL), lambda b, t: (b * nt + t, 0))] + _FFN_W_SPECS,
        out_specs=[pl.BlockSpec((tm, D_MODEL), lambda b, t: (b * nt + t, 0)),
                   pl.BlockSpec((None, SUBLANE, 2 * D_FF), lambda b, t: (b, 0, 0))],
        out_shape=[jax.ShapeDtypeStruct((bsz * seq, D_MODEL), F32),
                   jax.ShapeDtypeStruct((bsz, SUBLANE, 2 * D_FF), F32)],
        scratch_shapes=[pltpu.VMEM((tm, D_MODEL), BF16), pltpu.VMEM((tm, D_FF), BF16)],
        compiler_params=_cparams("arbitrary", "arbitrary"),
        name="out_ffn_seq",
    )(x2d, mix, wo, g2, wup, cwb, wd)


def _out_ffn_step(x2d, mix, wo, g2, wup, cwb, wd, prev):
    m = x2d.shape[0]
    return pl.pallas_call(
        _out_ffn_step_kernel,
        grid=(1,),
        in_specs=[_const_spec((m, D_MODEL)), _const_spec((m, D_MODEL))] + _FFN_W_SPECS
        + [_const_spec((FFN_CONV - 1, m, 2 * D_FF))],
        out_specs=[_const_spec((m, D_MODEL)), _const_spec((m, 2 * D_FF))],
        out_shape=[jax.ShapeDtypeStruct((m, D_MODEL), F32),
                   jax.ShapeDtypeStruct((m, 2 * D_FF), F32)],
        scratch_shapes=[pltpu.VMEM((m, D_MODEL), BF16), pltpu.VMEM((m, D_FF), BF16)],
        compiler_params=_cparams("arbitrary"),
        name="out_ffn_step",
    )(x2d, mix, wo, g2, wup, cwb, wd, prev)


def _final_norm_kernel(x_ref, g_ref, o_ref):
    o_ref[...] = _rmsnorm_rows(x_ref[...], g_ref[...])


def _final_norm(x2d, g):
    m = x2d.shape[0]
    tm = min(1024, m)
    return pl.pallas_call(
        _final_norm_kernel,
        grid=(m // tm,),
        in_specs=[pl.BlockSpec((tm, D_MODEL), lambda i: (i, 0)), _const_spec((1, D_MODEL))],
        out_specs=pl.BlockSpec((tm, D_MODEL), lambda i: (i, 0)),
        out_shape=jax.ShapeDtypeStruct((m, D_MODEL), F32),
        compiler_params=_cparams("arbitrary"),
        name="final_norm",
    )(x2d, g)


def _hi_lo(x):
    hi = x.astype(BF16)
    return hi, (x - hi.astype(F32)).astype(BF16)


def _repeat_matrix(src0, per):
    r = lax.broadcasted_iota(I32, (LANE, LANE), 0)
    c = lax.broadcasted_iota(I32, (LANE, LANE), 1)
    return jnp.where(r == src0 + (c >> (per.bit_length() - 1)), 1.0, 0.0).astype(BF16)


def _step_mlstm_kernel(proj_ref, bias_ref, mlg_ref, c_ref, n_ref, m_ref, c_out, n_out, m_out, y_ref):
    bsz = proj_ref.shape[0]
    lane = lax.broadcasted_iota(I32, (bsz, LANE), 1)
    g_blk = proj_ref[:, GT:GT + LANE] + bias_ref[...]
    m_all = m_ref[...]
    m_new_all = m_all
    y_parts, n_parts = [], []
    for h in range(ML_HEADS):
        pair, lo = h // 2, 64 * (h % 2)
        hm = (lane >= lo) & (lane < lo + 64)
        li = g_blk[:, GATE_LI + h:GATE_LI + h + 1]
        lf = _log_sigmoid(g_blk[:, GATE_LF + h:GATE_LF + h + 1])
        m0 = m_all[:, h:h + 1]
        mt = jnp.maximum(lf + m0, li)
        w_old = jnp.exp(lf + m0 - mt)
        w_in = jnp.exp(li - mt)
        qp = proj_ref[:, MQ + LANE * pair:MQ + LANE * (pair + 1)] * 0.125
        kp = proj_ref[:, MK + LANE * pair:MK + LANE * (pair + 1)]
        vs = proj_ref[:, MV + LANE * h:MV + LANE * (h + 1)]
        vt = vs + pltpu.roll(vs, 64, 1)
        stack = jnp.concatenate(_hi_lo(qp) + _hi_lo(kp), axis=0)
        acc = jnp.zeros((bsz, LANE), F32)
        for j in range(ML_DK // 2):
            ex = _dot(stack, _repeat_matrix(lo + 2 * j, 64))
            qe = ex[0:bsz] + ex[bsz:2 * bsz]
            ke = ex[2 * bsz:3 * bsz] + ex[3 * bsz:4 * bsz]
            c0 = h * ML_DK * 64 + LANE * j
            cn = w_old * c_ref[:, c0:c0 + LANE] + (w_in * ke) * vt
            c_out[:, c0:c0 + LANE] = cn
            acc = acc + qe * cn
        num = acc + pltpu.roll(acc, 64, 1)
        nn = w_old * n_ref[:, LANE * pair:LANE * (pair + 1)] + w_in * kp
        den = jnp.sum(jnp.where(hm, qp * nn, 0.0), axis=-1, keepdims=True)
        hn = num / jnp.maximum(jnp.abs(den), jnp.exp(-mt))
        hv = jnp.where(lane < 64, hn, 0.0)
        y_parts.append(hv * lax.rsqrt(jnp.sum(hv * hv, axis=-1, keepdims=True) / 64.0 + EPS))
        n_parts.append(jnp.where(hm, nn, 0.0))
        m_new_all = jnp.where(lane == h, mt, m_new_all)
    m_out[...] = m_new_all
    n_out[...] = jnp.concatenate([n_parts[0] + n_parts[1], n_parts[2] + n_parts[3]], axis=1)
    y_ml = jnp.concatenate(
        [jnp.where(lane < 64, y_parts[2 * p], pltpu.roll(y_parts[2 * p + 1], 64, 1)) for p in range(2)], axis=1)
    y_ref[...] = _sigmoid(proj_ref[:, MO:MO + 256]) * (y_ml * mlg_ref[...])


def _step_mlstm(proj, bias_row, mlg, c2, n2, mpad):
    bsz = proj.shape[0]
    shapes = [(bsz, ML_HEADS * ML_DK * 64), (bsz, 256), (bsz, LANE), (bsz, 256)]
    return pl.pallas_call(
        _step_mlstm_kernel,
        grid=(1,),
        in_specs=[_const_spec((bsz, NP)), _const_spec((1, LANE)), _const_spec((1, 256)),
                  _const_spec(shapes[0]), _const_spec(shapes[1]), _const_spec(shapes[2])],
        out_specs=[_const_spec(s) for s in shapes],
        out_shape=[jax.ShapeDtypeStruct(s, F32) for s in shapes],
        compiler_params=_cparams("arbitrary"),
        name="step_mlstm",
    )(proj, bias_row, mlg, c2, n2, mpad)


def _step_ssd_kernel(gate_ref, x_ref, z_ref, bc_ref, buf_x_ref, buf_bc_ref, bias_ref, alog_ref,
                     cwx_ref, cwbc_ref, dskip_ref, ssdg_ref, s_ref, s_out, y_ref):
    g = pl.program_id(0)
    bsz = x_ref.shape[0]
    lane = lax.broadcasted_iota(I32, (bsz, LANE), 1)
    lane1 = lax.broadcasted_iota(I32, (1, LANE), 1)
    g_blk = gate_ref[...] + bias_ref[...]
    dt_all = _softplus(g_blk)
    a_row = -jnp.exp(alog_ref[...])

    def conv4(cur, buf_ref, cw_ref):
        wdt = cur.shape[1]
        acc = cw_ref[4:5, :] + cur * cw_ref[3:4, :]
        for j in range(SSD_CONV - 1):
            acc = acc + buf_ref[:, wdt * j:wdt * (j + 1)] * cw_ref[j:j + 1, :]
        return acc * _sigmoid(acc)

    xc = conv4(x_ref[...], buf_x_ref, cwx_ref)
    bc = conv4(bc_ref[...], buf_bc_ref, cwbc_ref)
    b_p, c_p = bc[:, :LANE], bc[:, LANE:]
    b_al = jnp.where(g == 0, b_p, pltpu.roll(b_p, 64, 1))
    c_al = jnp.where(g == 0, c_p, pltpu.roll(c_p, 64, 1))
    b_t = jnp.where(lane < 64, b_al, pltpu.roll(b_al, 64, 1))
    c_t = jnp.where(lane < 64, c_al, pltpu.roll(c_al, 64, 1))
    rr = lax.broadcasted_iota(I32, (LANE, LANE), 0)
    cc = lax.broadcasted_iota(I32, (LANE, LANE), 1)
    ys = []
    for p in range(2):
        x_pair = xc[:, LANE * p:LANE * (p + 1)]
        y_pair = jnp.zeros((bsz, LANE), F32)
        for hh2 in range(2):
            hh = 2 * p + hh2
            lo = 64 * hh2
            sel = lane == GATE_DT + 4 * g + hh
            dt_col = jnp.sum(jnp.where(sel, dt_all, 0.0), axis=-1, keepdims=True)
            a_h = jnp.sum(jnp.where(lane1 == GATE_DT + 4 * g + hh, a_row, 0.0), axis=-1, keepdims=True)
            d_a = jnp.exp(dt_col * a_h)
            stack = jnp.concatenate(_hi_lo(dt_col * x_pair), axis=0)
            for j in range(32):
                ue = _dot(stack, _repeat_matrix(lo + 2 * j, 64))
                c0 = hh * 4096 + LANE * j
                sn = d_a * s_ref[:, c0:c0 + LANE] + (ue[0:bsz] + ue[bsz:2 * bsz]) * b_t
                s_out[:, c0:c0 + LANE] = sn
                place = jnp.where(cc == lo + 2 * j + (rr >> 6), 1.0, 0.0).astype(BF16)
                yy = _dot(jnp.concatenate(_hi_lo(sn * c_t), axis=0), place)
                y_pair = y_pair + yy[0:bsz] + yy[bsz:2 * bsz]
        ys.append(y_pair)
    y_s = jnp.concatenate(ys, axis=1) + dskip_ref[...] * xc
    z = z_ref[...]
    y_ref[...] = _seg_rmsnorm(y_s * (z * _sigmoid(z)), 256) * ssdg_ref[...]


def _step_ssd(proj, buf2d, bias_row, alog_row, cwb, dskip, ssdg, s2):
    bsz = proj.shape[0]
    gw = 256
    return pl.pallas_call(
        _step_ssd_kernel,
        grid=(2,),
        in_specs=[pl.BlockSpec((bsz, LANE), lambda g: (0, GT // LANE)),
                  pl.BlockSpec((bsz, gw), lambda g: (0, SXBC // gw + g)),
                  pl.BlockSpec((bsz, gw), lambda g: (0, SZ // gw + g)),
                  pl.BlockSpec((bsz, gw), lambda g: (0, (SXBC + SSD_W) // gw)),
                  pl.BlockSpec((bsz, 3 * gw), lambda g: (0, g)),
                  pl.BlockSpec((bsz, 3 * gw), lambda g: (0, 2)),
                  _const_spec((1, LANE)), _const_spec((1, LANE)),
                  pl.BlockSpec((SUBLANE, gw), lambda g: (0, g)),
                  pl.BlockSpec((SUBLANE, gw), lambda g: (0, 2)),
                  pl.BlockSpec((1, gw), lambda g: (0, g)),
                  pl.BlockSpec((1, gw), lambda g: (0, g)),
                  pl.BlockSpec((bsz, 4 * 4096), lambda g: (0, g))],
        out_specs=[pl.BlockSpec((bsz, 4 * 4096), lambda g: (0, g)),
                   pl.BlockSpec((bsz, gw), lambda g: (0, g))],
        out_shape=[jax.ShapeDtypeStruct((bsz, SSD_HEADS * 4096), F32),
                   jax.ShapeDtypeStruct((bsz, SSD_W), F32)],
        compiler_params=_cparams("arbitrary"),
        name="step_ssd",
    )(proj, proj, proj, proj, buf2d, buf2d, bias_row, alog_row, cwb, cwb, dskip, ssdg, s2)


def _step_gla_kernel(proj_ref, wf2_ref, bf_ref, glag_ref, s_ref, s_out, y_ref):
    bsz = proj_ref.shape[0]
    lane = lax.broadcasted_iota(I32, (bsz, LANE), 1)
    la = _log_sigmoid(_dot(proj_ref[:, GT:GT + LANE].astype(BF16), wf2_ref[...]) + bf_ref[...]) / GLA_TAU
    q = proj_ref[:, GQ:GQ + LANE] * (GLA_DK ** -0.5)
    k = proj_ref[:, GK:GK + LANE]
    stack = jnp.concatenate(_hi_lo(q) + _hi_lo(k) + _hi_lo(jnp.exp(la)), axis=0)
    o_parts = []
    for h in range(GLA_HEADS):
        v_pair = proj_ref[:, GV + LANE * (h // 2):GV + LANE * (h // 2 + 1)]
        v_al = v_pair if h % 2 == 0 else pltpu.roll(v_pair, 64, 1)
        v_t = jnp.where(lane < 64, v_al, pltpu.roll(v_al, 64, 1))
        acc = jnp.zeros((bsz, LANE), F32)
        for j in range(GLA_DK // 2):
            ex = _dot(stack, _repeat_matrix(GLA_DK * h + 2 * j, 64))
            qe = ex[0:bsz] + ex[bsz:2 * bsz]
            ke = ex[2 * bsz:3 * bsz] + ex[3 * bsz:4 * bsz]
            de = ex[4 * bsz:5 * bsz] + ex[5 * bsz:6 * bsz]
            c0 = h * GLA_DK * 64 + LANE * j
            sn = de * s_ref[:, c0:c0 + LANE] + ke * v_t
            s_out[:, c0:c0 + LANE] = sn
            acc = acc + qe * sn
        o_parts.append(jnp.where(lane < 64, acc + pltpu.roll(acc, 64, 1), 0.0))
    o = jnp.concatenate(
        [jnp.where(lane < 64, o_parts[2 * p], pltpu.roll(o_parts[2 * p + 1], 64, 1)) for p in range(2)], axis=1)
    gg = proj_ref[:, GG:GG + 256]
    y_ref[...] = _seg_rmsnorm(o, 64) * glag_ref[...] * (gg * _sigmoid(gg))


def _step_gla(proj, wf2p, bfr, glag, s2):
    bsz = proj.shape[0]
    sw = GLA_HEADS * GLA_DK * 64
    return pl.pallas_call(
        _step_gla_kernel,
        grid=(1,),
        in_specs=[_const_spec((bsz, NP)), _const_spec((LANE, LANE)), _const_spec((1, LANE)),
                  _const_spec((1, 256)), _const_spec((bsz, sw))],
        out_specs=[_const_spec((bsz, sw)), _const_spec((bsz, 256))],
        out_shape=[jax.ShapeDtypeStruct((bsz, sw), F32), jax.ShapeDtypeStruct((bsz, 256), F32)],
        compiler_params=_cparams("arbitrary"),
        name="step_gla",
    )(proj, wf2p, bfr, glag, s2)


def _prep_w_in(w_in):
    d = w_in.shape[0]
    parts, acc = [], 0
    for s in IN_SIZES:
        parts.append(w_in[..., acc:acc + s])
        acc += s
    mq, mk, mv, mo, mi, mf, sz, sxbc, sdt, gq, gk, gv, gg, gf = parts
    lead = w_in.shape[:-1]
    mv_pad = jnp.pad(mv.reshape(lead + (ML_HEADS, 64)), [(0, 0)] * len(lead) + [(0, 0), (0, 64)])
    mv_pad = mv_pad.reshape(lead + (ML_HEADS * LANE,))
    gates = jnp.concatenate([mi, mf, sdt, gf, jnp.zeros(lead + (LANE - 32,), w_in.dtype)], axis=-1)
    del d
    return jnp.concatenate([mq, mk, mv_pad, mo, sz, sxbc, gq, gk, gv, gg, gates], axis=-1).astype(BF16)


def _pad_row(x, width):
    return jnp.pad(x, [(0, 0), (0, width - x.shape[-1])])


def kernel(x_prompt, x_sample, state_mlstm_c, state_mlstm_n, state_mlstm_m, state_ssd, state_ssd_conv,
           state_gla, state_ffn_conv, norm1_g, w_in, mlstm_b_i, mlstm_b_f, mlstm_norm_g, ssd_conv_w,
           ssd_conv_b, ssd_dt_bias, ssd_a_log, ssd_d, ssd_norm_g, gla_w_f2, gla_b_f, gla_norm_g, w_out,
           norm2_g, w_up, ffn_conv_w, ffn_conv_b, w_down, final_norm_g):
    bsz, seq, _ = x_prompt.shape
    dbs = x_sample.shape[0]
    depth = w_in.shape[0]

    w_in_p = _prep_w_in(w_in)
    w_out_b = w_out.astype(BF16)
    w_up_p = w_up.astype(BF16)
    w_down_p = w_down.astype(BF16)
    ffn_cwb = jnp.concatenate([ffn_conv_w, ffn_conv_b[:, None, :],
                               jnp.zeros((depth, SUBLANE - FFN_CONV - 1, 2 * D_FF), F32)], axis=1)
    zeros8 = jnp.zeros((depth, 8), F32)
    bias_rows = _pad_row(jnp.concatenate([mlstm_b_i, mlstm_b_f, ssd_dt_bias], axis=-1), LANE)[:, None, :]
    alog_rows = _pad_row(jnp.concatenate([zeros8, ssd_a_log], axis=-1), LANE)[:, None, :]
    ssd_cwb = jnp.concatenate([ssd_conv_w, ssd_conv_b[:, None, :],
                               jnp.zeros((depth, SUBLANE - SSD_CONV - 1, SSD_CONV_DIM), F32)], axis=1)
    dskip_rows = jnp.repeat(ssd_d, 64, axis=-1)[:, None, :]
    wf2_p = jnp.pad(gla_w_f2, [(0, 0), (GATE_GF, LANE - GATE_GF - GLA_RANK), (0, 0)]).astype(BF16)

    xp = x_prompt.reshape(bsz * seq, D_MODEL)
    xs = x_sample.reshape(dbs, D_MODEL)
    p_states, s_states = [], []
    for i in range(depth):
        g1 = norm1_g[i][None, :]
        g2 = norm2_g[i][None, :]
        mlg = mlstm_norm_g[i][None, :]
        ssdg = ssd_norm_g[i][None, :]
        glag = gla_norm_g[i][None, :]
        bfr = gla_b_f[i][None, :]

        proj = _norm_proj(xp, g1, w_in_p[i])
        mix, caug, m_o, ssd_o, gla_o = _mixer(proj, bsz, seq, bias_rows[i], alog_rows[i], mlg, ssd_cwb[i],
                                              dskip_rows[i], ssdg, wf2_p[i], bfr, glag)
        xp, ffn_st = _out_ffn_seq(xp, mix, bsz, seq, w_out_b[i], g2, w_up_p[i], ffn_cwb[i], w_down_p[i])
        caug4 = caug.reshape(bsz, ML_HEADS, ML_DK, LANE)
        ssd4 = ssd_o.reshape(bsz, SSD_HEADS, 64, LANE)
        gla4 = gla_o.reshape(bsz, GLA_HEADS, GLA_DK, 256)
        p_states.append((
            caug4[..., :64], caug4[..., 64], m_o[:, 0, :ML_HEADS],
            jnp.where(jnp.arange(SSD_HEADS)[None, :, None, None] < 4, ssd4[..., :64], ssd4[..., 64:]),
            proj.reshape(bsz, seq, NP)[:, seq - (SSD_CONV - 1):, SXBC:SXBC + SSD_CONV_DIM],
            jnp.stack([gla4[:, h, :, 64 * h:64 * (h + 1)] for h in range(GLA_HEADS)], axis=1),
            ffn_st[:, SUBLANE - (FFN_CONV - 1):, :],
        ))

        proj_s = _norm_proj(xs, g1, w_in_p[i])
        c_n, n_n, m_n, y_ml = _step_mlstm(proj_s, bias_rows[i], mlg,
                                          state_mlstm_c[i].reshape(dbs, -1), state_mlstm_n[i].reshape(dbs, -1),
                                          _pad_row(state_mlstm_m[i], LANE))
        buf2d = state_ssd_conv[i].reshape(dbs, SSD_CONV - 1, 3, 256).transpose(0, 2, 1, 3).reshape(dbs, -1)
        s_n, y_ssd = _step_ssd(proj_s, buf2d, bias_rows[i], alog_rows[i], ssd_cwb[i],
                               dskip_rows[i], ssdg, state_ssd[i].reshape(dbs, -1))
        g_n, y_gla = _step_gla(proj_s, wf2_p[i], bfr, glag, state_gla[i].reshape(dbs, -1))
        mix_s = jnp.concatenate([y_ml, y_ssd, y_gla], axis=-1).astype(BF16)
        prev = state_ffn_conv[i].transpose(1, 0, 2)
        xs, up_s = _out_ffn_step(xs, mix_s, w_out_b[i], g2, w_up_p[i], ffn_cwb[i], w_down_p[i], prev)
        s_states.append((
            c_n.reshape(state_mlstm_c.shape[1:]), n_n.reshape(state_mlstm_n.shape[1:]), m_n[:, :ML_HEADS],
            s_n.reshape(state_ssd.shape[1:]),
            jnp.concatenate([state_ssd_conv[i][:, 1:], proj_s[:, None, SXBC:SXBC + SSD_CONV_DIM]], axis=1),
            g_n.reshape(state_gla.shape[1:]),
            jnp.concatenate([state_ffn_conv[i][:, 1:], up_s[:, None, :]], axis=1),
        ))

    fg = final_norm_g[None, :]
    y_prompt = _final_norm(xp, fg).reshape(bsz, seq, D_MODEL)
    y_sample = _final_norm(xs, fg).reshape(dbs, 1, D_MODEL)

    def stk(sts, j):
        return jnp.stack([s[j] for s in sts], axis=0)

    return ((y_prompt, y_sample) + tuple(stk(p_states, j) for j in range(7))
            + tuple(stk(s_states, j) for j in range(7)))
```

```python
import functools

import jax
import jax.numpy as jnp
from jax import lax
from jax.experimental import pallas as pl
from jax.experimental.pallas import tpu as pltpu

F32 = jnp.float32
BF16 = jnp.bfloat16
I32 = jnp.int32

D_MODEL = 1024
DEPTH = 4
ML_HEADS = 4
ML_DK = 64
SSD_HEADS = 8
SSD_W = 512
SSD_CONV = 4
SSD_CONV_DIM = 768
GLA_HEADS = 4
GLA_DK = 32
GLA_RANK = 16
GLA_TAU = 16.0
D_FF = 2816
FFN_CONV = 3
EPS = 1e-6
IN_SIZES = (256, 256, 256, 256, 4, 4, 512, 768, 8, 128, 128, 256, 256, 16)

MQ, MK, MV, MO, SZ, SXBC, GQ, GK, GV, GG, GT, NP = (
    0, 256, 512, 1024, 1280, 1792, 2560, 2688, 2816, 3072, 3328, 3456)
GATE_LI, GATE_LF, GATE_DT, GATE_GF = 0, 4, 8, 16

LANE = 128
SUBLANE = 8
MIX_T = 128
MIX_ROWS = 2
GLA_C = 8
FF_CHUNK = 256
N_FF_CHUNK = D_FF // FF_CHUNK
VMEM_LIMIT = 56 * 1024 * 1024


def _cparams(*sem):
    return pltpu.CompilerParams(dimension_semantics=sem if sem else None,
                                vmem_limit_bytes=VMEM_LIMIT)


def _const_spec(shape):
    nd = len(shape)
    return pl.BlockSpec(shape, lambda *_: (0,) * nd)


def _sigmoid(x):
    return 1.0 / (1.0 + jnp.exp(-x))


def _softplus(x):
    return jnp.maximum(x, 0.0) + jnp.log(1.0 + jnp.exp(-jnp.abs(x)))


def _log_sigmoid(x):
    return -_softplus(-x)


def _rmsnorm_rows(x, g):
    ms = jnp.mean(x * x, axis=-1, keepdims=True)
    return x * lax.rsqrt(ms + EPS) * g


def _dot(a, b):
    return jnp.dot(a, b, preferred_element_type=F32)


def _dot_nt(a, b):
    return lax.dot_general(a, b, (((1,), (1,)), ((), ())), preferred_element_type=F32)


def _dot_tn(a, b):
    return lax.dot_general(a, b, (((0,), (0,)), ((), ())), preferred_element_type=F32)


def _split3(x):
    x1 = x.astype(BF16)
    r1 = x - x1.astype(F32)
    x2 = r1.astype(BF16)
    x3 = (r1 - x2.astype(F32)).astype(BF16)
    return x1, x2, x3


def _cumsum_rows(x, tri):
    c = _dot(tri, jnp.concatenate(_split3(x), axis=1))
    return c[:, :LANE] + c[:, LANE:2 * LANE] + c[:, 2 * LANE:]


def _shift_rows(x, k, carry):
    r = pltpu.roll(x, k, 0)
    c = pltpu.roll(carry, k, 0)
    row = lax.broadcasted_iota(I32, c.shape, 0)
    head = jnp.where(row < k, c, r[:SUBLANE])
    return jnp.concatenate([head, r[SUBLANE:]], axis=0)


def _seg_rmsnorm(x, seg):
    w = x.shape[1]
    if seg % LANE == 0:
        parts = []
        for s0 in range(0, w, seg):
            xs = x[:, s0:s0 + seg]
            parts.append(xs * lax.rsqrt(jnp.mean(xs * xs, axis=-1, keepdims=True) + EPS))
        return jnp.concatenate(parts, axis=1)
    parts = []
    lane = lax.broadcasted_iota(I32, (x.shape[0], LANE), 1)
    for s0 in range(0, w, LANE):
        xs = x[:, s0:s0 + LANE]
        sq = xs * xs
        lo = jnp.sum(jnp.where(lane < seg, sq, 0.0), axis=-1, keepdims=True)
        hi = jnp.sum(jnp.where(lane >= seg, sq, 0.0), axis=-1, keepdims=True)
        r = jnp.where(lane < seg, lax.rsqrt(lo / seg + EPS), lax.rsqrt(hi / seg + EPS))
        parts.append(xs * r)
    return jnp.concatenate(parts, axis=1)


def _norm_proj_kernel(x_ref, g_ref, w_ref, o_ref):
    h = _rmsnorm_rows(x_ref[...], g_ref[...]).astype(BF16)
    for n0 in range(0, NP, 512):
        n1 = min(n0 + 512, NP)
        o_ref[:, n0:n1] = _dot(h, w_ref[:, n0:n1])


def _norm_proj(x2d, g, w):
    m = x2d.shape[0]
    tm = min(512, m)
    return pl.pallas_call(
        _norm_proj_kernel,
        grid=(m // tm,),
        in_specs=[pl.BlockSpec((tm, D_MODEL), lambda i: (i, 0)),
                  _const_spec((1, D_MODEL)),
                  _const_spec((D_MODEL, NP))],
        out_specs=pl.BlockSpec((tm, NP), lambda i: (i, 0)),
        out_shape=jax.ShapeDtypeStruct((m, NP), F32),
        compiler_params=_cparams("arbitrary"),
        name="norm_proj",
    )(x2d, g, w)


def _mixer_kernel(proj_ref, bias_ref, alog_ref, mlg_ref, cw_ref, dskip_ref, ssdg_ref, wf2_ref,
                  bf_ref, glag_ref,
                  mix_ref, caug_ref, m_ref, ssd_ref, gla_ref,
                  conv_s, cum_s):
    @pl.when(pl.program_id(1) == 0)
    def _():
        caug_ref[...] = jnp.zeros_like(caug_ref)
        m_ref[...] = jnp.zeros_like(m_ref)
        ssd_ref[...] = jnp.zeros_like(ssd_ref)
        gla_ref[...] = jnp.zeros_like(gla_ref)
        conv_s[...] = jnp.zeros_like(conv_s)

    rows = [_MixerRow(proj_ref.at[r], bias_ref, alog_ref, mlg_ref, cw_ref, dskip_ref, ssdg_ref, wf2_ref,
                      bf_ref, glag_ref, mix_ref.at[r], caug_ref.at[r], m_ref.at[r], ssd_ref.at[r],
                      gla_ref.at[r], conv_s.at[r], cum_s.at[r]) for r in range(proj_ref.shape[0])]
    _drive([row.setup() for row in rows])
    _drive([c for row in rows for c in row.chains()])
    for row in rows:
        row.finish()


def _drive(chains):
    chains = list(chains)
    while chains:
        for c in list(chains):
            try:
                next(c)
            except StopIteration:
                chains.remove(c)


class _MixerRow:
    def __init__(self, proj_ref, bias_ref, alog_ref, mlg_ref, cw_ref, dskip_ref, ssdg_ref, wf2_ref,
                 bf_ref, glag_ref, mix_ref, caug_ref, m_ref, ssd_ref, gla_ref, conv_s, cum_s):
        self.proj_ref, self.bias_ref, self.alog_ref, self.mlg_ref = proj_ref, bias_ref, alog_ref, mlg_ref
        self.cw_ref, self.dskip_ref, self.ssdg_ref, self.wf2_ref = cw_ref, dskip_ref, ssdg_ref, wf2_ref
        self.bf_ref, self.glag_ref, self.mix_ref, self.caug_ref = bf_ref, glag_ref, mix_ref, caug_ref
        self.m_ref, self.ssd_ref, self.gla_ref, self.conv_s, self.cum_s = m_ref, ssd_ref, gla_ref, conv_s, cum_s
        self.out = {}
        self.levels = {}

    def setup(self):
        T = MIX_T
        proj_ref = self.proj_ref
        self.lane = lane = lax.broadcasted_iota(I32, (T, LANE), 1)
        self.rowi = lax.broadcasted_iota(I32, (T, T), 0)
        self.coli = lax.broadcasted_iota(I32, (T, T), 1)
        self.causal = self.rowi >= self.coli
        tri = jnp.where(self.causal, 1.0, 0.0).astype(BF16)
        self.row128 = lax.broadcasted_iota(I32, (LANE, LANE), 0)
        self.g_blk = g_blk = proj_ref[:, GT:GT + LANE] + self.bias_ref[...]
        is_lf = (lane >= GATE_LF) & (lane < GATE_DT)
        is_dt = (lane >= GATE_DT) & (lane < GATE_GF)
        self.dt = dt = _softplus(jnp.where(is_lf, -g_blk, g_blk))
        a_row = -jnp.exp(self.alog_ref[...])
        self.cum_g = _cumsum_rows(jnp.where(is_lf, -dt, jnp.where(is_dt, dt * a_row, 0.0)), tri)
        yield
        self.p1_t = jnp.where(lane < GATE_LF, g_blk, self.cum_g).T
        self.dt_t = dt.T
        self.k_ml_t = [proj_ref[:, MK + LANE * p:MK + LANE * (p + 1)].T for p in range(2)]
        raw = proj_ref[:, SXBC:SXBC + SSD_CONV_DIM]
        carry = self.conv_s[...]
        cw_ref = self.cw_ref
        conv = (cw_ref[4:5, :] + _shift_rows(raw, 3, carry) * cw_ref[0:1, :]
                + _shift_rows(raw, 2, carry) * cw_ref[1:2, :]
                + _shift_rows(raw, 1, carry) * cw_ref[2:3, :] + raw * cw_ref[3:4, :])
        self.conv_s[...] = raw[T - SUBLANE:T, :]
        self.xbc = xbc = conv * _sigmoid(conv)
        self.x_ssd_t = [xbc[:, LANE * j:LANE * (j + 1)].T for j in range(SSD_HEADS // 2)]
        self.b_p = b_p = xbc[:, SSD_W:SSD_W + LANE]
        c_p = xbc[:, SSD_W + LANE:SSD_W + 2 * LANE]
        self.c_pb = c_p.astype(BF16)
        self.cb = []
        for g in range(2):
            in_g = (lane >= 64 * g) & (lane < 64 * (g + 1))
            self.cb.append(_dot_nt(jnp.where(in_g, c_p, 0.0).astype(BF16), b_p.astype(BF16)))
            yield
        la = _log_sigmoid(_dot(g_blk.astype(BF16), self.wf2_ref[...]) + self.bf_ref[...]) / GLA_TAU
        yield
        self.cum = cum = _cumsum_rows(la, tri)
        yield
        self.cum_s[...] = cum
        self.cum_t = cum.T
        self.q = proj_ref[:, GQ:GQ + LANE] * (GLA_DK ** -0.5)
        self.k = proj_ref[:, GK:GK + LANE]
        self.k_gla_t = self.k.T
        self.v = proj_ref[:, GV:GV + 256]
        r_s = lax.broadcasted_iota(I32, (LANE, 256), 0)
        c_s = lax.broadcasted_iota(I32, (LANE, 256), 1)
        self.same_head = (r_s >> 5) == (c_s >> 6)
        self.tpos = lax.broadcasted_iota(I32, (T, LANE), 0)

    def chains(self):
        return ([self.ml_head(h) for h in range(ML_HEADS)] + [self.ssd_pair(j) for j in range(SSD_HEADS // 2)]
                + [self.gla_state(), self.gla_diag()] + [self.gla_head(h) for h in range(GLA_HEADS)])

    def ml_head(self, h):
        T = MIX_T
        proj_ref, lane, cum_g, p1_t = self.proj_ref, self.lane, self.cum_g, self.p1_t
        pair, lo = h // 2, 64 * (h % 2)
        hm = (lane >= lo) & (lane < lo + 64)
        qm = jnp.where(hm, proj_ref[:, MQ + LANE * pair:MQ + LANE * (pair + 1)] * 0.125, 0.0).astype(BF16)
        km = jnp.where(hm, proj_ref[:, MK + LANE * pair:MK + LANE * (pair + 1)], 0.0)
        qk = _dot_nt(qm, km.astype(BF16))
        yield
        c_pair = self.caug_ref[LANE * pair:LANE * (pair + 1), :]
        qc = _dot(qm, c_pair.astype(BF16))
        yield
        vaug = (proj_ref[:, MV + LANE * h:MV + LANE * (h + 1)] + jnp.where(lane == 64, 1.0, 0.0)).astype(BF16)
        b_col = cum_g[:, GATE_LF + h:GATE_LF + h + 1]
        b_row = p1_t[GATE_LF + h:GATE_LF + h + 1, :]
        li_row = p1_t[GATE_LI + h:GATE_LI + h + 1, :]
        m_prev = self.m_ref[:, h:h + 1]
        d_mat = jnp.where(self.causal, b_col - b_row + li_row, -jnp.inf)
        yield
        row_max = jnp.max(d_mat, axis=1, keepdims=True)
        yield
        a_col = b_col + m_prev
        mt = jnp.maximum(a_col, row_max)
        s_mat = (qk * jnp.exp(d_mat - mt)).astype(BF16)
        numaug = _dot(s_mat, vaug) + jnp.exp(a_col - mt) * qc
        yield
        den = numaug[:, 64:65]
        hn = numaug / jnp.maximum(jnp.abs(den), jnp.exp(-mt))
        hv = jnp.where(lane < 64, hn, 0.0)
        yield
        self.out["ml", h] = hv * lax.rsqrt(jnp.sum(hv * hv, axis=-1, keepdims=True) / 64.0 + EPS)
        b_last = b_col[T - 1:T, :]
        m_new = mt[T - 1:T, :]
        g_state = jnp.exp(b_last + m_prev - m_new)
        g_s_row = jnp.exp(b_last - b_row + li_row - m_new)
        u = _dot((self.k_ml_t[pair][lo:lo + 64, :] * g_s_row).astype(BF16), vaug)
        yield
        self.caug_ref[LANE * pair + lo:LANE * pair + lo + 64, :] = g_state * c_pair[lo:lo + 64] + u
        self.out["m", h] = m_new

    def ssd_pair(self, j):
        T = MIX_T
        lane, cum_g, p1_t = self.lane, self.cum_g, self.p1_t
        g = j // 2
        x_pair = self.xbc[:, LANE * j:LANE * (j + 1)]
        s_pair = self.ssd_ref[LANE * j:LANE * (j + 1), :]
        inter = _dot_nt(self.c_pb, s_pair.astype(BF16))
        yield
        y_pair = None
        ecs, wrow, dec = [], [], []
        for hh in range(2):
            h = 2 * j + hh
            cs_col = cum_g[:, GATE_DT + h:GATE_DT + h + 1]
            cs_row = p1_t[GATE_DT + h:GATE_DT + h + 1, :]
            dt_row = self.dt_t[GATE_DT + h:GATE_DT + h + 1, :]
            seg = jnp.where(self.causal, cs_col - cs_row, -jnp.inf)
            yield
            sc = (self.cb[g] * jnp.exp(seg) * dt_row).astype(BF16)
            in_h = (lane >= 64 * hh) & (lane < 64 * (hh + 1))
            part = _dot(sc, jnp.where(in_h, x_pair, 0.0).astype(BF16))
            yield
            y_pair = part if y_pair is None else y_pair + part
            cs_last = cs_col[T - 1:T, :]
            ecs.append(jnp.exp(cs_col))
            wrow.append(jnp.exp(cs_last - cs_row) * dt_row)
            dec.append(jnp.exp(cs_last))
        self.out["ssd", j] = y_pair + jnp.where(lane < 64, ecs[0], ecs[1]) * inter
        wx_t = (self.x_ssd_t[j] * jnp.where(self.row128 < 64, wrow[0], wrow[1])).astype(BF16)
        in_g = (lane >= 64 * g) & (lane < 64 * (g + 1))
        u = _dot(wx_t, jnp.where(in_g, self.b_p, 0.0).astype(BF16))
        yield
        self.ssd_ref[LANE * j:LANE * (j + 1), :] = jnp.where(self.row128 < 64, dec[0], dec[1]) * s_pair + u

    def gla_state(self):
        T = MIX_T
        cum = self.cum
        s_gla = self.gla_ref[...]
        self.out["gla_inter"] = _dot((self.q * jnp.exp(cum)).astype(BF16), s_gla.astype(BF16))
        yield
        cum_t = self.cum_t
        u = _dot((self.k_gla_t * jnp.exp(cum_t[:, T - 1:T] - cum_t)).astype(BF16), self.v.astype(BF16))
        yield
        self.gla_ref[...] = jnp.exp(self.cum_t[:, T - 1:T]) * s_gla + jnp.where(self.same_head, u, 0.0)

    def gla_level(self, w):
        if w not in self.levels:
            T = MIX_T
            is_t = (self.tpos & (2 * w - 1)) >= w
            mid = jnp.concatenate(
                [jnp.broadcast_to(self.cum_s[2 * w * blk + w - 1:2 * w * blk + w, :], (2 * w, LANE))
                 for blk in range(T // (2 * w))], axis=0)
            e = jnp.exp(jnp.where(is_t, self.cum - mid, mid - self.cum))
            ql = jnp.where(is_t, self.q * e, 0.0)
            kl = jnp.where(is_t, 0.0, self.k * e).astype(BF16)
            shift = (2 * w).bit_length() - 1
            self.levels[w] = (ql, kl, (self.rowi >> shift) == (self.coli >> shift))
        return self.levels[w]

    def gla_head(self, h):
        lane = self.lane
        in_h = (lane >= GLA_DK * h) & (lane < GLA_DK * (h + 1))
        att = None
        w = GLA_C
        while w < MIX_T:
            ql, kl, same_blk = self.gla_level(w)
            a = jnp.where(same_blk, _dot_nt(jnp.where(in_h, ql, 0.0).astype(BF16), kl), 0.0)
            yield
            att = a if att is None else att + a
            w *= 2
        v_pair = self.v[:, LANE * (h // 2):LANE * (h // 2 + 1)]
        in_half = (lane >= 64) if h % 2 else (lane < 64)
        self.out["gla_o", h] = _dot(att.astype(BF16), jnp.where(in_half, v_pair, 0.0).astype(BF16))
        yield

    def gla_diag(self):
        expand = jnp.where(self.same_head, 1.0, 0.0).astype(BF16)
        nblk = MIX_T // GLA_C
        o = None
        for jj in range(GLA_C):
            def rows(ref, c0, c1):
                return jnp.concatenate(
                    [jnp.broadcast_to(ref[GLA_C * i + jj:GLA_C * i + jj + 1, c0:c1], (GLA_C, c1 - c0))
                     for i in range(nblk)], axis=0)
            k_s = rows(self.proj_ref, GK, GK + LANE)
            c_srow = rows(self.cum_s, 0, LANE)
            v_s = rows(self.proj_ref, GV, GV + 256)
            valid = (self.tpos & (GLA_C - 1)) >= jj
            e = jnp.exp(jnp.where(valid, self.cum - c_srow, -jnp.inf))
            part = _dot((self.q * k_s * e).astype(BF16), expand) * v_s
            yield
            o = part if o is None else o + part
        self.out["gla_diag"] = o

    def finish(self):
        proj_ref, mix_ref, lane, out = self.proj_ref, self.mix_ref, self.lane, self.out
        m_row = self.m_ref[...]
        m_lane = lax.broadcasted_iota(I32, m_row.shape, 1)
        for h in range(ML_HEADS):
            m_row = jnp.where(m_lane == h, out["m", h], m_row)
        self.m_ref[...] = m_row
        y_ml = jnp.concatenate(
            [jnp.where(lane < 64, out["ml", 2 * p], pltpu.roll(out["ml", 2 * p + 1], 64, 1)) for p in range(2)],
            axis=1)
        mix_ref[:, 0:256] = (_sigmoid(proj_ref[:, MO:MO + 256]) * (y_ml * self.mlg_ref[...])).astype(BF16)
        y_s = (jnp.concatenate([out["ssd", j] for j in range(SSD_HEADS // 2)], axis=1)
               + self.dskip_ref[...] * self.xbc[:, :SSD_W])
        z = proj_ref[:, SZ:SZ + SSD_W]
        mix_ref[:, 256:768] = (_seg_rmsnorm(y_s * (z * _sigmoid(z)), 256) * self.ssdg_ref[...]).astype(BF16)
        o = (out["gla_inter"] + out["gla_diag"]
             + jnp.concatenate([out["gla_o", 2 * p] + out["gla_o", 2 * p + 1] for p in range(2)], axis=1))
        gg = proj_ref[:, GG:GG + 256]
        mix_ref[:, 768:1024] = (_seg_rmsnorm(o, 64) * self.glag_ref[...] * (gg * _sigmoid(gg))).astype(BF16)


def _mixer(proj, bsz, seq, bias_row, alog_row, mlg, cwb, dskip, ssdg, wf2p, bfr, glag):
    nt = seq // MIX_T
    rows = MIX_ROWS if bsz % MIX_ROWS == 0 else 1
    row_spec = lambda w: _const_spec((1, w))
    mix, caug, m_o, ssd_o, gla_o = pl.pallas_call(
        _mixer_kernel,
        grid=(bsz // rows, nt),
        in_specs=[pl.BlockSpec((rows, MIX_T, NP), lambda b, t: (b, t, 0)),
                  row_spec(LANE), row_spec(LANE), row_spec(256),
                  _const_spec((SUBLANE, SSD_CONV_DIM)), row_spec(SSD_W), row_spec(SSD_W),
                  _const_spec((LANE, LANE)), row_spec(LANE), row_spec(256)],
        out_specs=[pl.BlockSpec((rows, MIX_T, D_MODEL), lambda b, t: (b, t, 0)),
                   pl.BlockSpec((rows, 256, LANE), lambda b, t: (b, 0, 0)),
                   pl.BlockSpec((rows, 1, LANE), lambda b, t: (b, 0, 0)),
                   pl.BlockSpec((rows, 512, LANE), lambda b, t: (b, 0, 0)),
                   pl.BlockSpec((rows, LANE, 256), lambda b, t: (b, 0, 0))],
        out_shape=[jax.ShapeDtypeStruct((bsz, seq, D_MODEL), BF16),
                   jax.ShapeDtypeStruct((bsz, 256, LANE), F32),
                   jax.ShapeDtypeStruct((bsz, 1, LANE), F32),
                   jax.ShapeDtypeStruct((bsz, 512, LANE), F32),
                   jax.ShapeDtypeStruct((bsz, LANE, 256), F32)],
        scratch_shapes=[pltpu.VMEM((rows, SUBLANE, SSD_CONV_DIM), F32), pltpu.VMEM((rows, MIX_T, LANE), F32)],
        compiler_params=_cparams("arbitrary", "arbitrary"),
        name="mixer",
    )(proj.reshape(bsz, seq, NP), bias_row, alog_row, mlg, cwb, dskip, ssdg, wf2p, bfr, glag)
    return mix.reshape(bsz * seq, D_MODEL), caug, m_o, ssd_o, gla_o


def _ffn_chunks(h2_s, act_s, wup_ref, cwb_ref, wd_ref, o_ref, prev_fn, keep_fn):
    for c in range(N_FF_CHUNK):
        h2 = h2_s[...]
        halves = []
        for ug in range(2):
            col = slice(ug * D_FF + c * FF_CHUNK, ug * D_FF + (c + 1) * FF_CHUNK)
            up = _dot(h2, wup_ref[:, col])
            x1, x2 = prev_fn(col, up)
            keep_fn(col, up)
            halves.append(cwb_ref[3:4, col] + x2 * cwb_ref[0:1, col] + x1 * cwb_ref[1:2, col]
                          + up * cwb_ref[2:3, col])
        u, gate = halves
        act_s[:, c * FF_CHUNK:(c + 1) * FF_CHUNK] = (gate * _sigmoid(gate) * u).astype(BF16)
    o_ref[...] += _dot(act_s[...], wd_ref[...])


def _out_ffn_seq_kernel(x_ref, mix_ref, wo_ref, g2_ref, wup_ref, cwb_ref, wd_ref, o_ref, st_ref, h2_s, act_s):
    @pl.when(pl.program_id(1) == 0)
    def _():
        st_ref[...] = jnp.zeros_like(st_ref)

    xn = x_ref[...] + _dot(mix_ref[...], wo_ref[...])
    o_ref[...] = xn
    h2_s[...] = _rmsnorm_rows(xn, g2_ref[...]).astype(BF16)
    tm = x_ref.shape[0]

    def prev_fn(col, up):
        carry = st_ref[:, col]
        return _shift_rows(up, 1, carry), _shift_rows(up, 2, carry)

    def keep_fn(col, up):
        st_ref[:, col] = up[tm - SUBLANE:tm, :]

    _ffn_chunks(h2_s, act_s, wup_ref, cwb_ref, wd_ref, o_ref, prev_fn, keep_fn)


def _out_ffn_step_kernel(x_ref, mix_ref, wo_ref, g2_ref, wup_ref, cwb_ref, wd_ref, prev_ref, o_ref, up_ref,
                         h2_s, act_s):
    xn = x_ref[...] + _dot(mix_ref[...], wo_ref[...])
    o_ref[...] = xn
    h2_s[...] = _rmsnorm_rows(xn, g2_ref[...]).astype(BF16)

    def prev_fn(col, up):
        return prev_ref[1, :, col], prev_ref[0, :, col]

    def keep_fn(col, up):
        up_ref[:, col] = up

    _ffn_chunks(h2_s, act_s, wup_ref, cwb_ref, wd_ref, o_ref, prev_fn, keep_fn)


def _resident_spec(shape):
    nd = len(shape)
    return pl.BlockSpec(shape, lambda *_: (0,) * nd, pipeline_mode=pl.Buffered(1))


_FFN_W_SPECS = [_resident_spec((D_MODEL, D_MODEL)), _const_spec((1, D_MODEL)),
                _resident_spec((D_MODEL, 2 * D_FF)),
                _const_spec((SUBLANE, 2 * D_FF)),
                _resident_spec((D_FF, D_MODEL))]


def _out_ffn_seq(x2d, mix, bsz, seq, wo, g2, wup, cwb, wd):
    tm = min(512, seq)
    nt = seq // tm
    return pl.pallas_call(
        _out_ffn_seq_kernel,
        grid=(bsz, nt),
        in_specs=[pl.BlockSpec((tm, D_MODEL), lambda b, t: (b * nt + t, 0)),
                  pl.BlockSpec((tm, D_MODEL), lambda b, t: (b * nt + t, 0))] + _FFN_W_SPECS,
        out_specs=[pl.BlockSpec((tm, D_MODEL), lambda b, t: (b * nt + t, 0)),
                   pl.BlockSpec((None, SUBLANE, 2 * D_FF), lambda b, t: (b, 0, 0))],
        out_shape=[jax.ShapeDtypeStruct((bsz * seq, D_MODEL), F32),
                   jax.ShapeDtypeStruct((bsz, SUBLANE, 2 * D_FF), F32)],
        scratch_shapes=[pltpu.VMEM((tm, D_MODEL), BF16), pltpu.VMEM((tm, D_FF), BF16)],
        compiler_params=_cparams("arbitrary", "arbitrary"),
        name="out_ffn_seq",
    )(x2d, mix, wo, g2, wup, cwb, wd)


def _out_ffn_step(x2d, mix, wo, g2, wup, cwb, wd, prev):
    m = x2d.shape[0]
    return pl.pallas_call(
        _out_ffn_step_kernel,
        grid=(1,),
        in_specs=[_const_spec((m, D_MODEL)), _const_spec((m, D_MODEL))] + _FFN_W_SPECS
        + [_const_spec((FFN_CONV - 1, m, 2 * D_FF))],
        out_specs=[_const_spec((m, D_MODEL)), _const_spec((m, 2 * D_FF))],
        out_shape=[jax.ShapeDtypeStruct((m, D_MODEL), F32),
                   jax.ShapeDtypeStruct((m, 2 * D_FF), F32)],
        scratch_shapes=[pltpu.VMEM((m, D_MODEL), BF16), pltpu.VMEM((m, D_FF), BF16)],
        compiler_params=_cparams("arbitrary"),
        name="out_ffn_step",
    )(x2d, mix, wo, g2, wup, cwb, wd, prev)


def _final_norm_kernel(x_ref, g_ref, o_ref):
    o_ref[...] = _rmsnorm_rows(x_ref[...], g_ref[...])


def _final_norm(x2d, g):
    m = x2d.shape[0]
    tm = min(1024, m)
    return pl.pallas_call(
        _final_norm_kernel,
        grid=(m // tm,),
        in_specs=[pl.BlockSpec((tm, D_MODEL), lambda i: (i, 0)), _const_spec((1, D_MODEL))],
        out_specs=pl.BlockSpec((tm, D_MODEL), lambda i: (i, 0)),
        out_shape=jax.ShapeDtypeStruct((m, D_MODEL), F32),
        compiler_params=_cparams("arbitrary"),
        name="final_norm",
    )(x2d, g)


def _hi_lo(x):
    hi = x.astype(BF16)
    return hi, (x - hi.astype(F32)).astype(BF16)


def _repeat_matrix(src0, per):
    r = lax.broadcasted_iota(I32, (LANE, LANE), 0)
    c = lax.broadcasted_iota(I32, (LANE, LANE), 1)
    return jnp.where(r == src0 + (c >> (per.bit_length() - 1)), 1.0, 0.0).astype(BF16)


def _step_mlstm_kernel(proj_ref, bias_ref, mlg_ref, c_ref, n_ref, m_ref, c_out, n_out, m_out, y_ref):
    bsz = proj_ref.shape[0]
    lane = lax.broadcasted_iota(I32, (bsz, LANE), 1)
    g_blk = proj_ref[:, GT:GT + LANE] + bias_ref[...]
    m_all = m_ref[...]
    m_new_all = m_all
    y_parts, n_parts = [], []
    for h in range(ML_HEADS):
        pair, lo = h // 2, 64 * (h % 2)
        hm = (lane >= lo) & (lane < lo + 64)
        li = g_blk[:, GATE_LI + h:GATE_LI + h + 1]
        lf = _log_sigmoid(g_blk[:, GATE_LF + h:GATE_LF + h + 1])
        m0 = m_all[:, h:h + 1]
        mt = jnp.maximum(lf + m0, li)
        w_old = jnp.exp(lf + m0 - mt)
        w_in = jnp.exp(li - mt)
        qp = proj_ref[:, MQ + LANE * pair:MQ + LANE * (pair + 1)] * 0.125
        kp = proj_ref[:, MK + LANE * pair:MK + LANE * (pair + 1)]
        vs = proj_ref[:, MV + LANE * h:MV + LANE * (h + 1)]
        vt = vs + pltpu.roll(vs, 64, 1)
        stack = jnp.concatenate(_hi_lo(qp) + _hi_lo(kp), axis=0)
        acc = jnp.zeros((bsz, LANE), F32)
        for j in range(ML_DK // 2):
            ex = _dot(stack, _repeat_matrix(lo + 2 * j, 64))
            qe = ex[0:bsz] + ex[bsz:2 * bsz]
            ke = ex[2 * bsz:3 * bsz] + ex[3 * bsz:4 * bsz]
            c0 = h * ML_DK * 64 + LANE * j
            cn = w_old * c_ref[:, c0:c0 + LANE] + (w_in * ke) * vt
            c_out[:, c0:c0 + LANE] = cn
            acc = acc + qe * cn
        num = acc + pltpu.roll(acc, 64, 1)
        nn = w_old * n_ref[:, LANE * pair:LANE * (pair + 1)] + w_in * kp
        den = jnp.sum(jnp.where(hm, qp * nn, 0.0), axis=-1, keepdims=True)
        hn = num / jnp.maximum(jnp.abs(den), jnp.exp(-mt))
        hv = jnp.where(lane < 64, hn, 0.0)
        y_parts.append(hv * lax.rsqrt(jnp.sum(hv * hv, axis=-1, keepdims=True) / 64.0 + EPS))
        n_parts.append(jnp.where(hm, nn, 0.0))
        m_new_all = jnp.where(lane == h, mt, m_new_all)
    m_out[...] = m_new_all
    n_out[...] = jnp.concatenate([n_parts[0] + n_parts[1], n_parts[2] + n_parts[3]], axis=1)
    y_ml = jnp.concatenate(
        [jnp.where(lane < 64, y_parts[2 * p], pltpu.roll(y_parts[2 * p + 1], 64, 1)) for p in range(2)], axis=1)
    y_ref[...] = _sigmoid(proj_ref[:, MO:MO + 256]) * (y_ml * mlg_ref[...])


def _step_mlstm(proj, bias_row, mlg, c2, n2, mpad):
    bsz = proj.shape[0]
    shapes = [(bsz, ML_HEADS * ML_DK * 64), (bsz, 256), (bsz, LANE), (bsz, 256)]
    return pl.pallas_call(
        _step_mlstm_kernel,
        grid=(1,),
        in_specs=[_const_spec((bsz, NP)), _const_spec((1, LANE)), _const_spec((1, 256)),
                  _const_spec(shapes[0]), _const_spec(shapes[1]), _const_spec(shapes[2])],
        out_specs=[_const_spec(s) for s in shapes],
        out_shape=[jax.ShapeDtypeStruct(s, F32) for s in shapes],
        compiler_params=_cparams("arbitrary"),
        name="step_mlstm",
    )(proj, bias_row, mlg, c2, n2, mpad)


def _step_ssd_kernel(gate_ref, x_ref, z_ref, bc_ref, buf_x_ref, buf_bc_ref, bias_ref, alog_ref,
                     cwx_ref, cwbc_ref, dskip_ref, ssdg_ref, s_ref, s_out, y_ref):
    g = pl.program_id(0)
    bsz = x_ref.shape[0]
    lane = lax.broadcasted_iota(I32, (bsz, LANE), 1)
    lane1 = lax.broadcasted_iota(I32, (1, LANE), 1)
    g_blk = gate_ref[...] + bias_ref[...]
    dt_all = _softplus(g_blk)
    a_row = -jnp.exp(alog_ref[...])

    def conv4(cur, buf_ref, cw_ref):
        wdt = cur.shape[1]
        acc = cw_ref[4:5, :] + cur * cw_ref[3:4, :]
        for j in range(SSD_CONV - 1):
            acc = acc + buf_ref[:, wdt * j:wdt * (j + 1)] * cw_ref[j:j + 1, :]
        return acc * _sigmoid(acc)

    xc = conv4(x_ref[...], buf_x_ref, cwx_ref)
    bc = conv4(bc_ref[...], buf_bc_ref, cwbc_ref)
    b_p, c_p = bc[:, :LANE], bc[:, LANE:]
    b_al = jnp.where(g == 0, b_p, pltpu.roll(b_p, 64, 1))
    c_al = jnp.where(g == 0, c_p, pltpu.roll(c_p, 64, 1))
    b_t = jnp.where(lane < 64, b_al, pltpu.roll(b_al, 64, 1))
    c_t = jnp.where(lane < 64, c_al, pltpu.roll(c_al, 64, 1))
    rr = lax.broadcasted_iota(I32, (LANE, LANE), 0)
    cc = lax.broadcasted_iota(I32, (LANE, LANE), 1)
    ys = []
    for p in range(2):
        x_pair = xc[:, LANE * p:LANE * (p + 1)]
        y_pair = jnp.zeros((bsz, LANE), F32)
        for hh2 in range(2):
            hh = 2 * p + hh2
            lo = 64 * hh2
            sel = lane == GATE_DT + 4 * g + hh
            dt_col = jnp.sum(jnp.where(sel, dt_all, 0.0), axis=-1, keepdims=True)
            a_h = jnp.sum(jnp.where(lane1 == GATE_DT + 4 * g + hh, a_row, 0.0), axis=-1, keepdims=True)
            d_a = jnp.exp(dt_col * a_h)
            stack = jnp.concatenate(_hi_lo(dt_col * x_pair), axis=0)
            for j in range(32):
                ue = _dot(stack, _repeat_matrix(lo + 2 * j, 64))
                c0 = hh * 4096 + LANE * j
                sn = d_a * s_ref[:, c0:c0 + LANE] + (ue[0:bsz] + ue[bsz:2 * bsz]) * b_t
                s_out[:, c0:c0 + LANE] = sn
                place = jnp.where(cc == lo + 2 * j + (rr >> 6), 1.0, 0.0).astype(BF16)
                yy = _dot(jnp.concatenate(_hi_lo(sn * c_t), axis=0), place)
                y_pair = y_pair + yy[0:bsz] + yy[bsz:2 * bsz]
        ys.append(y_pair)
    y_s = jnp.concatenate(ys, axis=1) + dskip_ref[...] * xc
    z = z_ref[...]
    y_ref[...] = _seg_rmsnorm(y_s * (z * _sigmoid(z)), 256) * ssdg_ref[...]


def _step_ssd(proj, buf2d, bias_row, alog_row, cwb, dskip, ssdg, s2):
    bsz = proj.shape[0]
    gw = 256
    return pl.pallas_call(
        _step_ssd_kernel,
        grid=(2,),
        in_specs=[pl.BlockSpec((bsz, LANE), lambda g: (0, GT // LANE)),
                  pl.BlockSpec((bsz, gw), lambda g: (0, SXBC // gw + g)),
                  pl.BlockSpec((bsz, gw), lambda g: (0, SZ // gw + g)),
                  pl.BlockSpec((bsz, gw), lambda g: (0, (SXBC + SSD_W) // gw)),
                  pl.BlockSpec((bsz, 3 * gw), lambda g: (0, g)),
                  pl.BlockSpec((bsz, 3 * gw), lambda g: (0, 2)),
                  _const_spec((1, LANE)), _const_spec((1, LANE)),
                  pl.BlockSpec((SUBLANE, gw), lambda g: (0, g)),
                  pl.BlockSpec((SUBLANE, gw), lambda g: (0, 2)),
                  pl.BlockSpec((1, gw), lambda g: (0, g)),
                  pl.BlockSpec((1, gw), lambda g: (0, g)),
                  pl.BlockSpec((bsz, 4 * 4096), lambda g: (0, g))],
        out_specs=[pl.BlockSpec((bsz, 4 * 4096), lambda g: (0, g)),
                   pl.BlockSpec((bsz, gw), lambda g: (0, g))],
        out_shape=[jax.ShapeDtypeStruct((bsz, SSD_HEADS * 4096), F32),
                   jax.ShapeDtypeStruct((bsz, SSD_W), F32)],
        compiler_params=_cparams("arbitrary"),
        name="step_ssd",
    )(proj, proj, proj, proj, buf2d, buf2d, bias_row, alog_row, cwb, cwb, dskip, ssdg, s2)


def _step_gla_kernel(proj_ref, wf2_ref, bf_ref, glag_ref, s_ref, s_out, y_ref):
    bsz = proj_ref.shape[0]
    lane = lax.broadcasted_iota(I32, (bsz, LANE), 1)
    la = _log_sigmoid(_dot(proj_ref[:, GT:GT + LANE].astype(BF16), wf2_ref[...]) + bf_ref[...]) / GLA_TAU
    q = proj_ref[:, GQ:GQ + LANE] * (GLA_DK ** -0.5)
    k = proj_ref[:, GK:GK + LANE]
    stack = jnp.concatenate(_hi_lo(q) + _hi_lo(k) + _hi_lo(jnp.exp(la)), axis=0)
    o_parts = []
    for h in range(GLA_HEADS):
        v_pair = proj_ref[:, GV + LANE * (h // 2):GV + LANE * (h // 2 + 1)]
        v_al = v_pair if h % 2 == 0 else pltpu.roll(v_pair, 64, 1)
        v_t = jnp.where(lane < 64, v_al, pltpu.roll(v_al, 64, 1))
        acc = jnp.zeros((bsz, LANE), F32)
        for j in range(GLA_DK // 2):
            ex = _dot(stack, _repeat_matrix(GLA_DK * h + 2 * j, 64))
            qe = ex[0:bsz] + ex[bsz:2 * bsz]
            ke = ex[2 * bsz:3 * bsz] + ex[3 * bsz:4 * bsz]
            de = ex[4 * bsz:5 * bsz] + ex[5 * bsz:6 * bsz]
            c0 = h * GLA_DK * 64 + LANE * j
            sn = de * s_ref[:, c0:c0 + LANE] + ke * v_t
            s_out[:, c0:c0 + LANE] = sn
            acc = acc + qe * sn
        o_parts.append(jnp.where(lane < 64, acc + pltpu.roll(acc, 64, 1), 0.0))
    o = jnp.concatenate(
        [jnp.where(lane < 64, o_parts[2 * p], pltpu.roll(o_parts[2 * p + 1], 64, 1)) for p in range(2)], axis=1)
    gg = proj_ref[:, GG:GG + 256]
    y_ref[...] = _seg_rmsnorm(o, 64) * glag_ref[...] * (gg * _sigmoid(gg))


def _step_gla(proj, wf2p, bfr, glag, s2):
    bsz = proj.shape[0]
    sw = GLA_HEADS * GLA_DK * 64
    return pl.pallas_call(
        _step_gla_kernel,
        grid=(1,),
        in_specs=[_const_spec((bsz, NP)), _const_spec((LANE, LANE)), _const_spec((1, LANE)),
                  _const_spec((1, 256)), _const_spec((bsz, sw))],
        out_specs=[_const_spec((bsz, sw)), _const_spec((bsz, 256))],
        out_shape=[jax.ShapeDtypeStruct((bsz, sw), F32), jax.ShapeDtypeStruct((bsz, 256), F32)],
        compiler_params=_cparams("arbitrary"),
        name="step_gla",
    )(proj, wf2p, bfr, glag, s2)


def _prep_w_in(w_in):
    d = w_in.shape[0]
    parts, acc = [], 0
    for s in IN_SIZES:
        parts.append(w_in[..., acc:acc + s])
        acc += s
    mq, mk, mv, mo, mi, mf, sz, sxbc, sdt, gq, gk, gv, gg, gf = parts
    lead = w_in.shape[:-1]
    mv_pad = jnp.pad(mv.reshape(lead + (ML_HEADS, 64)), [(0, 0)] * len(lead) + [(0, 0), (0, 64)])
    mv_pad = mv_pad.reshape(lead + (ML_HEADS * LANE,))
    gates = jnp.concatenate([mi, mf, sdt, gf, jnp.zeros(lead + (LANE - 32,), w_in.dtype)], axis=-1)
    del d
    return jnp.concatenate([mq, mk, mv_pad, mo, sz, sxbc, gq, gk, gv, gg, gates], axis=-1).astype(BF16)


def _pad_row(x, width):
    return jnp.pad(x, [(0, 0), (0, width - x.shape[-1])])


def kernel(x_prompt, x_sample, state_mlstm_c, state_mlstm_n, state_mlstm_m, state_ssd, state_ssd_conv,
           state_gla, state_ffn_conv, norm1_g, w_in, mlstm_b_i, mlstm_b_f, mlstm_norm_g, ssd_conv_w,
           ssd_conv_b, ssd_dt_bias, ssd_a_log, ssd_d, ssd_norm_g, gla_w_f2, gla_b_f, gla_norm_g, w_out,
           norm2_g, w_up, ffn_conv_w, ffn_conv_b, w_down, final_norm_g):
    bsz, seq, _ = x_prompt.shape
    dbs = x_sample.shape[0]
    depth = w_in.shape[0]

    w_in_p = _prep_w_in(w_in)
    w_out_b = w_out.astype(BF16)
    w_up_p = w_up.astype(BF16)
    w_down_p = w_down.astype(BF16)
    ffn_cwb = jnp.concatenate([ffn_conv_w, ffn_conv_b[:, None, :],
                               jnp.zeros((depth, SUBLANE - FFN_CONV - 1, 2 * D_FF), F32)], axis=1)
    zeros8 = jnp.zeros((depth, 8), F32)
    bias_rows = _pad_row(jnp.concatenate([mlstm_b_i, mlstm_b_f, ssd_dt_bias], axis=-1), LANE)[:, None, :]
    alog_rows = _pad_row(jnp.concatenate([zeros8, ssd_a_log], axis=-1), LANE)[:, None, :]
    ssd_cwb = jnp.concatenate([ssd_conv_w, ssd_conv_b[:, None, :],
                               jnp.zeros((depth, SUBLANE - SSD_CONV - 1, SSD_CONV_DIM), F32)], axis=1)
    dskip_rows = jnp.repeat(ssd_d, 64, axis=-1)[:, None, :]
    wf2_p = jnp.pad(gla_w_f2, [(0, 0), (GATE_GF, LANE - GATE_GF - GLA_RANK), (0, 0)]).astype(BF16)

    xp = x_prompt.reshape(bsz * seq, D_MODEL)
    xs = x_sample.reshape(dbs, D_MODEL)
    p_states, s_states = [], []
    for i in range(depth):
        g1 = norm1_g[i][None, :]
        g2 = norm2_g[i][None, :]
        mlg = mlstm_norm_g[i][None, :]
        ssdg = ssd_norm_g[i][None, :]
        glag = gla_norm_g[i][None, :]
        bfr = gla_b_f[i][None, :]

        proj = _norm_proj(xp, g1, w_in_p[i])
        mix, caug, m_o, ssd_o, gla_o = _mixer(proj, bsz, seq, bias_rows[i], alog_rows[i], mlg, ssd_cwb[i],
                                              dskip_rows[i], ssdg, wf2_p[i], bfr, glag)
        xp, ffn_st = _out_ffn_seq(xp, mix, bsz, seq, w_out_b[i], g2, w_up_p[i], ffn_cwb[i], w_down_p[i])
        caug4 = caug.reshape(bsz, ML_HEADS, ML_DK, LANE)
        ssd4 = ssd_o.reshape(bsz, SSD_HEADS, 64, LANE)
        gla4 = gla_o.reshape(bsz, GLA_HEADS, GLA_DK, 256)
        p_states.append((
            caug4[..., :64], caug4[..., 64], m_o[:, 0, :ML_HEADS],
            jnp.where(jnp.arange(SSD_HEADS)[None, :, None, None] < 4, ssd4[..., :64], ssd4[..., 64:]),
            proj.reshape(bsz, seq, NP)[:, seq - (SSD_CONV - 1):, SXBC:SXBC + SSD_CONV_DIM],
            jnp.stack([gla4[:, h, :, 64 * h:64 * (h + 1)] for h in range(GLA_HEADS)], axis=1),
            ffn_st[:, SUBLANE - (FFN_CONV - 1):, :],
        ))

        proj_s = _norm_proj(xs, g1, w_in_p[i])
        c_n, n_n, m_n, y_ml = _step_mlstm(proj_s, bias_rows[i], mlg,
                                          state_mlstm_c[i].reshape(dbs, -1), state_mlstm_n[i].reshape(dbs, -1),
                                          _pad_row(state_mlstm_m[i], LANE))
        buf2d = state_ssd_conv[i].reshape(dbs, SSD_CONV - 1, 3, 256).transpose(0, 2, 1, 3).reshape(dbs, -1)
        s_n, y_ssd = _step_ssd(proj_s, buf2d, bias_rows[i], alog_rows[i], ssd_cwb[i],
                               dskip_rows[i], ssdg, state_ssd[i].reshape(dbs, -1))
        g_n, y_gla = _step_gla(proj_s, wf2_p[i], bfr, glag, state_gla[i].reshape(dbs, -1))
        mix_s = jnp.concatenate([y_ml, y_ssd, y_gla], axis=-1).astype(BF16)
        prev = state_ffn_conv[i].transpose(1, 0, 2)
        xs, up_s = _out_ffn_step(xs, mix_s, w_out_b[i], g2, w_up_p[i], ffn_cwb[i], w_down_p[i], prev)
        s_states.append((
            c_n.reshape(state_mlstm_c.shape[1:]), n_n.reshape(state_mlstm_n.shape[1:]), m_n[:, :ML_HEADS],
            s_n.reshape(state_ssd.shape[1:]),
            jnp.concatenate([state_ssd_conv[i][:, 1:], proj_s[:, None, SXBC:SXBC + SSD_CONV_DIM]], axis=1),
            g_n.reshape(state_gla.shape[1:]),
            jnp.concatenate([state_ffn_conv[i][:, 1:], up_s[:, None, :]], axis=1),
        ))

    fg = final_norm_g[None, :]
    y_prompt = _final_norm(xp, fg).reshape(bsz, seq, D_MODEL)
    y_sample = _final_norm(xs, fg).reshape(dbs, 1, D_MODEL)

    def stk(sts, j):
        return jnp.stack([s[j] for s in sts], axis=0)

    return ((y_prompt, y_sample) + tuple(stk(p_states, j) for j in range(7))
            + tuple(stk(s_states, j) for j in range(7)))
```

```python
import functools

import jax
import jax.numpy as jnp
from jax import lax
from jax.experimental import pallas as pl
from jax.experimental.pallas import tpu as pltpu

F32 = jnp.float32
BF16 = jnp.bfloat16
I32 = jnp.int32

D_MODEL = 1024
DEPTH = 4
ML_HEADS = 4
ML_DK = 64
SSD_HEADS = 8
SSD_W = 512
SSD_CONV = 4
SSD_CONV_DIM = 768
GLA_HEADS = 4
GLA_DK = 32
GLA_RANK = 16
GLA_TAU = 16.0
D_FF = 2816
FFN_CONV = 3
EPS = 1e-6
IN_SIZES = (256, 256, 256, 256, 4, 4, 512, 768, 8, 128, 128, 256, 256, 16)

MQ, MK, MV, MO, SZ, SXBC, GQ, GK, GV, GG, GT, NP = (
    0, 256, 512, 1024, 1280, 1792, 2560, 2688, 2816, 3072, 3328, 3456)
GATE_LI, GATE_LF, GATE_DT, GATE_GF = 0, 4, 8, 16

LANE = 128
SUBLANE = 8
MIX_T = 128
MIX_ROWS = 2
GLA_C = 8
FF_CHUNK = 256
N_FF_CHUNK = D_FF // FF_CHUNK
VMEM_LIMIT = 56 * 1024 * 1024


def _cparams(*sem):
    return pltpu.CompilerParams(dimension_semantics=sem if sem else None,
                                vmem_limit_bytes=VMEM_LIMIT)


def _const_spec(shape):
    nd = len(shape)
    return pl.BlockSpec(shape, lambda *_: (0,) * nd)


def _sigmoid(x):
    return 1.0 / (1.0 + jnp.exp(-x))


def _softplus(x):
    return jnp.maximum(x, 0.0) + jnp.log(1.0 + jnp.exp(-jnp.abs(x)))


def _log_sigmoid(x):
    return -_softplus(-x)


def _rmsnorm_rows(x, g):
    ms = jnp.mean(x * x, axis=-1, keepdims=True)
    return x * lax.rsqrt(ms + EPS) * g


def _dot(a, b):
    return jnp.dot(a, b, preferred_element_type=F32)


def _dot_nt(a, b):
    return lax.dot_general(a, b, (((1,), (1,)), ((), ())), preferred_element_type=F32)


def _split3(x):
    x1 = x.astype(BF16)
    r1 = x - x1.astype(F32)
    x2 = r1.astype(BF16)
    x3 = (r1 - x2.astype(F32)).astype(BF16)
    return x1, x2, x3


def _cumsum_rows(x, tri):
    c = _dot(tri, jnp.concatenate(_split3(x), axis=1))
    return c[:, :LANE] + c[:, LANE:2 * LANE] + c[:, 2 * LANE:]


def _shift_rows(x, k, carry):
    r = pltpu.roll(x, k, 0)
    c = pltpu.roll(carry, k, 0)
    row = lax.broadcasted_iota(I32, c.shape, 0)
    head = jnp.where(row < k, c, r[:SUBLANE])
    return jnp.concatenate([head, r[SUBLANE:]], axis=0)


def _seg_rmsnorm(x, seg):
    w = x.shape[1]
    if seg % LANE == 0:
        parts = []
        for s0 in range(0, w, seg):
            xs = x[:, s0:s0 + seg]
            parts.append(xs * lax.rsqrt(jnp.mean(xs * xs, axis=-1, keepdims=True) + EPS))
        return jnp.concatenate(parts, axis=1)
    parts = []
    lane = lax.broadcasted_iota(I32, (x.shape[0], LANE), 1)
    for s0 in range(0, w, LANE):
        xs = x[:, s0:s0 + LANE]
        sq = xs * xs
        lo = jnp.sum(jnp.where(lane < seg, sq, 0.0), axis=-1, keepdims=True)
        hi = jnp.sum(jnp.where(lane >= seg, sq, 0.0), axis=-1, keepdims=True)
        r = jnp.where(lane < seg, lax.rsqrt(lo / seg + EPS), lax.rsqrt(hi / seg + EPS))
        parts.append(xs * r)
    return jnp.concatenate(parts, axis=1)


def _norm_proj_kernel(x_ref, g_ref, w_ref, o_ref):
    h = _rmsnorm_rows(x_ref[...], g_ref[...]).astype(BF16)
    for n0 in range(0, NP, 512):
        n1 = min(n0 + 512, NP)
        o_ref[:, n0:n1] = _dot(h, w_ref[:, n0:n1])


def _norm_proj(x2d, g, w):
    m = x2d.shape[0]
    tm = min(512, m)
    return pl.pallas_call(
        _norm_proj_kernel,
        grid=(m // tm,),
        in_specs=[pl.BlockSpec((tm, D_MODEL), lambda i: (i, 0)),
                  _const_spec((1, D_MODEL)),
                  _const_spec((D_MODEL, NP))],
        out_specs=pl.BlockSpec((tm, NP), lambda i: (i, 0)),
        out_shape=jax.ShapeDtypeStruct((m, NP), F32),
        compiler_params=_cparams("arbitrary"),
        name="norm_proj",
    )(x2d, g, w)


def _mixer_kernel(proj_ref, bias_ref, alog_ref, mlg_ref, cw_ref, dskip_ref, ssdg_ref, wf2_ref,
                  bf_ref, glag_ref,
                  mix_ref, caug_ref, m_ref, ssd_ref, gla_ref,
                  conv_s, cum_s):
    @pl.when(pl.program_id(1) == 0)
    def _():
        caug_ref[...] = jnp.zeros_like(caug_ref)
        m_ref[...] = jnp.zeros_like(m_ref)
        ssd_ref[...] = jnp.zeros_like(ssd_ref)
        gla_ref[...] = jnp.zeros_like(gla_ref)
        conv_s[...] = jnp.zeros_like(conv_s)

    rows = [_MixerRow(proj_ref.at[r], bias_ref, alog_ref, mlg_ref, cw_ref, dskip_ref, ssdg_ref, wf2_ref,
                      bf_ref, glag_ref, mix_ref.at[r], caug_ref.at[r], m_ref.at[r], ssd_ref.at[r],
                      gla_ref.at[r], conv_s.at[r], cum_s.at[r]) for r in range(proj_ref.shape[0])]
    _drive([row.setup() for row in rows])
    _drive([c for row in rows for c in row.chains()])
    for row in rows:
        row.finish()


def _drive(chains):
    chains = list(chains)
    while chains:
        for c in list(chains):
            try:
                next(c)
            except StopIteration:
                chains.remove(c)


class _MixerRow:
    def __init__(self, proj_ref, bias_ref, alog_ref, mlg_ref, cw_ref, dskip_ref, ssdg_ref, wf2_ref,
                 bf_ref, glag_ref, mix_ref, caug_ref, m_ref, ssd_ref, gla_ref, conv_s, cum_s):
        self.proj_ref, self.bias_ref, self.alog_ref, self.mlg_ref = proj_ref, bias_ref, alog_ref, mlg_ref
        self.cw_ref, self.dskip_ref, self.ssdg_ref, self.wf2_ref = cw_ref, dskip_ref, ssdg_ref, wf2_ref
        self.bf_ref, self.glag_ref, self.mix_ref, self.caug_ref = bf_ref, glag_ref, mix_ref, caug_ref
        self.m_ref, self.ssd_ref, self.gla_ref, self.conv_s, self.cum_s = m_ref, ssd_ref, gla_ref, conv_s, cum_s
        self.out = {}
        self.levels = {}

    def setup(self):
        T = MIX_T
        proj_ref = self.proj_ref
        self.lane = lane = lax.broadcasted_iota(I32, (T, LANE), 1)
        self.rowi = lax.broadcasted_iota(I32, (T, T), 0)
        self.coli = lax.broadcasted_iota(I32, (T, T), 1)
        self.causal = self.rowi >= self.coli
        tri = jnp.where(self.causal, 1.0, 0.0).astype(BF16)
        self.row128 = lax.broadcasted_iota(I32, (LANE, LANE), 0)
        self.g_blk = g_blk = proj_ref[:, GT:GT + LANE] + self.bias_ref[...]
        is_lf = (lane >= GATE_LF) & (lane < GATE_DT)
        is_dt = (lane >= GATE_DT) & (lane < GATE_GF)
        self.dt = dt = _softplus(jnp.where(is_lf, -g_blk, g_blk))
        a_row = -jnp.exp(self.alog_ref[...])
        self.cum_g = _cumsum_rows(jnp.where(is_lf, -dt, jnp.where(is_dt, dt * a_row, 0.0)), tri)
        yield
        self.p1_t = jnp.where(lane < GATE_LF, g_blk, self.cum_g).T
        self.dt_t = dt.T
        self.k_ml_t = [proj_ref[:, MK + LANE * p:MK + LANE * (p + 1)].T for p in range(2)]
        raw = proj_ref[:, SXBC:SXBC + SSD_CONV_DIM]
        carry = self.conv_s[...]
        cw_ref = self.cw_ref
        conv = (cw_ref[4:5, :] + _shift_rows(raw, 3, carry) * cw_ref[0:1, :]
                + _shift_rows(raw, 2, carry) * cw_ref[1:2, :]
                + _shift_rows(raw, 1, carry) * cw_ref[2:3, :] + raw * cw_ref[3:4, :])
        self.conv_s[...] = raw[T - SUBLANE:T, :]
        self.xbc = xbc = conv * _sigmoid(conv)
        self.x_ssd_t = [xbc[:, LANE * j:LANE * (j + 1)].T for j in range(SSD_HEADS // 2)]
        self.b_p = b_p = xbc[:, SSD_W:SSD_W + LANE]
        c_p = xbc[:, SSD_W + LANE:SSD_W + 2 * LANE]
        self.c_pb = c_p.astype(BF16)
        self.cb = []
        for g in range(2):
            in_g = (lane >= 64 * g) & (lane < 64 * (g + 1))
            self.cb.append(_dot_nt(jnp.where(in_g, c_p, 0.0).astype(BF16), b_p.astype(BF16)))
            yield
        la = _log_sigmoid(_dot(g_blk.astype(BF16), self.wf2_ref[...]) + self.bf_ref[...]) / GLA_TAU
        yield
        self.cum = cum = _cumsum_rows(la, tri)
        yield
        self.cum_s[...] = cum
        self.cum_t = cum.T
        self.q = proj_ref[:, GQ:GQ + LANE] * (GLA_DK ** -0.5)
        self.k = proj_ref[:, GK:GK + LANE]
        self.k_gla_t = self.k.T
        self.v = proj_ref[:, GV:GV + 256]
        r_s = lax.broadcasted_iota(I32, (LANE, 256), 0)
        c_s = lax.broadcasted_iota(I32, (LANE, 256), 1)
        self.same_head = (r_s >> 5) == (c_s >> 6)
        self.tpos = lax.broadcasted_iota(I32, (T, LANE), 0)

    def chains(self):
        return ([self.ml_head(h) for h in range(ML_HEADS)] + [self.ssd_pair(j) for j in range(SSD_HEADS // 2)]
                + [self.gla_state(), self.gla_diag()] + [self.gla_head(h) for h in range(GLA_HEADS)])

    def ml_head(self, h):
        T = MIX_T
        proj_ref, lane, cum_g, p1_t = self.proj_ref, self.lane, self.cum_g, self.p1_t
        pair, lo = h // 2, 64 * (h % 2)
        hm = (lane >= lo) & (lane < lo + 64)
        qm = jnp.where(hm, proj_ref[:, MQ + LANE * pair:MQ + LANE * (pair + 1)] * 0.125, 0.0).astype(BF16)
        km = jnp.where(hm, proj_ref[:, MK + LANE * pair:MK + LANE * (pair + 1)], 0.0)
        qk = _dot_nt(qm, km.astype(BF16))
        yield
        c_pair = self.caug_ref[LANE * pair:LANE * (pair + 1), :]
        qc = _dot(qm, c_pair.astype(BF16))
        yield
        vaug = (proj_ref[:, MV + LANE * h:MV + LANE * (h + 1)] + jnp.where(lane == 64, 1.0, 0.0)).astype(BF16)
        b_col = cum_g[:, GATE_LF + h:GATE_LF + h + 1]
        b_row = p1_t[GATE_LF + h:GATE_LF + h + 1, :]
        li_row = p1_t[GATE_LI + h:GATE_LI + h + 1, :]
        m_prev = self.m_ref[:, h:h + 1]
        d_mat = jnp.where(self.causal, b_col - b_row + li_row, -jnp.inf)
        yield
        row_max = jnp.max(d_mat, axis=1, keepdims=True)
        yield
        a_col = b_col + m_prev
        mt = jnp.maximum(a_col, row_max)
        s_mat = (qk * jnp.exp(d_mat - mt)).astype(BF16)
        numaug = _dot(s_mat, vaug) + jnp.exp(a_col - mt) * qc
        yield
        den = numaug[:, 64:65]
        hn = numaug / jnp.maximum(jnp.abs(den), jnp.exp(-mt))
        hv = jnp.where(lane < 64, hn, 0.0)
        yield
        self.out["ml", h] = hv * lax.rsqrt(jnp.sum(hv * hv, axis=-1, keepdims=True) / 64.0 + EPS)
        b_last = b_col[T - 1:T, :]
        m_new = mt[T - 1:T, :]
        g_state = jnp.exp(b_last + m_prev - m_new)
        g_s_row = jnp.exp(b_last - b_row + li_row - m_new)
        u = _dot((self.k_ml_t[pair][lo:lo + 64, :] * g_s_row).astype(BF16), vaug)
        yield
        self.caug_ref[LANE * pair + lo:LANE * pair + lo + 64, :] = g_state * c_pair[lo:lo + 64] + u
        self.out["m", h] = m_new

    def ssd_pair(self, j):
        T = MIX_T
        lane, cum_g, p1_t = self.lane, self.cum_g, self.p1_t
        g = j // 2
        x_pair = self.xbc[:, LANE * j:LANE * (j + 1)]
        s_pair = self.ssd_ref[LANE * j:LANE * (j + 1), :]
        inter = _dot_nt(self.c_pb, s_pair.astype(BF16))
        yield
        y_pair = None
        ecs, wrow, dec = [], [], []
        for hh in range(2):
            h = 2 * j + hh
            cs_col = cum_g[:, GATE_DT + h:GATE_DT + h + 1]
            cs_row = p1_t[GATE_DT + h:GATE_DT + h + 1, :]
            dt_row = self.dt_t[GATE_DT + h:GATE_DT + h + 1, :]
            seg = jnp.where(self.causal, cs_col - cs_row, -jnp.inf)
            yield
            sc = (self.cb[g] * jnp.exp(seg) * dt_row).astype(BF16)
            in_h = (lane >= 64 * hh) & (lane < 64 * (hh + 1))
            part = _dot(sc, jnp.where(in_h, x_pair, 0.0).astype(BF16))
            yield
            y_pair = part if y_pair is None else y_pair + part
            cs_last = cs_col[T - 1:T, :]
            ecs.append(jnp.exp(cs_col))
            wrow.append(jnp.exp(cs_last - cs_row) * dt_row)
            dec.append(jnp.exp(cs_last))
        self.out["ssd", j] = y_pair + jnp.where(lane < 64, ecs[0], ecs[1]) * inter
        wx_t = (self.x_ssd_t[j] * jnp.where(self.row128 < 64, wrow[0], wrow[1])).astype(BF16)
        in_g = (lane >= 64 * g) & (lane < 64 * (g + 1))
        u = _dot(wx_t, jnp.where(in_g, self.b_p, 0.0).astype(BF16))
        yield
        self.ssd_ref[LANE * j:LANE * (j + 1), :] = jnp.where(self.row128 < 64, dec[0], dec[1]) * s_pair + u

    def gla_state(self):
        T = MIX_T
        cum = self.cum
        s_gla = self.gla_ref[...]
        self.out["gla_inter"] = _dot((self.q * jnp.exp(cum)).astype(BF16), s_gla.astype(BF16))
        yield
        cum_t = self.cum_t
        u = _dot((self.k_gla_t * jnp.exp(cum_t[:, T - 1:T] - cum_t)).astype(BF16), self.v.astype(BF16))
        yield
        self.gla_ref[...] = jnp.exp(self.cum_t[:, T - 1:T]) * s_gla + jnp.where(self.same_head, u, 0.0)

    def gla_level(self, w):
        if w not in self.levels:
            T = MIX_T
            is_t = (self.tpos & (2 * w - 1)) >= w
            mid = jnp.concatenate(
                [jnp.broadcast_to(self.cum_s[2 * w * blk + w - 1:2 * w * blk + w, :], (2 * w, LANE))
                 for blk in range(T // (2 * w))], axis=0)
            e = jnp.exp(jnp.where(is_t, self.cum - mid, mid - self.cum))
            ql = jnp.where(is_t, self.q * e, 0.0)
            kl = jnp.where(is_t, 0.0, self.k * e).astype(BF16)
            shift = (2 * w).bit_length() - 1
            self.levels[w] = (ql, kl, (self.rowi >> shift) == (self.coli >> shift))
        return self.levels[w]

    def gla_head(self, h):
        lane = self.lane
        in_h = (lane >= GLA_DK * h) & (lane < GLA_DK * (h + 1))
        att = None
        w = GLA_C
        while w < MIX_T:
            ql, kl, same_blk = self.gla_level(w)
            a = jnp.where(same_blk, _dot_nt(jnp.where(in_h, ql, 0.0).astype(BF16), kl), 0.0)
            yield
            att = a if att is None else att + a
            w *= 2
        v_pair = self.v[:, LANE * (h // 2):LANE * (h // 2 + 1)]
        in_half = (lane >= 64) if h % 2 else (lane < 64)
        self.out["gla_o", h] = _dot(att.astype(BF16), jnp.where(in_half, v_pair, 0.0).astype(BF16))
        yield

    def gla_diag(self):
        expand = jnp.where(self.same_head, 1.0, 0.0).astype(BF16)
        nblk = MIX_T // GLA_C
        o = None
        for jj in range(GLA_C):
            def rows(ref, c0, c1):
                return jnp.concatenate(
                    [jnp.broadcast_to(ref[GLA_C * i + jj:GLA_C * i + jj + 1, c0:c1], (GLA_C, c1 - c0))
                     for i in range(nblk)], axis=0)
            k_s = rows(self.proj_ref, GK, GK + LANE)
            c_srow = rows(self.cum_s, 0, LANE)
            v_s = rows(self.proj_ref, GV, GV + 256)
            valid = (self.tpos & (GLA_C - 1)) >= jj
            e = jnp.exp(jnp.where(valid, self.cum - c_srow, -jnp.inf))
            part = _dot((self.q * k_s * e).astype(BF16), expand) * v_s
            yield
            o = part if o is None else o + part
        self.out["gla_diag"] = o

    def finish(self):
        proj_ref, mix_ref, lane, out = self.proj_ref, self.mix_ref, self.lane, self.out
        m_row = self.m_ref[...]
        m_lane = lax.broadcasted_iota(I32, m_row.shape, 1)
        for h in range(ML_HEADS):
            m_row = jnp.where(m_lane == h, out["m", h], m_row)
        self.m_ref[...] = m_row
        y_ml = jnp.concatenate(
            [jnp.where(lane < 64, out["ml", 2 * p], pltpu.roll(out["ml", 2 * p + 1], 64, 1)) for p in range(2)],
            axis=1)
        mix_ref[:, 0:256] = (_sigmoid(proj_ref[:, MO:MO + 256]) * (y_ml * self.mlg_ref[...])).astype(BF16)
        y_s = (jnp.concatenate([out["ssd", j] for j in range(SSD_HEADS // 2)], axis=1)
               + self.dskip_ref[...] * self.xbc[:, :SSD_W])
        z = proj_ref[:, SZ:SZ + SSD_W]
        mix_ref[:, 256:768] = (_seg_rmsnorm(y_s * (z * _sigmoid(z)), 256) * self.ssdg_ref[...]).astype(BF16)
        o = (out["gla_inter"] + out["gla_diag"]
             + jnp.concatenate([out["gla_o", 2 * p] + out["gla_o", 2 * p + 1] for p in range(2)], axis=1))
        gg = proj_ref[:, GG:GG + 256]
        mix_ref[:, 768:1024] = (_seg_rmsnorm(o, 64) * self.glag_ref[...] * (gg * _sigmoid(gg))).astype(BF16)


def _mixer(proj, bsz, seq, bias_row, alog_row, mlg, cwb, dskip, ssdg, wf2p, bfr, glag):
    nt = seq // MIX_T
    rows = MIX_ROWS if bsz % MIX_ROWS == 0 else 1
    row_spec = lambda w: _const_spec((1, w))
    mix, caug, m_o, ssd_o, gla_o = pl.pallas_call(
        _mixer_kernel,
        grid=(bsz // rows, nt),
        in_specs=[pl.BlockSpec((rows, MIX_T, NP), lambda b, t: (b, t, 0)),
                  row_spec(LANE), row_spec(LANE), row_spec(256),
                  _const_spec((SUBLANE, SSD_CONV_DIM)), row_spec(SSD_W), row_spec(SSD_W),
                  _const_spec((LANE, LANE)), row_spec(LANE), row_spec(256)],
        out_specs=[pl.BlockSpec((rows, MIX_T, D_MODEL), lambda b, t: (b, t, 0)),
                   pl.BlockSpec((rows, 256, LANE), lambda b, t: (b, 0, 0)),
                   pl.BlockSpec((rows, 1, LANE), lambda b, t: (b, 0, 0)),
                   pl.BlockSpec((rows, 512, LANE), lambda b, t: (b, 0, 0)),
                   pl.BlockSpec((rows, LANE, 256), lambda b, t: (b, 0, 0))],
        out_shape=[jax.ShapeDtypeStruct((bsz, seq, D_MODEL), BF16),
                   jax.ShapeDtypeStruct((bsz, 256, LANE), F32),
                   jax.ShapeDtypeStruct((bsz, 1, LANE), F32),
                   jax.ShapeDtypeStruct((bsz, 512, LANE), F32),
                   jax.ShapeDtypeStruct((bsz, LANE, 256), F32)],
        scratch_shapes=[pltpu.VMEM((rows, SUBLANE, SSD_CONV_DIM), F32), pltpu.VMEM((rows, MIX_T, LANE), F32)],
        compiler_params=_cparams("arbitrary", "arbitrary"),
        name="mixer",
    )(proj.reshape(bsz, seq, NP), bias_row, alog_row, mlg, cwb, dskip, ssdg, wf2p, bfr, glag)
    return mix.reshape(bsz * seq, D_MODEL), caug, m_o, ssd_o, gla_o


def _ffn_chunks(h2_s, act_s, wup_ref, cwb_ref, wd_ref, o_ref, prev_fn, keep_fn):
    for c in range(N_FF_CHUNK):
        h2 = h2_s[...]
        halves = []
        for ug in range(2):
            col = slice(ug * D_FF + c * FF_CHUNK, ug * D_FF + (c + 1) * FF_CHUNK)
            up = _dot(h2, wup_ref[:, col])
            x1, x2 = prev_fn(col, up)
            keep_fn(col, up)
            halves.append(cwb_ref[3:4, col] + x2 * cwb_ref[0:1, col] + x1 * cwb_ref[1:2, col]
                          + up * cwb_ref[2:3, col])
        u, gate = halves
        act_s[:, c * FF_CHUNK:(c + 1) * FF_CHUNK] = (gate * _sigmoid(gate) * u).astype(BF16)
    o_ref[...] += _dot(act_s[...], wd_ref[...])


def _out_ffn_seq_kernel(x_ref, mix_ref, wo_ref, g2_ref, wup_ref, cwb_ref, wd_ref, o_ref, st_ref, h2_s, act_s):
    @pl.when(pl.program_id(1) == 0)
    def _():
        st_ref[...] = jnp.zeros_like(st_ref)

    xn = x_ref[...] + _dot(mix_ref[...], wo_ref[...])
    o_ref[...] = xn
    h2_s[...] = _rmsnorm_rows(xn, g2_ref[...]).astype(BF16)
    tm = x_ref.shape[0]

    def prev_fn(col, up):
        carry = st_ref[:, col]
        return _shift_rows(up, 1, carry), _shift_rows(up, 2, carry)

    def keep_fn(col, up):
        st_ref[:, col] = up[tm - SUBLANE:tm, :]

    _ffn_chunks(h2_s, act_s, wup_ref, cwb_ref, wd_ref, o_ref, prev_fn, keep_fn)


def _out_ffn_step_kernel(x_ref, mix_ref, wo_ref, g2_ref, wup_ref, cwb_ref, wd_ref, prev_ref, o_ref, up_ref,
                         h2_s, act_s):
    mix = jnp.concatenate([mix_ref[c0:c0 + LANE, :].T for c0 in range(0, D_MODEL, LANE)], axis=1)
    xn = x_ref[...] + _dot(mix.astype(BF16), wo_ref[...])
    o_ref[...] = xn
    h2_s[...] = _rmsnorm_rows(xn, g2_ref[...]).astype(BF16)

    def prev_fn(col, up):
        return prev_ref[1, :, col], prev_ref[0, :, col]

    def keep_fn(col, up):
        up_ref[:, col] = up

    _ffn_chunks(h2_s, act_s, wup_ref, cwb_ref, wd_ref, o_ref, prev_fn, keep_fn)


def _resident_spec(shape):
    nd = len(shape)
    return pl.BlockSpec(shape, lambda *_: (0,) * nd, pipeline_mode=pl.Buffered(1))


_FFN_W_SPECS = [_resident_spec((D_MODEL, D_MODEL)), _const_spec((1, D_MODEL)),
                _resident_spec((D_MODEL, 2 * D_FF)),
                _const_spec((SUBLANE, 2 * D_FF)),
                _resident_spec((D_FF, D_MODEL))]


def _out_ffn_seq(x2d, mix, bsz, seq, wo, g2, wup, cwb, wd):
    tm = min(512, seq)
    nt = seq // tm
    return pl.pallas_call(
        _out_ffn_seq_kernel,
        grid=(bsz, nt),
        in_specs=[pl.BlockSpec((tm, D_MODEL), lambda b, t: (b * nt + t, 0)),
                  pl.BlockSpec((tm, D_MODEL), lambda b, t: (b * nt + t, 0))] + _FFN_W_SPECS,
        out_specs=[pl.BlockSpec((tm, D_MODEL), lambda b, t: (b * nt + t, 0)),
                   pl.BlockSpec((None, SUBLANE, 2 * D_FF), lambda b, t: (b, 0, 0))],
        out_shape=[jax.ShapeDtypeStruct((bsz * seq, D_MODEL), F32),
                   jax.ShapeDtypeStruct((bsz, SUBLANE, 2 * D_FF), F32)],
        scratch_shapes=[pltpu.VMEM((tm, D_MODEL), BF16), pltpu.VMEM((tm, D_FF), BF16)],
        compiler_params=_cparams("arbitrary", "arbitrary"),
        name="out_ffn_seq",
    )(x2d, mix, wo, g2, wup, cwb, wd)


def _out_ffn_step(x2d, mix, wo, g2, wup, cwb, wd, prev):
    m = x2d.shape[0]
    return pl.pallas_call(
        _out_ffn_step_kernel,
        grid=(1,),
        in_specs=[_const_spec((m, D_MODEL)), _const_spec((D_MODEL, m))] + _FFN_W_SPECS
        + [_const_spec((FFN_CONV - 1, m, 2 * D_FF))],
        out_specs=[_const_spec((m, D_MODEL)), _const_spec((m, 2 * D_FF))],
        out_shape=[jax.ShapeDtypeStruct((m, D_MODEL), F32),
                   jax.ShapeDtypeStruct((m, 2 * D_FF), F32)],
        scratch_shapes=[pltpu.VMEM((m, D_MODEL), BF16), pltpu.VMEM((m, D_FF), BF16)],
        compiler_params=_cparams("arbitrary"),
        name="out_ffn_step",
    )(x2d, mix, wo, g2, wup, cwb, wd, prev)


def _final_norm_kernel(x_ref, g_ref, o_ref):
    o_ref[...] = _rmsnorm_rows(x_ref[...], g_ref[...])


def _final_norm(x2d, g):
    m = x2d.shape[0]
    tm = min(1024, m)
    return pl.pallas_call(
        _final_norm_kernel,
        grid=(m // tm,),
        in_specs=[pl.BlockSpec((tm, D_MODEL), lambda i: (i, 0)), _const_spec((1, D_MODEL))],
        out_specs=pl.BlockSpec((tm, D_MODEL), lambda i: (i, 0)),
        out_shape=jax.ShapeDtypeStruct((m, D_MODEL), F32),
        compiler_params=_cparams("arbitrary"),
        name="final_norm",
    )(x2d, g)


def _norm_proj_t_kernel(x_ref, g_ref, wt_ref, o_ref):
    h = _rmsnorm_rows(x_ref[...], g_ref[...]).astype(BF16)
    for n0 in range(0, NP, 512):
        n1 = min(n0 + 512, NP)
        o_ref[n0:n1, :] = _dot_nt(wt_ref[n0:n1, :], h)


def _norm_proj_t(x2d, g, wt):
    m = x2d.shape[0]
    return pl.pallas_call(
        _norm_proj_t_kernel,
        grid=(1,),
        in_specs=[_const_spec((m, D_MODEL)), _const_spec((1, D_MODEL)), _const_spec((NP, D_MODEL))],
        out_specs=_const_spec((NP, m)),
        out_shape=jax.ShapeDtypeStruct((NP, m), F32),
        compiler_params=_cparams("arbitrary"),
        name="norm_proj_t",
    )(x2d, g, wt)


def _pick_row(x, r):
    row = lax.broadcasted_iota(I32, x.shape, 0)
    return jnp.sum(jnp.where(row == r, x, 0.0), axis=0, keepdims=True)


def _rows(ref, start, size):
    return ref[pl.ds(pl.multiple_of(start, size), size), :]


def _step_mlstm_kernel(proj_ref, bias_ref, mlg_ref, c_ref, n_ref, m_ref, c_out, n_out, m_out, y_ref):
    h = pl.program_id(0)
    gates = proj_ref[GT:GT + LANE, :] + bias_ref[...]
    li = _pick_row(gates, GATE_LI + h)
    lf = _log_sigmoid(_pick_row(gates, GATE_LF + h))
    m0 = m_ref[pl.ds(h, 1), :]
    mt = jnp.maximum(lf + m0, li)
    w_old = jnp.exp(lf + m0 - mt)
    q = _rows(proj_ref, MQ + ML_DK * h, ML_DK) * 0.125
    kw = _rows(proj_ref, MK + ML_DK * h, ML_DK) * jnp.exp(li - mt)
    v = proj_ref[pl.ds(pl.multiple_of(MV + LANE * h, LANE), 64), :]
    acc = jnp.zeros_like(v)
    for d in range(ML_DK):
        cn = w_old * c_ref[d] + kw[d:d + 1, :] * v
        c_out[d] = cn
        acc = acc + q[d:d + 1, :] * cn
    nn = w_old * n_ref[...] + kw
    n_out[...] = nn
    den = jnp.sum(q * nn, axis=0, keepdims=True)
    hn = acc / jnp.maximum(jnp.abs(den), jnp.exp(-mt))
    yn = hn * lax.rsqrt(jnp.mean(hn * hn, axis=0, keepdims=True) + EPS)
    y_ref[...] = _sigmoid(_rows(proj_ref, MO + 64 * h, 64)) * (yn * _rows(mlg_ref, 64 * h, 64))
    m_out[pl.ds(h, 1), :] = mt


def _step_mlstm(proj_t, layer, bias_col, mlg_col, c_t, n_t, m_t):
    bsz = proj_t.shape[1]
    return pl.pallas_call(
        _step_mlstm_kernel,
        grid=(ML_HEADS,),
        in_specs=[_const_spec((NP, bsz)), _const_spec((LANE, 1)), _const_spec((256, 1)),
                  pl.BlockSpec((None, None, ML_DK, 64, bsz), lambda h: (layer, h, 0, 0, 0)),
                  pl.BlockSpec((None, None, ML_DK, bsz), lambda h: (layer, h, 0, 0)),
                  pl.BlockSpec((None, ML_HEADS, bsz), lambda h: (layer, 0, 0))],
        out_specs=[pl.BlockSpec((None, ML_DK, 64, bsz), lambda h: (h, 0, 0, 0)),
                   pl.BlockSpec((None, ML_DK, bsz), lambda h: (h, 0, 0)),
                   _const_spec((ML_HEADS, bsz)),
                   pl.BlockSpec((64, bsz), lambda h: (h, 0))],
        out_shape=[jax.ShapeDtypeStruct((ML_HEADS, ML_DK, 64, bsz), F32),
                   jax.ShapeDtypeStruct((ML_HEADS, ML_DK, bsz), F32),
                   jax.ShapeDtypeStruct((ML_HEADS, bsz), F32),
                   jax.ShapeDtypeStruct((256, bsz), F32)],
        compiler_params=_cparams("arbitrary"),
        name="step_mlstm",
    )(proj_t, bias_col, mlg_col, c_t, n_t, m_t)


def _step_ssd_kernel(proj_ref, buf_ref, bias_ref, alog_ref, cw_ref, dskip_ref, ssdg_ref, s_ref,
                     s_out, y_ref, raw_out, xbc_s, yh_s, ys_s):
    h = pl.program_id(0)
    g = h // (SSD_HEADS // 2)

    @pl.when(h == 0)
    def _():
        for c0 in range(0, SSD_CONV_DIM, LANE):
            raw_t = proj_ref[SXBC + c0:SXBC + c0 + LANE, :]
            cw = cw_ref[c0:c0 + LANE, :]
            acc = cw[:, 4:5] + raw_t * cw[:, 3:4]
            for j in range(SSD_CONV - 1):
                acc = acc + buf_ref[j, :, c0:c0 + LANE].T * cw[:, j:j + 1]
            xbc_s[c0:c0 + LANE, :] = acc * _sigmoid(acc)
            raw_out[:, c0:c0 + LANE] = raw_t.T

    dt = _pick_row(_softplus(proj_ref[GT:GT + LANE, :] + bias_ref[...]), GATE_DT + h)
    d_a = jnp.exp(dt * _pick_row(-jnp.exp(alog_ref[...]), GATE_DT + h))
    x_h = _rows(xbc_s, 64 * h, 64)
    b_g = _rows(xbc_s, SSD_W + 64 * g, 64)
    c_g = _rows(xbc_s, SSD_W + LANE + 64 * g, 64)
    u = dt * x_h
    for p in range(64):
        sn = d_a * s_ref[p] + u[p:p + 1, :] * b_g
        s_out[p] = sn
        yh_s[p:p + 1, :] = jnp.sum(sn * c_g, axis=0, keepdims=True)
    z = _rows(proj_ref, SZ + 64 * h, 64)
    ys_s[pl.ds(pl.multiple_of(64 * h, 64), 64), :] = (yh_s[...] + _rows(dskip_ref, 64 * h, 64) * x_h) * (z * _sigmoid(z))

    @pl.when(h % (SSD_HEADS // 2) == SSD_HEADS // 2 - 1)
    def _():
        grp = _rows(ys_s, 256 * g, 256)
        y_ref[...] = (grp * lax.rsqrt(jnp.mean(grp * grp, axis=0, keepdims=True) + EPS)
                      * _rows(ssdg_ref, 256 * g, 256))


def _step_ssd(proj_t, layer, buf_t, bias_col, alog_col, cw_t, dskip_col, ssdg_col, s_t):
    bsz = proj_t.shape[1]
    return pl.pallas_call(
        _step_ssd_kernel,
        grid=(SSD_HEADS,),
        in_specs=[_const_spec((NP, bsz)),
                  pl.BlockSpec((None, SSD_CONV - 1, bsz, SSD_CONV_DIM), lambda h: (layer, 0, 0, 0)),
                  _const_spec((LANE, 1)), _const_spec((LANE, 1)), _const_spec((SSD_CONV_DIM, SUBLANE)),
                  _const_spec((SSD_W, 1)), _const_spec((SSD_W, 1)),
                  pl.BlockSpec((None, None, 64, 64, bsz), lambda h: (layer, h, 0, 0, 0))],
        out_specs=[pl.BlockSpec((None, 64, 64, bsz), lambda h: (h, 0, 0, 0)),
                   pl.BlockSpec((256, bsz), lambda h: (h // (SSD_HEADS // 2), 0)),
                   _const_spec((bsz, SSD_CONV_DIM))],
        out_shape=[jax.ShapeDtypeStruct((SSD_HEADS, 64, 64, bsz), F32),
                   jax.ShapeDtypeStruct((SSD_W, bsz), F32),
                   jax.ShapeDtypeStruct((bsz, SSD_CONV_DIM), F32)],
        scratch_shapes=[pltpu.VMEM((SSD_CONV_DIM, bsz), F32), pltpu.VMEM((64, bsz), F32),
                        pltpu.VMEM((SSD_W, bsz), F32)],
        compiler_params=_cparams("arbitrary"),
        name="step_ssd",
    )(proj_t, buf_t, bias_col, alog_col, cw_t, dskip_col, ssdg_col, s_t)


def _step_gla_kernel(proj_ref, wf2t_ref, bf_ref, glag_ref, s_ref, s_out, y_ref, dec_s):
    h = pl.program_id(0)

    @pl.when(h == 0)
    def _():
        la = _log_sigmoid(_dot(wf2t_ref[...], proj_ref[GT:GT + LANE, :].astype(BF16)) + bf_ref[...]) / GLA_TAU
        dec_s[...] = jnp.exp(la)

    dec = _rows(dec_s, GLA_DK * h, GLA_DK)
    q = _rows(proj_ref, GQ + GLA_DK * h, GLA_DK) * (GLA_DK ** -0.5)
    k = _rows(proj_ref, GK + GLA_DK * h, GLA_DK)
    v = _rows(proj_ref, GV + 64 * h, 64)
    acc = jnp.zeros_like(v)
    for j in range(GLA_DK):
        sn = dec[j:j + 1, :] * s_ref[j] + k[j:j + 1, :] * v
        s_out[j] = sn
        acc = acc + q[j:j + 1, :] * sn
    gg = _rows(proj_ref, GG + 64 * h, 64)
    y_ref[...] = (acc * lax.rsqrt(jnp.mean(acc * acc, axis=0, keepdims=True) + EPS)
                  * _rows(glag_ref, 64 * h, 64) * (gg * _sigmoid(gg)))


def _step_gla(proj_t, layer, wf2_t, bf_col, glag_col, s_t):
    bsz = proj_t.shape[1]
    return pl.pallas_call(
        _step_gla_kernel,
        grid=(GLA_HEADS,),
        in_specs=[_const_spec((NP, bsz)), _const_spec((LANE, LANE)), _const_spec((LANE, 1)),
                  _const_spec((256, 1)),
                  pl.BlockSpec((None, None, GLA_DK, 64, bsz), lambda h: (layer, h, 0, 0, 0))],
        out_specs=[pl.BlockSpec((None, GLA_DK, 64, bsz), lambda h: (h, 0, 0, 0)),
                   pl.BlockSpec((64, bsz), lambda h: (h, 0))],
        out_shape=[jax.ShapeDtypeStruct((GLA_HEADS, GLA_DK, 64, bsz), F32),
                   jax.ShapeDtypeStruct((256, bsz), F32)],
        scratch_shapes=[pltpu.VMEM((LANE, bsz), F32)],
        compiler_params=_cparams("arbitrary"),
        name="step_gla",
    )(proj_t, wf2_t, bf_col, glag_col, s_t)


def _prep_w_in(w_in):
    d = w_in.shape[0]
    parts, acc = [], 0
    for s in IN_SIZES:
        parts.append(w_in[..., acc:acc + s])
        acc += s
    mq, mk, mv, mo, mi, mf, sz, sxbc, sdt, gq, gk, gv, gg, gf = parts
    lead = w_in.shape[:-1]
    mv_pad = jnp.pad(mv.reshape(lead + (ML_HEADS, 64)), [(0, 0)] * len(lead) + [(0, 0), (0, 64)])
    mv_pad = mv_pad.reshape(lead + (ML_HEADS * LANE,))
    gates = jnp.concatenate([mi, mf, sdt, gf, jnp.zeros(lead + (LANE - 32,), w_in.dtype)], axis=-1)
    del d
    return jnp.concatenate([mq, mk, mv_pad, mo, sz, sxbc, gq, gk, gv, gg, gates], axis=-1).astype(BF16)


def _pad_row(x, width):
    return jnp.pad(x, [(0, 0), (0, width - x.shape[-1])])


def kernel(x_prompt, x_sample, state_mlstm_c, state_mlstm_n, state_mlstm_m, state_ssd, state_ssd_conv,
           state_gla, state_ffn_conv, norm1_g, w_in, mlstm_b_i, mlstm_b_f, mlstm_norm_g, ssd_conv_w,
           ssd_conv_b, ssd_dt_bias, ssd_a_log, ssd_d, ssd_norm_g, gla_w_f2, gla_b_f, gla_norm_g, w_out,
           norm2_g, w_up, ffn_conv_w, ffn_conv_b, w_down, final_norm_g):
    bsz, seq, _ = x_prompt.shape
    dbs = x_sample.shape[0]
    depth = w_in.shape[0]

    w_in_p = _prep_w_in(w_in)
    w_out_b = w_out.astype(BF16)
    w_up_p = w_up.astype(BF16)
    w_down_p = w_down.astype(BF16)
    ffn_cwb = jnp.concatenate([ffn_conv_w, ffn_conv_b[:, None, :],
                               jnp.zeros((depth, SUBLANE - FFN_CONV - 1, 2 * D_FF), F32)], axis=1)
    zeros8 = jnp.zeros((depth, 8), F32)
    bias_rows = _pad_row(jnp.concatenate([mlstm_b_i, mlstm_b_f, ssd_dt_bias], axis=-1), LANE)[:, None, :]
    alog_rows = _pad_row(jnp.concatenate([zeros8, ssd_a_log], axis=-1), LANE)[:, None, :]
    ssd_cwb = jnp.concatenate([ssd_conv_w, ssd_conv_b[:, None, :],
                               jnp.zeros((depth, SUBLANE - SSD_CONV - 1, SSD_CONV_DIM), F32)], axis=1)
    dskip_rows = jnp.repeat(ssd_d, 64, axis=-1)[:, None, :]
    wf2_p = jnp.pad(gla_w_f2, [(0, 0), (GATE_GF, LANE - GATE_GF - GLA_RANK), (0, 0)]).astype(BF16)

    w_in_t = w_in_p.transpose(0, 2, 1)
    c_t = state_mlstm_c.transpose(0, 2, 3, 4, 1)
    n_t = state_mlstm_n.transpose(0, 2, 3, 1)
    m_t = state_mlstm_m.transpose(0, 2, 1)
    s_t = state_ssd.transpose(0, 2, 3, 4, 1)
    g_t = state_gla.transpose(0, 2, 3, 4, 1)
    conv_t = state_ssd_conv.transpose(0, 2, 1, 3)

    xp = x_prompt.reshape(bsz * seq, D_MODEL)
    xs = x_sample.reshape(dbs, D_MODEL)
    p_states, s_states = [], []
    for i in range(depth):
        g1 = norm1_g[i][None, :]
        g2 = norm2_g[i][None, :]
        mlg = mlstm_norm_g[i][None, :]
        ssdg = ssd_norm_g[i][None, :]
        glag = gla_norm_g[i][None, :]
        bfr = gla_b_f[i][None, :]

        proj = _norm_proj(xp, g1, w_in_p[i])
        mix, caug, m_o, ssd_o, gla_o = _mixer(proj, bsz, seq, bias_rows[i], alog_rows[i], mlg, ssd_cwb[i],
                                              dskip_rows[i], ssdg, wf2_p[i], bfr, glag)
        xp, ffn_st = _out_ffn_seq(xp, mix, bsz, seq, w_out_b[i], g2, w_up_p[i], ffn_cwb[i], w_down_p[i])
        caug4 = caug.reshape(bsz, ML_HEADS, ML_DK, LANE)
        ssd4 = ssd_o.reshape(bsz, SSD_HEADS, 64, LANE)
        gla4 = gla_o.reshape(bsz, GLA_HEADS, GLA_DK, 256)
        p_states.append((
            caug4[..., :64], caug4[..., 64], m_o[:, 0, :ML_HEADS],
            jnp.where(jnp.arange(SSD_HEADS)[None, :, None, None] < 4, ssd4[..., :64], ssd4[..., 64:]),
            proj.reshape(bsz, seq, NP)[:, seq - (SSD_CONV - 1):, SXBC:SXBC + SSD_CONV_DIM],
            jnp.stack([gla4[:, h, :, 64 * h:64 * (h + 1)] for h in range(GLA_HEADS)], axis=1),
            ffn_st[:, SUBLANE - (FFN_CONV - 1):, :],
        ))

        proj_t = _norm_proj_t(xs, g1, w_in_t[i])
        c_n, n_n, m_n, y_ml = _step_mlstm(proj_t, i, bias_rows[i].T, mlg.T, c_t, n_t, m_t)
        s_n, y_ssd, raw_s = _step_ssd(proj_t, i, conv_t, bias_rows[i].T, alog_rows[i].T, ssd_cwb[i].T,
                                      dskip_rows[i].T, ssdg.T, s_t)
        g_n, y_gla = _step_gla(proj_t, i, wf2_p[i].T, bfr.T, glag.T, g_t)
        mix_t = jnp.concatenate([y_ml, y_ssd, y_gla], axis=0)
        prev = state_ffn_conv[i].transpose(1, 0, 2)
        xs, up_s = _out_ffn_step(xs, mix_t, w_out_b[i], g2, w_up_p[i], ffn_cwb[i], w_down_p[i], prev)
        s_states.append((
            c_n, n_n, m_n, s_n,
            jnp.concatenate([state_ssd_conv[i][:, 1:], raw_s[:, None, :]], axis=1),
            g_n,
            jnp.concatenate([state_ffn_conv[i][:, 1:], up_s[:, None, :]], axis=1),
        ))

    fg = final_norm_g[None, :]
    y_prompt = _final_norm(xp, fg).reshape(bsz, seq, D_MODEL)
    y_sample = _final_norm(xs, fg).reshape(dbs, 1, D_MODEL)

    def stk(sts, j):
        return jnp.stack([s[j] for s in sts], axis=0)

    s_perm = {0: (0, 4, 1, 2, 3), 1: (0, 3, 1, 2), 2: (0, 2, 1), 3: (0, 4, 1, 2, 3), 5: (0, 4, 1, 2, 3)}
    s_out = tuple(stk(s_states, j).transpose(s_perm[j]) if j in s_perm else stk(s_states, j) for j in range(7))
    return (y_prompt, y_sample) + tuple(stk(p_states, j) for j in range(7)) + s_out
```

```python
import functools

import jax
import jax.numpy as jnp
from jax import lax
from jax.experimental import pallas as pl
from jax.experimental.pallas import tpu as pltpu

F32 = jnp.float32
BF16 = jnp.bfloat16
I32 = jnp.int32

D_MODEL = 1024
DEPTH = 4
ML_HEADS = 4
ML_DK = 64
SSD_HEADS = 8
SSD_W = 512
SSD_CONV = 4
SSD_CONV_DIM = 768
GLA_HEADS = 4
GLA_DK = 32
GLA_RANK = 16
GLA_TAU = 16.0
D_FF = 2816
FFN_CONV = 3
EPS = 1e-6
IN_SIZES = (256, 256, 256, 256, 4, 4, 512, 768, 8, 128, 128, 256, 256, 16)

MQ, MK, MV, MO, SZ, SXBC, GQ, GK, GV, GG, GT, NP = (
    0, 256, 512, 768, 1024, 1536, 2304, 2432, 2560, 2816, 3072, 3200)
GATE_LI, GATE_LF, GATE_DT, GATE_GF = 0, 4, 8, 16

LANE = 128
SUBLANE = 8
MIX_T = 128
MIX_ROWS = 2
GLA_C = 8
FFN_TM = 512
FFN_ROWS = 2
FF_CHUNK = 256
N_FF_CHUNK = D_FF // FF_CHUNK
VMEM_LIMIT = 56 * 1024 * 1024


def _cparams(*sem):
    return pltpu.CompilerParams(dimension_semantics=sem if sem else None,
                                vmem_limit_bytes=VMEM_LIMIT)


def _const_spec(shape):
    nd = len(shape)
    return pl.BlockSpec(shape, lambda *_: (0,) * nd)


def _layer_spec(shape, layer, resident=False):
    nd = len(shape)
    return pl.BlockSpec((None,) + tuple(shape), lambda *_: (layer,) + (0,) * nd,
                        pipeline_mode=pl.Buffered(1) if resident else None)


def _sigmoid(x):
    return 1.0 / (1.0 + jnp.exp(-x))


def _softplus(x):
    return jnp.maximum(x, 0.0) + jnp.log(1.0 + jnp.exp(-jnp.abs(x)))


def _log_sigmoid(x):
    return -_softplus(-x)


def _rmsnorm_rows(x, g):
    ms = jnp.mean(x * x, axis=-1, keepdims=True)
    return x * lax.rsqrt(ms + EPS) * g


def _dot(a, b):
    return jnp.dot(a, b, preferred_element_type=F32)


def _dot_nt(a, b):
    return lax.dot_general(a, b, (((1,), (1,)), ((), ())), preferred_element_type=F32)


def _split3(x):
    x1 = x.astype(BF16)
    r1 = x - x1.astype(F32)
    x2 = r1.astype(BF16)
    x3 = (r1 - x2.astype(F32)).astype(BF16)
    return x1, x2, x3


def _cumsum_rows(x, tri):
    c = _dot(tri, jnp.concatenate(_split3(x), axis=1))
    return c[:, :LANE] + c[:, LANE:2 * LANE] + c[:, 2 * LANE:]


def _shift_rows(x, k, carry):
    r = pltpu.roll(x, k, 0)
    c = pltpu.roll(carry, k, 0)
    row = lax.broadcasted_iota(I32, c.shape, 0)
    head = jnp.where(row < k, c, r[:SUBLANE])
    return jnp.concatenate([head, r[SUBLANE:]], axis=0)


def _seg_rmsnorm(x, seg):
    w = x.shape[1]
    if seg % LANE == 0:
        parts = []
        for s0 in range(0, w, seg):
            xs = x[:, s0:s0 + seg]
            parts.append(xs * lax.rsqrt(jnp.mean(xs * xs, axis=-1, keepdims=True) + EPS))
        return jnp.concatenate(parts, axis=1)
    parts = []
    lane = lax.broadcasted_iota(I32, (x.shape[0], LANE), 1)
    for s0 in range(0, w, LANE):
        xs = x[:, s0:s0 + LANE]
        sq = xs * xs
        lo = jnp.sum(jnp.where(lane < seg, sq, 0.0), axis=-1, keepdims=True)
        hi = jnp.sum(jnp.where(lane >= seg, sq, 0.0), axis=-1, keepdims=True)
        r = jnp.where(lane < seg, lax.rsqrt(lo / seg + EPS), lax.rsqrt(hi / seg + EPS))
        parts.append(xs * r)
    return jnp.concatenate(parts, axis=1)


def _drive(chains):
    chains = list(chains)
    while chains:
        for c in list(chains):
            try:
                next(c)
            except StopIteration:
                chains.remove(c)


def _norm_proj_kernel(x_ref, g_ref, w_ref, o_ref):
    h = _rmsnorm_rows(x_ref[...], g_ref[...]).astype(BF16)
    for n0 in range(0, NP, 512):
        n1 = min(n0 + 512, NP)
        o_ref[:, n0:n1] = _dot(h, w_ref[:, n0:n1])


def _norm_proj(x2d, g, w_all, layer):
    m = x2d.shape[0]
    tm = min(512, m)
    return pl.pallas_call(
        _norm_proj_kernel,
        grid=(m // tm,),
        in_specs=[pl.BlockSpec((tm, D_MODEL), lambda i: (i, 0)),
                  _const_spec((1, D_MODEL)),
                  _layer_spec((D_MODEL, NP), layer)],
        out_specs=pl.BlockSpec((tm, NP), lambda i: (i, 0)),
        out_shape=jax.ShapeDtypeStruct((m, NP), F32),
        compiler_params=_cparams("arbitrary"),
        name="norm_proj",
    )(x2d, g, w_all)


def _mixer_kernel(proj_ref, bias_ref, alog_ref, mlg_ref, cw_ref, dskip_ref, ssdg_ref, wf2_ref,
                  bf_ref, glag_ref,
                  mix_ref, caug_ref, m_ref, ssd_ref, gla_ref,
                  conv_s, cum_s):
    @pl.when(pl.program_id(1) == 0)
    def _():
        caug_ref[...] = jnp.zeros_like(caug_ref)
        m_ref[...] = jnp.zeros_like(m_ref)
        ssd_ref[...] = jnp.zeros_like(ssd_ref)
        gla_ref[...] = jnp.zeros_like(gla_ref)
        conv_s[...] = jnp.zeros_like(conv_s)

    rows = [_MixerRow(proj_ref.at[r], bias_ref, alog_ref, mlg_ref, cw_ref, dskip_ref, ssdg_ref, wf2_ref,
                      bf_ref, glag_ref, mix_ref.at[r], caug_ref.at[r], m_ref.at[r], ssd_ref.at[r],
                      gla_ref.at[r], conv_s.at[r], cum_s.at[r]) for r in range(proj_ref.shape[0])]
    _drive([row.setup() for row in rows])
    _drive([c for row in rows for c in row.chains()])
    for row in rows:
        row.finish()


class _MixerRow:
    def __init__(self, proj_ref, bias_ref, alog_ref, mlg_ref, cw_ref, dskip_ref, ssdg_ref, wf2_ref,
                 bf_ref, glag_ref, mix_ref, caug_ref, m_ref, ssd_ref, gla_ref, conv_s, cum_s):
        self.proj_ref, self.bias_ref, self.alog_ref, self.mlg_ref = proj_ref, bias_ref, alog_ref, mlg_ref
        self.cw_ref, self.dskip_ref, self.ssdg_ref, self.wf2_ref = cw_ref, dskip_ref, ssdg_ref, wf2_ref
        self.bf_ref, self.glag_ref, self.mix_ref, self.caug_ref = bf_ref, glag_ref, mix_ref, caug_ref
        self.m_ref, self.ssd_ref, self.gla_ref, self.conv_s, self.cum_s = m_ref, ssd_ref, gla_ref, conv_s, cum_s
        self.out = {}
        self.levels = {}

    def setup(self):
        T = MIX_T
        proj_ref = self.proj_ref
        self.lane = lane = lax.broadcasted_iota(I32, (T, LANE), 1)
        self.rowi = lax.broadcasted_iota(I32, (T, T), 0)
        self.coli = lax.broadcasted_iota(I32, (T, T), 1)
        self.causal = self.rowi >= self.coli
        tri = jnp.where(self.causal, 1.0, 0.0).astype(BF16)
        self.row128 = lax.broadcasted_iota(I32, (LANE, LANE), 0)
        self.g_blk = g_blk = proj_ref[:, GT:GT + LANE] + self.bias_ref[...]
        is_lf = (lane >= GATE_LF) & (lane < GATE_DT)
        is_dt = (lane >= GATE_DT) & (lane < GATE_GF)
        self.dt = dt = _softplus(jnp.where(is_lf, -g_blk, g_blk))
        a_row = -jnp.exp(self.alog_ref[...])
        self.cum_g = _cumsum_rows(jnp.where(is_lf, -dt, jnp.where(is_dt, dt * a_row, 0.0)), tri)
        yield
        self.p1_t = jnp.where(lane < GATE_LF, g_blk, self.cum_g).T
        self.dt_t = dt.T
        self.k_ml_t = [proj_ref[:, MK + LANE * p:MK + LANE * (p + 1)].T for p in range(2)]
        raw = proj_ref[:, SXBC:SXBC + SSD_CONV_DIM]
        carry = self.conv_s[...]
        cw_ref = self.cw_ref
        conv = (cw_ref[4:5, :] + _shift_rows(raw, 3, carry) * cw_ref[0:1, :]
                + _shift_rows(raw, 2, carry) * cw_ref[1:2, :]
                + _shift_rows(raw, 1, carry) * cw_ref[2:3, :] + raw * cw_ref[3:4, :])
        self.conv_s[...] = raw[T - SUBLANE:T, :]
        self.xbc = xbc = conv * _sigmoid(conv)
        self.x_ssd_t = [xbc[:, LANE * j:LANE * (j + 1)].T for j in range(SSD_HEADS // 2)]
        self.b_p = b_p = xbc[:, SSD_W:SSD_W + LANE]
        c_p = xbc[:, SSD_W + LANE:SSD_W + 2 * LANE]
        self.c_pb = c_p.astype(BF16)
        self.cb = []
        for g in range(2):
            in_g = (lane >= 64 * g) & (lane < 64 * (g + 1))
            self.cb.append(_dot_nt(jnp.where(in_g, c_p, 0.0).astype(BF16), b_p.astype(BF16)))
            yield
        la = _log_sigmoid(_dot(g_blk.astype(BF16), self.wf2_ref[...]) + self.bf_ref[...]) / GLA_TAU
        yield
        self.cum = cum = _cumsum_rows(la, tri)
        yield
        self.cum_s[...] = cum
        self.cum_t = cum.T
        self.q = proj_ref[:, GQ:GQ + LANE] * (GLA_DK ** -0.5)
        self.k = proj_ref[:, GK:GK + LANE]
        self.k_gla_t = self.k.T
        self.v = proj_ref[:, GV:GV + 256]
        r_s = lax.broadcasted_iota(I32, (LANE, 256), 0)
        c_s = lax.broadcasted_iota(I32, (LANE, 256), 1)
        self.same_head = (r_s >> 5) == (c_s >> 6)
        self.tpos = lax.broadcasted_iota(I32, (T, LANE), 0)

    def chains(self):
        return ([self.ml_head(h) for h in range(ML_HEADS)] + [self.ssd_pair(j) for j in range(SSD_HEADS // 2)]
                + [self.gla_state(), self.gla_diag()] + [self.gla_head(h) for h in range(GLA_HEADS)])

    def ml_head(self, h):
        T = MIX_T
        proj_ref, lane, cum_g, p1_t = self.proj_ref, self.lane, self.cum_g, self.p1_t
        pair, lo = h // 2, 64 * (h % 2)
        hm = (lane >= lo) & (lane < lo + 64)
        qm = jnp.where(hm, proj_ref[:, MQ + LANE * pair:MQ + LANE * (pair + 1)] * 0.125, 0.0).astype(BF16)
        km = jnp.where(hm, proj_ref[:, MK + LANE * pair:MK + LANE * (pair + 1)], 0.0)
        qk = _dot_nt(qm, km.astype(BF16))
        yield
        c_pair = self.caug_ref[LANE * pair:LANE * (pair + 1), :]
        qc = _dot(qm, c_pair.astype(BF16))
        yield
        v_pair = proj_ref[:, MV + LANE * pair:MV + LANE * (pair + 1)]
        v_lo = pltpu.roll(v_pair, 64, 1) if h % 2 else v_pair
        vaug = jnp.where(lane < 64, v_lo, jnp.where(lane == 64, 1.0, 0.0)).astype(BF16)
        b_col = cum_g[:, GATE_LF + h:GATE_LF + h + 1]
        b_row = p1_t[GATE_LF + h:GATE_LF + h + 1, :]
        li_row = p1_t[GATE_LI + h:GATE_LI + h + 1, :]
        m_prev = self.m_ref[:, h:h + 1]
        d_mat = jnp.where(self.causal, b_col - b_row + li_row, -jnp.inf)
        yield
        row_max = jnp.max(d_mat, axis=1, keepdims=True)
        yield
        a_col = b_col + m_prev
        mt = jnp.maximum(a_col, row_max)
        s_mat = (qk * jnp.exp(d_mat - mt)).astype(BF16)
        numaug = _dot(s_mat, vaug) + jnp.exp(a_col - mt) * qc
        yield
        den = numaug[:, 64:65]
        hn = numaug / jnp.maximum(jnp.abs(den), jnp.exp(-mt))
        hv = jnp.where(lane < 64, hn, 0.0)
        yield
        self.out["ml", h] = hv * lax.rsqrt(jnp.sum(hv * hv, axis=-1, keepdims=True) / 64.0 + EPS)
        b_last = b_col[T - 1:T, :]
        m_new = mt[T - 1:T, :]
        g_state = jnp.exp(b_last + m_prev - m_new)
        g_s_row = jnp.exp(b_last - b_row + li_row - m_new)
        u = _dot((self.k_ml_t[pair][lo:lo + 64, :] * g_s_row).astype(BF16), vaug)
        yield
        self.caug_ref[LANE * pair + lo:LANE * pair + lo + 64, :] = g_state * c_pair[lo:lo + 64] + u
        self.out["m", h] = m_new

    def ssd_pair(self, j):
        T = MIX_T
        lane, cum_g, p1_t = self.lane, self.cum_g, self.p1_t
        g = j // 2
        x_pair = self.xbc[:, LANE * j:LANE * (j + 1)]
        s_pair = self.ssd_ref[LANE * j:LANE * (j + 1), :]
        inter = _dot_nt(self.c_pb, s_pair.astype(BF16))
        yield
        y_pair = None
        ecs, wrow, dec = [], [], []
        for hh in range(2):
            h = 2 * j + hh
            cs_col = cum_g[:, GATE_DT + h:GATE_DT + h + 1]
            cs_row = p1_t[GATE_DT + h:GATE_DT + h + 1, :]
            dt_row = self.dt_t[GATE_DT + h:GATE_DT + h + 1, :]
            seg = jnp.where(self.causal, cs_col - cs_row, -jnp.inf)
            yield
            sc = (self.cb[g] * jnp.exp(seg) * dt_row).astype(BF16)
            in_h = (lane >= 64 * hh) & (lane < 64 * (hh + 1))
            part = _dot(sc, jnp.where(in_h, x_pair, 0.0).astype(BF16))
            yield
            y_pair = part if y_pair is None else y_pair + part
            cs_last = cs_col[T - 1:T, :]
            ecs.append(jnp.exp(cs_col))
            wrow.append(jnp.exp(cs_last - cs_row) * dt_row)
            dec.append(jnp.exp(cs_last))
        self.out["ssd", j] = y_pair + jnp.where(lane < 64, ecs[0], ecs[1]) * inter
        wx_t = (self.x_ssd_t[j] * jnp.where(self.row128 < 64, wrow[0], wrow[1])).astype(BF16)
        in_g = (lane >= 64 * g) & (lane < 64 * (g + 1))
        u = _dot(wx_t, jnp.where(in_g, self.b_p, 0.0).astype(BF16))
        yield
        self.ssd_ref[LANE * j:LANE * (j + 1), :] = jnp.where(self.row128 < 64, dec[0], dec[1]) * s_pair + u

    def gla_state(self):
        T = MIX_T
        cum = self.cum
        s_gla = self.gla_ref[...]
        self.out["gla_inter"] = _dot((self.q * jnp.exp(cum)).astype(BF16), s_gla.astype(BF16))
        yield
        cum_t = self.cum_t
        u = _dot((self.k_gla_t * jnp.exp(cum_t[:, T - 1:T] - cum_t)).astype(BF16), self.v.astype(BF16))
        yield
        self.gla_ref[...] = jnp.exp(self.cum_t[:, T - 1:T]) * s_gla + jnp.where(self.same_head, u, 0.0)

    def gla_level(self, w):
        if w not in self.levels:
            T = MIX_T
            is_t = (self.tpos & (2 * w - 1)) >= w
            mid = jnp.concatenate(
                [jnp.broadcast_to(self.cum_s[2 * w * blk + w - 1:2 * w * blk + w, :], (2 * w, LANE))
                 for blk in range(T // (2 * w))], axis=0)
            e = jnp.exp(jnp.where(is_t, self.cum - mid, mid - self.cum))
            ql = jnp.where(is_t, self.q * e, 0.0)
            kl = jnp.where(is_t, 0.0, self.k * e).astype(BF16)
            shift = (2 * w).bit_length() - 1
            self.levels[w] = (ql, kl, (self.rowi >> shift) == (self.coli >> shift))
        return self.levels[w]

    def gla_head(self, h):
        lane = self.lane
        in_h = (lane >= GLA_DK * h) & (lane < GLA_DK * (h + 1))
        att = None
        w = GLA_C
        while w < MIX_T:
            ql, kl, same_blk = self.gla_level(w)
            a = jnp.where(same_blk, _dot_nt(jnp.where(in_h, ql, 0.0).astype(BF16), kl), 0.0)
            yield
            att = a if att is None else att + a
            w *= 2
        v_pair = self.v[:, LANE * (h // 2):LANE * (h // 2 + 1)]
        in_half = (lane >= 64) if h % 2 else (lane < 64)
        self.out["gla_o", h] = _dot(att.astype(BF16), jnp.where(in_half, v_pair, 0.0).astype(BF16))
        yield

    def gla_diag(self):
        expand = jnp.where(self.same_head, 1.0, 0.0).astype(BF16)
        nblk = MIX_T // GLA_C
        o = None
        for jj in range(GLA_C):
            def rows(ref, c0, c1):
                return jnp.concatenate(
                    [jnp.broadcast_to(ref[GLA_C * i + jj:GLA_C * i + jj + 1, c0:c1], (GLA_C, c1 - c0))
                     for i in range(nblk)], axis=0)
            k_s = rows(self.proj_ref, GK, GK + LANE)
            c_srow = rows(self.cum_s, 0, LANE)
            v_s = rows(self.proj_ref, GV, GV + 256)
            valid = (self.tpos & (GLA_C - 1)) >= jj
            e = jnp.exp(jnp.where(valid, self.cum - c_srow, -jnp.inf))
            part = _dot((self.q * k_s * e).astype(BF16), expand) * v_s
            yield
            o = part if o is None else o + part
        self.out["gla_diag"] = o

    def finish(self):
        proj_ref, mix_ref, lane, out = self.proj_ref, self.mix_ref, self.lane, self.out
        m_row = self.m_ref[...]
        m_lane = lax.broadcasted_iota(I32, m_row.shape, 1)
        for h in range(ML_HEADS):
            m_row = jnp.where(m_lane == h, out["m", h], m_row)
        self.m_ref[...] = m_row
        y_ml = jnp.concatenate(
            [jnp.where(lane < 64, out["ml", 2 * p], pltpu.roll(out["ml", 2 * p + 1], 64, 1)) for p in range(2)],
            axis=1)
        mix_ref[:, 0:256] = (_sigmoid(proj_ref[:, MO:MO + 256]) * (y_ml * self.mlg_ref[...])).astype(BF16)
        y_s = (jnp.concatenate([out["ssd", j] for j in range(SSD_HEADS // 2)], axis=1)
               + self.dskip_ref[...] * self.xbc[:, :SSD_W])
        z = proj_ref[:, SZ:SZ + SSD_W]
        mix_ref[:, 256:768] = (_seg_rmsnorm(y_s * (z * _sigmoid(z)), 256) * self.ssdg_ref[...]).astype(BF16)
        o = (out["gla_inter"] + out["gla_diag"]
             + jnp.concatenate([out["gla_o", 2 * p] + out["gla_o", 2 * p + 1] for p in range(2)], axis=1))
        gg = proj_ref[:, GG:GG + 256]
        mix_ref[:, 768:1024] = (_seg_rmsnorm(o, 64) * self.glag_ref[...] * (gg * _sigmoid(gg))).astype(BF16)


def _mixer(proj, bsz, seq, bias_row, alog_row, mlg, cwb, dskip, ssdg, wf2p, bfr, glag):
    nt = seq // MIX_T
    rows = MIX_ROWS if bsz % MIX_ROWS == 0 else 1
    row_spec = lambda w: _const_spec((1, w))
    mix, caug, m_o, ssd_o, gla_o = pl.pallas_call(
        _mixer_kernel,
        grid=(bsz // rows, nt),
        in_specs=[pl.BlockSpec((rows, MIX_T, NP), lambda b, t: (b, t, 0)),
                  row_spec(LANE), row_spec(LANE), row_spec(256),
                  _const_spec((SUBLANE, SSD_CONV_DIM)), row_spec(SSD_W), row_spec(SSD_W),
                  _const_spec((LANE, LANE)), row_spec(LANE), row_spec(256)],
        out_specs=[pl.BlockSpec((rows, MIX_T, D_MODEL), lambda b, t: (b, t, 0)),
                   pl.BlockSpec((rows, 256, LANE), lambda b, t: (b, 0, 0)),
                   pl.BlockSpec((rows, 1, LANE), lambda b, t: (b, 0, 0)),
                   pl.BlockSpec((rows, 512, LANE), lambda b, t: (b, 0, 0)),
                   pl.BlockSpec((rows, LANE, 256), lambda b, t: (b, 0, 0))],
        out_shape=[jax.ShapeDtypeStruct((bsz, seq, D_MODEL), BF16),
                   jax.ShapeDtypeStruct((bsz, 256, LANE), F32),
                   jax.ShapeDtypeStruct((bsz, 1, LANE), F32),
                   jax.ShapeDtypeStruct((bsz, 512, LANE), F32),
                   jax.ShapeDtypeStruct((bsz, LANE, 256), F32)],
        scratch_shapes=[pltpu.VMEM((rows, SUBLANE, SSD_CONV_DIM), F32), pltpu.VMEM((rows, MIX_T, LANE), F32)],
        compiler_params=_cparams("arbitrary", "arbitrary"),
        name="mixer",
    )(proj.reshape(bsz, seq, NP), bias_row, alog_row, mlg, cwb, dskip, ssdg, wf2p, bfr, glag)
    return mix.reshape(bsz * seq, D_MODEL), caug, m_o, ssd_o, gla_o


def _ffn_row(x_ref, mix_fn, wo_ref, g2_ref, wup_ref, cwb_ref, wd_ref, fg_ref, o_ref, h2_s, act_s,
             prev_fn, keep_fn, final):
    xn = x_ref[...] + _dot(mix_fn(), wo_ref[...])
    yield
    o_ref[...] = xn
    h2_s[...] = _rmsnorm_rows(xn, g2_ref[...]).astype(BF16)
    for c in range(N_FF_CHUNK):
        h2 = h2_s[...]
        halves = []
        for ug in range(2):
            col = slice(ug * D_FF + c * FF_CHUNK, ug * D_FF + (c + 1) * FF_CHUNK)
            up = _dot(h2, wup_ref[:, col])
            yield
            x1, x2 = prev_fn(col, up)
            keep_fn(col, up)
            halves.append(cwb_ref[3:4, col] + x2 * cwb_ref[0:1, col] + x1 * cwb_ref[1:2, col]
                          + up * cwb_ref[2:3, col])
        u, gate = halves
        act_s[:, c * FF_CHUNK:(c + 1) * FF_CHUNK] = (gate * _sigmoid(gate) * u).astype(BF16)
    y = o_ref[...] + _dot(act_s[...], wd_ref[...])
    yield
    o_ref[...] = _rmsnorm_rows(y, fg_ref[...]) if final else y


def _out_ffn_seq_kernel(x_ref, mix_ref, wo_ref, g2_ref, wup_ref, cwb_ref, wd_ref, fg_ref, o_ref, st_ref,
                        h2_s, act_s, *, final):
    @pl.when(pl.program_id(1) == 0)
    def _():
        st_ref[...] = jnp.zeros_like(st_ref)

    tm = x_ref.shape[1]

    def chain(r):
        st = st_ref.at[r]

        def prev_fn(col, up):
            carry = st[:, col]
            return _shift_rows(up, 1, carry), _shift_rows(up, 2, carry)

        def keep_fn(col, up):
            st[:, col] = up[tm - SUBLANE:tm, :]

        return _ffn_row(x_ref.at[r], lambda: mix_ref[r], wo_ref, g2_ref, wup_ref, cwb_ref, wd_ref, fg_ref,
                        o_ref.at[r], h2_s.at[r], act_s.at[r], prev_fn, keep_fn, final)

    _drive([chain(r) for r in range(x_ref.shape[0])])


def _out_ffn_step_kernel(x_ref, mix_ref, wo_ref, g2_ref, wup_ref, cwb_ref, wd_ref, fg_ref, prev_ref, o_ref, up_ref,
                         h2_s, act_s, *, final):
    def mix_fn():
        mix = jnp.concatenate([mix_ref[c0:c0 + LANE, :].T for c0 in range(0, D_MODEL, LANE)], axis=1)
        return mix.astype(BF16)

    def prev_fn(col, up):
        return prev_ref[1, :, col], prev_ref[0, :, col]

    def keep_fn(col, up):
        up_ref[:, col] = up

    _drive([_ffn_row(x_ref, mix_fn, wo_ref, g2_ref, wup_ref, cwb_ref, wd_ref, fg_ref, o_ref, h2_s, act_s,
                     prev_fn, keep_fn, final)])


def _ffn_w_specs(layer):
    return [_layer_spec((D_MODEL, D_MODEL), layer, resident=True), _const_spec((1, D_MODEL)),
            _layer_spec((D_MODEL, 2 * D_FF), layer, resident=True),
            _const_spec((SUBLANE, 2 * D_FF)),
            _layer_spec((D_FF, D_MODEL), layer, resident=True), _const_spec((1, D_MODEL))]


def _out_ffn_seq(x2d, mix, bsz, seq, layer, wo, g2, wup, cwb, wd, fg, final):
    tm = min(FFN_TM, seq)
    nt = seq // tm
    rows = FFN_ROWS if bsz % FFN_ROWS == 0 else 1
    tile = pl.BlockSpec((rows, tm, D_MODEL), lambda b, t: (b, t, 0))
    out, st = pl.pallas_call(
        functools.partial(_out_ffn_seq_kernel, final=final),
        grid=(bsz // rows, nt),
        in_specs=[tile, tile] + _ffn_w_specs(layer),
        out_specs=[tile, pl.BlockSpec((rows, SUBLANE, 2 * D_FF), lambda b, t: (b, 0, 0))],
        out_shape=[jax.ShapeDtypeStruct((bsz, seq, D_MODEL), F32),
                   jax.ShapeDtypeStruct((bsz, SUBLANE, 2 * D_FF), F32)],
        scratch_shapes=[pltpu.VMEM((rows, tm, D_MODEL), BF16), pltpu.VMEM((rows, tm, D_FF), BF16)],
        compiler_params=_cparams("arbitrary", "arbitrary"),
        name="out_ffn_seq",
    )(x2d.reshape(bsz, seq, D_MODEL), mix.reshape(bsz, seq, D_MODEL), wo, g2, wup, cwb, wd, fg)
    return out.reshape(bsz * seq, D_MODEL), st


def _out_ffn_step(x2d, mix, layer, wo, g2, wup, cwb, wd, fg, final, prev):
    m = x2d.shape[0]
    return pl.pallas_call(
        functools.partial(_out_ffn_step_kernel, final=final),
        grid=(1,),
        in_specs=[_const_spec((m, D_MODEL)), _const_spec((D_MODEL, m))] + _ffn_w_specs(layer)
        + [_const_spec((FFN_CONV - 1, m, 2 * D_FF))],
        out_specs=[_const_spec((m, D_MODEL)), _const_spec((m, 2 * D_FF))],
        out_shape=[jax.ShapeDtypeStruct((m, D_MODEL), F32),
                   jax.ShapeDtypeStruct((m, 2 * D_FF), F32)],
        scratch_shapes=[pltpu.VMEM((m, D_MODEL), BF16), pltpu.VMEM((m, D_FF), BF16)],
        compiler_params=_cparams("arbitrary"),
        name="out_ffn_step",
    )(x2d, mix, wo, g2, wup, cwb, wd, fg, prev)


def _norm_proj_t_kernel(x_ref, g_ref, wt_ref, o_ref):
    h = _rmsnorm_rows(x_ref[...], g_ref[...]).astype(BF16)
    for n0 in range(0, NP, 512):
        n1 = min(n0 + 512, NP)
        o_ref[n0:n1, :] = _dot_nt(wt_ref[n0:n1, :], h)


def _norm_proj_t(x2d, g, wt_all, layer):
    m = x2d.shape[0]
    return pl.pallas_call(
        _norm_proj_t_kernel,
        grid=(1,),
        in_specs=[_const_spec((m, D_MODEL)), _const_spec((1, D_MODEL)), _layer_spec((NP, D_MODEL), layer)],
        out_specs=_const_spec((NP, m)),
        out_shape=jax.ShapeDtypeStruct((NP, m), F32),
        compiler_params=_cparams("arbitrary"),
        name="norm_proj_t",
    )(x2d, g, wt_all)


def _pick_row(x, r):
    row = lax.broadcasted_iota(I32, x.shape, 0)
    return jnp.sum(jnp.where(row == r, x, 0.0), axis=0, keepdims=True)


def _rows(ref, start, size):
    return ref[pl.ds(pl.multiple_of(start, size), size), :]


def _step_mlstm_kernel(proj_ref, bias_ref, mlg_ref, c_ref, n_ref, m_ref, c_out, n_out, m_out, y_ref):
    h = pl.program_id(0)
    gates = proj_ref[GT:GT + LANE, :] + bias_ref[...]
    li = _pick_row(gates, GATE_LI + h)
    lf = _log_sigmoid(_pick_row(gates, GATE_LF + h))
    m0 = m_ref[pl.ds(h, 1), :]
    mt = jnp.maximum(lf + m0, li)
    w_old = jnp.exp(lf + m0 - mt)
    q = _rows(proj_ref, MQ + ML_DK * h, ML_DK) * 0.125
    kw = _rows(proj_ref, MK + ML_DK * h, ML_DK) * jnp.exp(li - mt)
    v = _rows(proj_ref, MV + 64 * h, 64)
    acc = jnp.zeros_like(v)
    for d in range(ML_DK):
        cn = w_old * c_ref[d] + kw[d:d + 1, :] * v
        c_out[d] = cn
        acc = acc + q[d:d + 1, :] * cn
    nn = w_old * n_ref[...] + kw
    n_out[...] = nn
    den = jnp.sum(q * nn, axis=0, keepdims=True)
    hn = acc / jnp.maximum(jnp.abs(den), jnp.exp(-mt))
    yn = hn * lax.rsqrt(jnp.mean(hn * hn, axis=0, keepdims=True) + EPS)
    y_ref[...] = _sigmoid(_rows(proj_ref, MO + 64 * h, 64)) * (yn * _rows(mlg_ref, 64 * h, 64))
    m_out[pl.ds(h, 1), :] = mt


def _step_mlstm(proj_t, layer, bias_col, mlg_col, c_t, n_t, m_t):
    bsz = proj_t.shape[1]
    return pl.pallas_call(
        _step_mlstm_kernel,
        grid=(ML_HEADS,),
        in_specs=[_const_spec((NP, bsz)), _const_spec((LANE, 1)), _const_spec((256, 1)),
                  pl.BlockSpec((None, None, ML_DK, 64, bsz), lambda h: (layer, h, 0, 0, 0)),
                  pl.BlockSpec((None, None, ML_DK, bsz), lambda h: (layer, h, 0, 0)),
                  pl.BlockSpec((None, ML_HEADS, bsz), lambda h: (layer, 0, 0))],
        out_specs=[pl.BlockSpec((None, ML_DK, 64, bsz), lambda h: (h, 0, 0, 0)),
                   pl.BlockSpec((None, ML_DK, bsz), lambda h: (h, 0, 0)),
                   _const_spec((ML_HEADS, bsz)),
                   pl.BlockSpec((64, bsz), lambda h: (h, 0))],
        out_shape=[jax.ShapeDtypeStruct((ML_HEADS, ML_DK, 64, bsz), F32),
                   jax.ShapeDtypeStruct((ML_HEADS, ML_DK, bsz), F32),
                   jax.ShapeDtypeStruct((ML_HEADS, bsz), F32),
                   jax.ShapeDtypeStruct((256, bsz), F32)],
        compiler_params=_cparams("arbitrary"),
        name="step_mlstm",
    )(proj_t, bias_col, mlg_col, c_t, n_t, m_t)


def _step_ssd_kernel(proj_ref, buf_ref, bias_ref, alog_ref, cw_ref, dskip_ref, ssdg_ref, s_ref,
                     s_out, y_ref, raw_out, xbc_s, yh_s, ys_s):
    h = pl.program_id(0)
    g = h // (SSD_HEADS // 2)

    @pl.when(h == 0)
    def _():
        for c0 in range(0, SSD_CONV_DIM, LANE):
            raw_t = proj_ref[SXBC + c0:SXBC + c0 + LANE, :]
            cw = cw_ref[c0:c0 + LANE, :]
            acc = cw[:, 4:5] + raw_t * cw[:, 3:4]
            for j in range(SSD_CONV - 1):
                acc = acc + buf_ref[j, :, c0:c0 + LANE].T * cw[:, j:j + 1]
            xbc_s[c0:c0 + LANE, :] = acc * _sigmoid(acc)
            raw_out[:, c0:c0 + LANE] = raw_t.T

    dt = _pick_row(_softplus(proj_ref[GT:GT + LANE, :] + bias_ref[...]), GATE_DT + h)
    d_a = jnp.exp(dt * _pick_row(-jnp.exp(alog_ref[...]), GATE_DT + h))
    x_h = _rows(xbc_s, 64 * h, 64)
    b_g = _rows(xbc_s, SSD_W + 64 * g, 64)
    c_g = _rows(xbc_s, SSD_W + LANE + 64 * g, 64)
    u = dt * x_h
    for p in range(64):
        sn = d_a * s_ref[p] + u[p:p + 1, :] * b_g
        s_out[p] = sn
        yh_s[p:p + 1, :] = jnp.sum(sn * c_g, axis=0, keepdims=True)
    z = _rows(proj_ref, SZ + 64 * h, 64)
    ys_s[pl.ds(pl.multiple_of(64 * h, 64), 64), :] = (yh_s[...] + _rows(dskip_ref, 64 * h, 64) * x_h) * (z * _sigmoid(z))

    @pl.when(h % (SSD_HEADS // 2) == SSD_HEADS // 2 - 1)
    def _():
        grp = _rows(ys_s, 256 * g, 256)
        y_ref[...] = (grp * lax.rsqrt(jnp.mean(grp * grp, axis=0, keepdims=True) + EPS)
                      * _rows(ssdg_ref, 256 * g, 256))


def _step_ssd(proj_t, layer, buf_t, bias_col, alog_col, cw_t, dskip_col, ssdg_col, s_t):
    bsz = proj_t.shape[1]
    return pl.pallas_call(
        _step_ssd_kernel,
        grid=(SSD_HEADS,),
        in_specs=[_const_spec((NP, bsz)),
                  pl.BlockSpec((None, SSD_CONV - 1, bsz, SSD_CONV_DIM), lambda h: (layer, 0, 0, 0)),
                  _const_spec((LANE, 1)), _const_spec((LANE, 1)), _const_spec((SSD_CONV_DIM, SUBLANE)),
                  _const_spec((SSD_W, 1)), _const_spec((SSD_W, 1)),
                  pl.BlockSpec((None, None, 64, 64, bsz), lambda h: (layer, h, 0, 0, 0))],
        out_specs=[pl.BlockSpec((None, 64, 64, bsz), lambda h: (h, 0, 0, 0)),
                   pl.BlockSpec((256, bsz), lambda h: (h // (SSD_HEADS // 2), 0)),
                   _const_spec((bsz, SSD_CONV_DIM))],
        out_shape=[jax.ShapeDtypeStruct((SSD_HEADS, 64, 64, bsz), F32),
                   jax.ShapeDtypeStruct((SSD_W, bsz), F32),
                   jax.ShapeDtypeStruct((bsz, SSD_CONV_DIM), F32)],
        scratch_shapes=[pltpu.VMEM((SSD_CONV_DIM, bsz), F32), pltpu.VMEM((64, bsz), F32),
                        pltpu.VMEM((SSD_W, bsz), F32)],
        compiler_params=_cparams("arbitrary"),
        name="step_ssd",
    )(proj_t, buf_t, bias_col, alog_col, cw_t, dskip_col, ssdg_col, s_t)


def _step_gla_kernel(proj_ref, wf2t_ref, bf_ref, glag_ref, s_ref, s_out, y_ref, dec_s):
    h = pl.program_id(0)

    @pl.when(h == 0)
    def _():
        la = _log_sigmoid(_dot(wf2t_ref[...], proj_ref[GT:GT + LANE, :].astype(BF16)) + bf_ref[...]) / GLA_TAU
        dec_s[...] = jnp.exp(la)

    dec = _rows(dec_s, GLA_DK * h, GLA_DK)
    q = _rows(proj_ref, GQ + GLA_DK * h, GLA_DK) * (GLA_DK ** -0.5)
    k = _rows(proj_ref, GK + GLA_DK * h, GLA_DK)
    v = _rows(proj_ref, GV + 64 * h, 64)
    acc = jnp.zeros_like(v)
    for j in range(GLA_DK):
        sn = dec[j:j + 1, :] * s_ref[j] + k[j:j + 1, :] * v
        s_out[j] = sn
        acc = acc + q[j:j + 1, :] * sn
    gg = _rows(proj_ref, GG + 64 * h, 64)
    y_ref[...] = (acc * lax.rsqrt(jnp.mean(acc * acc, axis=0, keepdims=True) + EPS)
                  * _rows(glag_ref, 64 * h, 64) * (gg * _sigmoid(gg)))


def _step_gla(proj_t, layer, wf2_t, bf_col, glag_col, s_t):
    bsz = proj_t.shape[1]
    return pl.pallas_call(
        _step_gla_kernel,
        grid=(GLA_HEADS,),
        in_specs=[_const_spec((NP, bsz)), _const_spec((LANE, LANE)), _const_spec((LANE, 1)),
                  _const_spec((256, 1)),
                  pl.BlockSpec((None, None, GLA_DK, 64, bsz), lambda h: (layer, h, 0, 0, 0))],
        out_specs=[pl.BlockSpec((None, GLA_DK, 64, bsz), lambda h: (h, 0, 0, 0)),
                   pl.BlockSpec((64, bsz), lambda h: (h, 0))],
        out_shape=[jax.ShapeDtypeStruct((GLA_HEADS, GLA_DK, 64, bsz), F32),
                   jax.ShapeDtypeStruct((256, bsz), F32)],
        scratch_shapes=[pltpu.VMEM((LANE, bsz), F32)],
        compiler_params=_cparams("arbitrary"),
        name="step_gla",
    )(proj_t, wf2_t, bf_col, glag_col, s_t)


def _prep_w_in(w_in):
    parts, acc = [], 0
    for s in IN_SIZES:
        parts.append(w_in[..., acc:acc + s])
        acc += s
    mq, mk, mv, mo, mi, mf, sz, sxbc, sdt, gq, gk, gv, gg, gf = parts
    gates = jnp.concatenate([mi, mf, sdt, gf, jnp.zeros(w_in.shape[:-1] + (LANE - 32,), w_in.dtype)], axis=-1)
    return jnp.concatenate([mq, mk, mv, mo, sz, sxbc, gq, gk, gv, gg, gates], axis=-1).astype(BF16)


def _pad_row(x, width):
    return jnp.pad(x, [(0, 0), (0, width - x.shape[-1])])


def kernel(x_prompt, x_sample, state_mlstm_c, state_mlstm_n, state_mlstm_m, state_ssd, state_ssd_conv,
           state_gla, state_ffn_conv, norm1_g, w_in, mlstm_b_i, mlstm_b_f, mlstm_norm_g, ssd_conv_w,
           ssd_conv_b, ssd_dt_bias, ssd_a_log, ssd_d, ssd_norm_g, gla_w_f2, gla_b_f, gla_norm_g, w_out,
           norm2_g, w_up, ffn_conv_w, ffn_conv_b, w_down, final_norm_g):
    bsz, seq, _ = x_prompt.shape
    dbs = x_sample.shape[0]
    depth = w_in.shape[0]

    w_in_p = _prep_w_in(w_in)
    w_out_b = w_out.astype(BF16)
    w_up_p = w_up.astype(BF16)
    w_down_p = w_down.astype(BF16)
    ffn_cwb = jnp.concatenate([ffn_conv_w, ffn_conv_b[:, None, :],
                               jnp.zeros((depth, SUBLANE - FFN_CONV - 1, 2 * D_FF), F32)], axis=1)
    zeros8 = jnp.zeros((depth, 8), F32)
    bias_rows = _pad_row(jnp.concatenate([mlstm_b_i, mlstm_b_f, ssd_dt_bias], axis=-1), LANE)[:, None, :]
    alog_rows = _pad_row(jnp.concatenate([zeros8, ssd_a_log], axis=-1), LANE)[:, None, :]
    ssd_cwb = jnp.concatenate([ssd_conv_w, ssd_conv_b[:, None, :],
                               jnp.zeros((depth, SUBLANE - SSD_CONV - 1, SSD_CONV_DIM), F32)], axis=1)
    dskip_rows = jnp.repeat(ssd_d, 64, axis=-1)[:, None, :]
    wf2_p = jnp.pad(gla_w_f2, [(0, 0), (GATE_GF, LANE - GATE_GF - GLA_RANK), (0, 0)]).astype(BF16)

    w_in_t = w_in_p.transpose(0, 2, 1)
    c_t = state_mlstm_c.transpose(0, 2, 3, 4, 1)
    n_t = state_mlstm_n.transpose(0, 2, 3, 1)
    m_t = state_mlstm_m.transpose(0, 2, 1)
    s_t = state_ssd.transpose(0, 2, 3, 4, 1)
    g_t = state_gla.transpose(0, 2, 3, 4, 1)
    conv_t = state_ssd_conv.transpose(0, 2, 1, 3)

    xp = x_prompt.reshape(bsz * seq, D_MODEL)
    xs = x_sample.reshape(dbs, D_MODEL)
    fg = final_norm_g[None, :]
    p_states, s_states = [], []
    for i in range(depth):
        g1 = norm1_g[i][None, :]
        g2 = norm2_g[i][None, :]
        mlg = mlstm_norm_g[i][None, :]
        ssdg = ssd_norm_g[i][None, :]
        glag = gla_norm_g[i][None, :]
        bfr = gla_b_f[i][None, :]
        last = i == depth - 1

        proj = _norm_proj(xp, g1, w_in_p, i)
        mix, caug, m_o, ssd_o, gla_o = _mixer(proj, bsz, seq, bias_rows[i], alog_rows[i], mlg, ssd_cwb[i],
                                              dskip_rows[i], ssdg, wf2_p[i], bfr, glag)
        xp, ffn_st = _out_ffn_seq(xp, mix, bsz, seq, i, w_out_b, g2, w_up_p, ffn_cwb[i], w_down_p, fg, last)
        caug4 = caug.reshape(bsz, ML_HEADS, ML_DK, LANE)
        ssd4 = ssd_o.reshape(bsz, SSD_HEADS, 64, LANE)
        gla4 = gla_o.reshape(bsz, GLA_HEADS, GLA_DK, 256)
        p_states.append((
            caug4[..., :64], caug4[..., 64], m_o[:, 0, :ML_HEADS],
            jnp.where(jnp.arange(SSD_HEADS)[None, :, None, None] < 4, ssd4[..., :64], ssd4[..., 64:]),
            proj.reshape(bsz, seq, NP)[:, seq - (SSD_CONV - 1):, SXBC:SXBC + SSD_CONV_DIM],
            jnp.stack([gla4[:, h, :, 64 * h:64 * (h + 1)] for h in range(GLA_HEADS)], axis=1),
            ffn_st[:, SUBLANE - (FFN_CONV - 1):, :],
        ))

        proj_t = _norm_proj_t(xs, g1, w_in_t, i)
        c_n, n_n, m_n, y_ml = _step_mlstm(proj_t, i, bias_rows[i].T, mlg.T, c_t, n_t, m_t)
        s_n, y_ssd, raw_s = _step_ssd(proj_t, i, conv_t, bias_rows[i].T, alog_rows[i].T, ssd_cwb[i].T,
                                      dskip_rows[i].T, ssdg.T, s_t)
        g_n, y_gla = _step_gla(proj_t, i, wf2_p[i].T, bfr.T, glag.T, g_t)
        mix_t = jnp.concatenate([y_ml, y_ssd, y_gla], axis=0)
        prev = state_ffn_conv[i].transpose(1, 0, 2)
        xs, up_s = _out_ffn_step(xs, mix_t, i, w_out_b, g2, w_up_p, ffn_cwb[i], w_down_p, fg, last, prev)
        s_states.append((
            c_n, n_n, m_n, s_n,
            jnp.concatenate([state_ssd_conv[i][:, 1:], raw_s[:, None, :]], axis=1),
            g_n,
            jnp.concatenate([state_ffn_conv[i][:, 1:], up_s[:, None, :]], axis=1),
        ))

    y_prompt = xp.reshape(bsz, seq, D_MODEL)
    y_sample = xs.reshape(dbs, 1, D_MODEL)

    def stk(sts, j):
        return jnp.stack([s[j] for s in sts], axis=0)

    s_perm = {0: (0, 4, 1, 2, 3), 1: (0, 3, 1, 2), 2: (0, 2, 1), 3: (0, 4, 1, 2, 3), 5: (0, 4, 1, 2, 3)}
    s_out = tuple(stk(s_states, j).transpose(s_perm[j]) if j in s_perm else stk(s_states, j) for j in range(7))
    return (y_prompt, y_sample) + tuple(stk(p_states, j) for j in range(7)) + s_out
```

```python
import functools

import jax
import jax.numpy as jnp
from jax import lax
from jax.experimental import pallas as pl
from jax.experimental.pallas import tpu as pltpu

F32 = jnp.float32
BF16 = jnp.bfloat16
I32 = jnp.int32

D_MODEL = 1024
DEPTH = 4
ML_HEADS = 4
ML_DK = 64
SSD_HEADS = 8
SSD_W = 512
SSD_CONV = 4
SSD_CONV_DIM = 768
GLA_HEADS = 4
GLA_DK = 32
GLA_RANK = 16
GLA_TAU = 16.0
D_FF = 2816
FFN_CONV = 3
EPS = 1e-6
IN_SIZES = (256, 256, 256, 256, 4, 4, 512, 768, 8, 128, 128, 256, 256, 16)

MQ, MK, MV, MO, SZ, SXBC, GQ, GK, GV, GG, GT, NP = (
    0, 256, 512, 768, 1024, 1536, 2304, 2432, 2560, 2816, 3072, 3200)
GATE_LI, GATE_LF, GATE_DT, GATE_GF = 0, 4, 8, 16

LANE = 128
SUBLANE = 8
MIX_T = 128
MIX_ROWS = 2
GLA_C = 8
FFN_TM = 512
FFN_ROWS = 2
FF_CHUNK = 256
N_FF_CHUNK = D_FF // FF_CHUNK
VMEM_LIMIT = 56 * 1024 * 1024


def _cparams(*sem):
    return pltpu.CompilerParams(dimension_semantics=sem if sem else None,
                                vmem_limit_bytes=VMEM_LIMIT)


def _const_spec(shape):
    nd = len(shape)
    return pl.BlockSpec(shape, lambda *_: (0,) * nd)


def _layer_spec(shape, layer, resident=False):
    nd = len(shape)
    return pl.BlockSpec((None,) + tuple(shape), lambda *_: (layer,) + (0,) * nd,
                        pipeline_mode=pl.Buffered(1) if resident else None)


def _sigmoid(x):
    return 1.0 / (1.0 + jnp.exp(-x))


def _silu(x):
    h = 0.5 * x
    return h + h * jnp.tanh(h)


def _softplus(x):
    return jnp.maximum(x, 0.0) + jnp.log(1.0 + jnp.exp(-jnp.abs(x)))


def _log_sigmoid(x):
    return -_softplus(-x)


def _rmsnorm_rows(x, g):
    ms = jnp.mean(x * x, axis=-1, keepdims=True)
    return x * lax.rsqrt(ms + EPS) * g


def _dot(a, b):
    return jnp.dot(a, b, preferred_element_type=F32)


def _dot_nt(a, b):
    return lax.dot_general(a, b, (((1,), (1,)), ((), ())), preferred_element_type=F32)


def _split3(x):
    x1 = x.astype(BF16)
    r1 = x - x1.astype(F32)
    x2 = r1.astype(BF16)
    x3 = (r1 - x2.astype(F32)).astype(BF16)
    return x1, x2, x3


def _cumsum_rows(x, tri):
    c = _dot(tri, jnp.concatenate(_split3(x), axis=1))
    return c[:, :LANE] + c[:, LANE:2 * LANE] + c[:, 2 * LANE:]


def _shift_rows(x, k, carry):
    r = pltpu.roll(x, k, 0)
    c = pltpu.roll(carry, k, 0)
    row = lax.broadcasted_iota(I32, c.shape, 0)
    head = jnp.where(row < k, c, r[:SUBLANE])
    return jnp.concatenate([head, r[SUBLANE:]], axis=0)


def _seg_rmsnorm(x, seg):
    w = x.shape[1]
    if seg % LANE == 0:
        parts = []
        for s0 in range(0, w, seg):
            xs = x[:, s0:s0 + seg]
            parts.append(xs * lax.rsqrt(jnp.mean(xs * xs, axis=-1, keepdims=True) + EPS))
        return jnp.concatenate(parts, axis=1)
    parts = []
    lane = lax.broadcasted_iota(I32, (x.shape[0], LANE), 1)
    for s0 in range(0, w, LANE):
        xs = x[:, s0:s0 + LANE]
        sq = xs * xs
        lo = jnp.sum(jnp.where(lane < seg, sq, 0.0), axis=-1, keepdims=True)
        hi = jnp.sum(jnp.where(lane >= seg, sq, 0.0), axis=-1, keepdims=True)
        r = jnp.where(lane < seg, lax.rsqrt(lo / seg + EPS), lax.rsqrt(hi / seg + EPS))
        parts.append(xs * r)
    return jnp.concatenate(parts, axis=1)


def _drive(chains):
    chains = list(chains)
    while chains:
        for c in list(chains):
            try:
                next(c)
            except StopIteration:
                chains.remove(c)


def _norm_proj_kernel(x_ref, g_ref, w_ref, o_ref):
    h = _rmsnorm_rows(x_ref[...], g_ref[...]).astype(BF16)
    for n0 in range(0, NP, 512):
        n1 = min(n0 + 512, NP)
        o_ref[:, n0:n1] = _dot(h, w_ref[:, n0:n1])


def _norm_proj(x2d, g, w_all, layer):
    m = x2d.shape[0]
    tm = min(512, m)
    return pl.pallas_call(
        _norm_proj_kernel,
        grid=(m // tm,),
        in_specs=[pl.BlockSpec((tm, D_MODEL), lambda i: (i, 0)),
                  _const_spec((1, D_MODEL)),
                  _layer_spec((D_MODEL, NP), layer)],
        out_specs=pl.BlockSpec((tm, NP), lambda i: (i, 0)),
        out_shape=jax.ShapeDtypeStruct((m, NP), F32),
        compiler_params=_cparams("arbitrary"),
        name="norm_proj",
    )(x2d, g, w_all)


def _mixer_kernel(proj_ref, bias_ref, alog_ref, mlg_ref, cw_ref, dskip_ref, ssdg_ref, wf2_ref,
                  bf_ref, glag_ref,
                  mix_ref, caug_ref, m_ref, ssd_ref, gla_ref,
                  conv_s, cum_s):
    @pl.when(pl.program_id(1) == 0)
    def _():
        caug_ref[...] = jnp.zeros_like(caug_ref)
        m_ref[...] = jnp.zeros_like(m_ref)
        ssd_ref[...] = jnp.zeros_like(ssd_ref)
        gla_ref[...] = jnp.zeros_like(gla_ref)
        conv_s[...] = jnp.zeros_like(conv_s)

    rows = [_MixerRow(proj_ref.at[r], bias_ref, alog_ref, mlg_ref, cw_ref, dskip_ref, ssdg_ref, wf2_ref,
                      bf_ref, glag_ref, mix_ref.at[r], caug_ref.at[r], m_ref.at[r], ssd_ref.at[r],
                      gla_ref.at[r], conv_s.at[r], cum_s.at[r]) for r in range(proj_ref.shape[0])]
    _drive([row.setup() for row in rows])
    _drive([c for row in rows for c in row.chains()])
    for row in rows:
        row.finish()


class _MixerRow:
    def __init__(self, proj_ref, bias_ref, alog_ref, mlg_ref, cw_ref, dskip_ref, ssdg_ref, wf2_ref,
                 bf_ref, glag_ref, mix_ref, caug_ref, m_ref, ssd_ref, gla_ref, conv_s, cum_s):
        self.proj_ref, self.bias_ref, self.alog_ref, self.mlg_ref = proj_ref, bias_ref, alog_ref, mlg_ref
        self.cw_ref, self.dskip_ref, self.ssdg_ref, self.wf2_ref = cw_ref, dskip_ref, ssdg_ref, wf2_ref
        self.bf_ref, self.glag_ref, self.mix_ref, self.caug_ref = bf_ref, glag_ref, mix_ref, caug_ref
        self.m_ref, self.ssd_ref, self.gla_ref, self.conv_s, self.cum_s = m_ref, ssd_ref, gla_ref, conv_s, cum_s
        self.out = {}
        self.levels = {}

    def setup(self):
        T = MIX_T
        proj_ref = self.proj_ref
        self.lane = lane = lax.broadcasted_iota(I32, (T, LANE), 1)
        self.rowi = lax.broadcasted_iota(I32, (T, T), 0)
        self.coli = lax.broadcasted_iota(I32, (T, T), 1)
        self.causal = self.rowi >= self.coli
        tri = jnp.where(self.causal, 1.0, 0.0).astype(BF16)
        self.row128 = lax.broadcasted_iota(I32, (LANE, LANE), 0)
        self.g_blk = g_blk = proj_ref[:, GT:GT + LANE] + self.bias_ref[...]
        is_lf = (lane >= GATE_LF) & (lane < GATE_DT)
        is_dt = (lane >= GATE_DT) & (lane < GATE_GF)
        self.dt = dt = _softplus(jnp.where(is_lf, -g_blk, g_blk))
        a_row = -jnp.exp(self.alog_ref[...])
        self.cum_g = _cumsum_rows(jnp.where(is_lf, -dt, jnp.where(is_dt, dt * a_row, 0.0)), tri)
        yield
        self.p1_t = jnp.where(lane < GATE_LF, g_blk, self.cum_g).T
        self.dt_t = dt.T
        self.k_ml_t = [proj_ref[:, MK + LANE * p:MK + LANE * (p + 1)].T for p in range(2)]
        raw = proj_ref[:, SXBC:SXBC + SSD_CONV_DIM]
        carry = self.conv_s[...]
        cw_ref = self.cw_ref
        conv = (cw_ref[4:5, :] + _shift_rows(raw, 3, carry) * cw_ref[0:1, :]
                + _shift_rows(raw, 2, carry) * cw_ref[1:2, :]
                + _shift_rows(raw, 1, carry) * cw_ref[2:3, :] + raw * cw_ref[3:4, :])
        self.conv_s[...] = raw[T - SUBLANE:T, :]
        self.xbc = xbc = _silu(conv)
        self.x_ssd_t = [xbc[:, LANE * j:LANE * (j + 1)].T for j in range(SSD_HEADS // 2)]
        self.b_p = b_p = xbc[:, SSD_W:SSD_W + LANE]
        c_p = xbc[:, SSD_W + LANE:SSD_W + 2 * LANE]
        self.c_pb = c_p.astype(BF16)
        self.cb = []
        for g in range(2):
            in_g = (lane >= 64 * g) & (lane < 64 * (g + 1))
            self.cb.append(_dot_nt(jnp.where(in_g, c_p, 0.0).astype(BF16), b_p.astype(BF16)))
            yield
        la = _log_sigmoid(_dot(g_blk.astype(BF16), self.wf2_ref[...]) + self.bf_ref[...]) / GLA_TAU
        yield
        self.cum = cum = _cumsum_rows(la, tri)
        yield
        self.cum_s[...] = cum
        self.cum_t = cum.T
        self.q = proj_ref[:, GQ:GQ + LANE] * (GLA_DK ** -0.5)
        self.k = proj_ref[:, GK:GK + LANE]
        self.k_gla_t = self.k.T
        self.v = proj_ref[:, GV:GV + 256]
        r_s = lax.broadcasted_iota(I32, (LANE, 256), 0)
        c_s = lax.broadcasted_iota(I32, (LANE, 256), 1)
        self.same_head = (r_s >> 5) == (c_s >> 6)
        self.tpos = lax.broadcasted_iota(I32, (T, LANE), 0)

    def chains(self):
        return ([self.ml_head(h) for h in range(ML_HEADS)] + [self.ssd_pair(j) for j in range(SSD_HEADS // 2)]
                + [self.gla_state(), self.gla_diag()] + [self.gla_head(h) for h in range(GLA_HEADS)])

    def ml_head(self, h):
        T = MIX_T
        proj_ref, lane, cum_g, p1_t = self.proj_ref, self.lane, self.cum_g, self.p1_t
        pair, lo = h // 2, 64 * (h % 2)
        hm = (lane >= lo) & (lane < lo + 64)
        qm = jnp.where(hm, proj_ref[:, MQ + LANE * pair:MQ + LANE * (pair + 1)] * 0.125, 0.0).astype(BF16)
        km = jnp.where(hm, proj_ref[:, MK + LANE * pair:MK + LANE * (pair + 1)], 0.0)
        qk = _dot_nt(qm, km.astype(BF16))
        yield
        c_pair = self.caug_ref[LANE * pair:LANE * (pair + 1), :]
        qc = _dot(qm, c_pair.astype(BF16))
        yield
        v_pair = proj_ref[:, MV + LANE * pair:MV + LANE * (pair + 1)]
        v_lo = pltpu.roll(v_pair, 64, 1) if h % 2 else v_pair
        vaug = jnp.where(lane < 64, v_lo, jnp.where(lane == 64, 1.0, 0.0)).astype(BF16)
        b_col = cum_g[:, GATE_LF + h:GATE_LF + h + 1]
        b_row = p1_t[GATE_LF + h:GATE_LF + h + 1, :]
        li_row = p1_t[GATE_LI + h:GATE_LI + h + 1, :]
        m_prev = self.m_ref[:, h:h + 1]
        d_mat = jnp.where(self.causal, b_col - b_row + li_row, -jnp.inf)
        yield
        row_max = jnp.max(d_mat, axis=1, keepdims=True)
        yield
        a_col = b_col + m_prev
        mt = jnp.maximum(a_col, row_max)
        s_mat = (qk * jnp.exp(d_mat - mt)).astype(BF16)
        numaug = _dot(s_mat, vaug) + jnp.exp(a_col - mt) * qc
        yield
        den = numaug[:, 64:65]
        hn = numaug / jnp.maximum(jnp.abs(den), jnp.exp(-mt))
        hv = jnp.where(lane < 64, hn, 0.0)
        yield
        self.out["ml", h] = hv * lax.rsqrt(jnp.sum(hv * hv, axis=-1, keepdims=True) / 64.0 + EPS)
        b_last = b_col[T - 1:T, :]
        m_new = mt[T - 1:T, :]
        g_state = jnp.exp(b_last + m_prev - m_new)
        g_s_row = jnp.exp(b_last - b_row + li_row - m_new)
        u = _dot((self.k_ml_t[pair][lo:lo + 64, :] * g_s_row).astype(BF16), vaug)
        yield
        self.caug_ref[LANE * pair + lo:LANE * pair + lo + 64, :] = g_state * c_pair[lo:lo + 64] + u
        self.out["m", h] = m_new

    def ssd_pair(self, j):
        T = MIX_T
        lane, cum_g, p1_t = self.lane, self.cum_g, self.p1_t
        g = j // 2
        x_pair = self.xbc[:, LANE * j:LANE * (j + 1)]
        s_pair = self.ssd_ref[LANE * j:LANE * (j + 1), :]
        inter = _dot_nt(self.c_pb, s_pair.astype(BF16))
        yield
        y_pair = None
        ecs, wrow, dec = [], [], []
        for hh in range(2):
            h = 2 * j + hh
            cs_col = cum_g[:, GATE_DT + h:GATE_DT + h + 1]
            cs_row = p1_t[GATE_DT + h:GATE_DT + h + 1, :]
            dt_row = self.dt_t[GATE_DT + h:GATE_DT + h + 1, :]
            seg = jnp.where(self.causal, cs_col - cs_row, -jnp.inf)
            yield
            sc = (self.cb[g] * jnp.exp(seg) * dt_row).astype(BF16)
            in_h = (lane >= 64 * hh) & (lane < 64 * (hh + 1))
            part = _dot(sc, jnp.where(in_h, x_pair, 0.0).astype(BF16))
            yield
            y_pair = part if y_pair is None else y_pair + part
            cs_last = cs_col[T - 1:T, :]
            ecs.append(jnp.exp(cs_col))
            wrow.append(jnp.exp(cs_last - cs_row) * dt_row)
            dec.append(jnp.exp(cs_last))
        self.out["ssd", j] = y_pair + jnp.where(lane < 64, ecs[0], ecs[1]) * inter
        wx_t = (self.x_ssd_t[j] * jnp.where(self.row128 < 64, wrow[0], wrow[1])).astype(BF16)
        in_g = (lane >= 64 * g) & (lane < 64 * (g + 1))
        u = _dot(wx_t, jnp.where(in_g, self.b_p, 0.0).astype(BF16))
        yield
        self.ssd_ref[LANE * j:LANE * (j + 1), :] = jnp.where(self.row128 < 64, dec[0], dec[1]) * s_pair + u

    def gla_state(self):
        T = MIX_T
        cum = self.cum
        s_gla = self.gla_ref[...]
        self.out["gla_inter"] = _dot((self.q * jnp.exp(cum)).astype(BF16), s_gla.astype(BF16))
        yield
        cum_t = self.cum_t
        u = _dot((self.k_gla_t * jnp.exp(cum_t[:, T - 1:T] - cum_t)).astype(BF16), self.v.astype(BF16))
        yield
        self.gla_ref[...] = jnp.exp(self.cum_t[:, T - 1:T]) * s_gla + jnp.where(self.same_head, u, 0.0)

    def gla_level(self, w):
        if w not in self.levels:
            T = MIX_T
            is_t = (self.tpos & (2 * w - 1)) >= w
            mid = jnp.concatenate(
                [jnp.broadcast_to(self.cum_s[2 * w * blk + w - 1:2 * w * blk + w, :], (2 * w, LANE))
                 for blk in range(T // (2 * w))], axis=0)
            e = jnp.exp(jnp.where(is_t, self.cum - mid, mid - self.cum))
            ql = jnp.where(is_t, self.q * e, 0.0)
            kl = jnp.where(is_t, 0.0, self.k * e).astype(BF16)
            shift = (2 * w).bit_length() - 1
            self.levels[w] = (ql, kl, (self.rowi >> shift) == (self.coli >> shift))
        return self.levels[w]

    def gla_head(self, h):
        lane = self.lane
        in_h = (lane >= GLA_DK * h) & (lane < GLA_DK * (h + 1))
        att = None
        w = GLA_C
        while w < MIX_T:
            ql, kl, same_blk = self.gla_level(w)
            a = jnp.where(same_blk, _dot_nt(jnp.where(in_h, ql, 0.0).astype(BF16), kl), 0.0)
            yield
            att = a if att is None else att + a
            w *= 2
        v_pair = self.v[:, LANE * (h // 2):LANE * (h // 2 + 1)]
        in_half = (lane >= 64) if h % 2 else (lane < 64)
        self.out["gla_o", h] = _dot(att.astype(BF16), jnp.where(in_half, v_pair, 0.0).astype(BF16))
        yield

    def gla_diag(self):
        expand = jnp.where(self.same_head, 1.0, 0.0).astype(BF16)
        nblk = MIX_T // GLA_C
        o = None
        for jj in range(GLA_C):
            def rows(ref, c0, c1):
                return jnp.concatenate(
                    [jnp.broadcast_to(ref[GLA_C * i + jj:GLA_C * i + jj + 1, c0:c1], (GLA_C, c1 - c0))
                     for i in range(nblk)], axis=0)
            k_s = rows(self.proj_ref, GK, GK + LANE)
            c_srow = rows(self.cum_s, 0, LANE)
            v_s = rows(self.proj_ref, GV, GV + 256)
            valid = (self.tpos & (GLA_C - 1)) >= jj
            e = jnp.exp(jnp.where(valid, self.cum - c_srow, -jnp.inf))
            part = _dot((self.q * k_s * e).astype(BF16), expand) * v_s
            yield
            o = part if o is None else o + part
        self.out["gla_diag"] = o

    def finish(self):
        proj_ref, mix_ref, lane, out = self.proj_ref, self.mix_ref, self.lane, self.out
        m_row = self.m_ref[...]
        m_lane = lax.broadcasted_iota(I32, m_row.shape, 1)
        for h in range(ML_HEADS):
            m_row = jnp.where(m_lane == h, out["m", h], m_row)
        self.m_ref[...] = m_row
        y_ml = jnp.concatenate(
            [jnp.where(lane < 64, out["ml", 2 * p], pltpu.roll(out["ml", 2 * p + 1], 64, 1)) for p in range(2)],
            axis=1)
        mix_ref[:, 0:256] = (_sigmoid(proj_ref[:, MO:MO + 256]) * (y_ml * self.mlg_ref[...])).astype(BF16)
        y_s = (jnp.concatenate([out["ssd", j] for j in range(SSD_HEADS // 2)], axis=1)
               + self.dskip_ref[...] * self.xbc[:, :SSD_W])
        z = proj_ref[:, SZ:SZ + SSD_W]
        mix_ref[:, 256:768] = (_seg_rmsnorm(y_s * _silu(z), 256) * self.ssdg_ref[...]).astype(BF16)
        o = (out["gla_inter"] + out["gla_diag"]
             + jnp.concatenate([out["gla_o", 2 * p] + out["gla_o", 2 * p + 1] for p in range(2)], axis=1))
        gg = proj_ref[:, GG:GG + 256]
        mix_ref[:, 768:1024] = (_seg_rmsnorm(o, 64) * self.glag_ref[...] * _silu(gg)).astype(BF16)


def _mixer(proj, bsz, seq, bias_row, alog_row, mlg, cwb, dskip, ssdg, wf2p, bfr, glag):
    nt = seq // MIX_T
    rows = MIX_ROWS if bsz % MIX_ROWS == 0 else 1
    row_spec = lambda w: _const_spec((1, w))
    mix, caug, m_o, ssd_o, gla_o = pl.pallas_call(
        _mixer_kernel,
        grid=(bsz // rows, nt),
        in_specs=[pl.BlockSpec((rows, MIX_T, NP), lambda b, t: (b, t, 0)),
                  row_spec(LANE), row_spec(LANE), row_spec(256),
                  _const_spec((SUBLANE, SSD_CONV_DIM)), row_spec(SSD_W), row_spec(SSD_W),
                  _const_spec((LANE, LANE)), row_spec(LANE), row_spec(256)],
        out_specs=[pl.BlockSpec((rows, MIX_T, D_MODEL), lambda b, t: (b, t, 0)),
                   pl.BlockSpec((rows, 256, LANE), lambda b, t: (b, 0, 0)),
                   pl.BlockSpec((rows, 1, LANE), lambda b, t: (b, 0, 0)),
                   pl.BlockSpec((rows, 512, LANE), lambda b, t: (b, 0, 0)),
                   pl.BlockSpec((rows, LANE, 256), lambda b, t: (b, 0, 0))],
        out_shape=[jax.ShapeDtypeStruct((bsz, seq, D_MODEL), BF16),
                   jax.ShapeDtypeStruct((bsz, 256, LANE), F32),
                   jax.ShapeDtypeStruct((bsz, 1, LANE), F32),
                   jax.ShapeDtypeStruct((bsz, 512, LANE), F32),
                   jax.ShapeDtypeStruct((bsz, LANE, 256), F32)],
        scratch_shapes=[pltpu.VMEM((rows, SUBLANE, SSD_CONV_DIM), F32), pltpu.VMEM((rows, MIX_T, LANE), F32)],
        compiler_params=_cparams("arbitrary", "arbitrary"),
        name="mixer",
    )(proj.reshape(bsz, seq, NP), bias_row, alog_row, mlg, cwb, dskip, ssdg, wf2p, bfr, glag)
    return mix.reshape(bsz * seq, D_MODEL), caug, m_o, ssd_o, gla_o


def _ffn_row(x_ref, mix_fn, wo_ref, g2_ref, wup_ref, cwb_ref, wd_ref, fg_ref, o_ref, h2_s, act_s,
             prev_fn, keep_fn, final):
    xn = x_ref[...] + _dot(mix_fn(), wo_ref[...])
    yield
    o_ref[...] = xn
    h2_s[...] = _rmsnorm_rows(xn, g2_ref[...]).astype(BF16)
    for c in range(N_FF_CHUNK):
        h2 = h2_s[...]
        halves = []
        for ug in range(2):
            col = slice(ug * D_FF + c * FF_CHUNK, ug * D_FF + (c + 1) * FF_CHUNK)
            up = _dot(h2, wup_ref[:, col])
            yield
            x1, x2 = prev_fn(col, up)
            keep_fn(col, up)
            halves.append(cwb_ref[3:4, col] + x2 * cwb_ref[0:1, col] + x1 * cwb_ref[1:2, col]
                          + up * cwb_ref[2:3, col])
        u, gate = halves
        act_s[:, c * FF_CHUNK:(c + 1) * FF_CHUNK] = (_silu(gate) * u).astype(BF16)
    y = o_ref[...] + _dot(act_s[...], wd_ref[...])
    yield
    o_ref[...] = _rmsnorm_rows(y, fg_ref[...]) if final else y


def _out_ffn_seq_kernel(x_ref, mix_ref, wo_ref, g2_ref, wup_ref, cwb_ref, wd_ref, fg_ref, o_ref, st_ref,
                        h2_s, act_s, *, final):
    @pl.when(pl.program_id(1) == 0)
    def _():
        st_ref[...] = jnp.zeros_like(st_ref)

    tm = x_ref.shape[1]

    def chain(r):
        st = st_ref.at[r]

        def prev_fn(col, up):
            carry = st[:, col]
            return _shift_rows(up, 1, carry), _shift_rows(up, 2, carry)

        def keep_fn(col, up):
            st[:, col] = up[tm - SUBLANE:tm, :]

        return _ffn_row(x_ref.at[r], lambda: mix_ref[r], wo_ref, g2_ref, wup_ref, cwb_ref, wd_ref, fg_ref,
                        o_ref.at[r], h2_s.at[r], act_s.at[r], prev_fn, keep_fn, final)

    _drive([chain(r) for r in range(x_ref.shape[0])])


def _out_ffn_step_kernel(x_ref, mix_ref, wo_ref, g2_ref, wup_ref, cwb_ref, wd_ref, fg_ref, prev_ref, o_ref, up_ref,
                         h2_s, act_s, *, final):
    def mix_fn():
        mix = jnp.concatenate([mix_ref[c0:c0 + LANE, :].T for c0 in range(0, D_MODEL, LANE)], axis=1)
        return mix.astype(BF16)

    def prev_fn(col, up):
        return prev_ref[1, :, col], prev_ref[0, :, col]

    def keep_fn(col, up):
        up_ref[:, col] = up

    _drive([_ffn_row(x_ref, mix_fn, wo_ref, g2_ref, wup_ref, cwb_ref, wd_ref, fg_ref, o_ref, h2_s, act_s,
                     prev_fn, keep_fn, final)])


def _ffn_w_specs(layer):
    return [_layer_spec((D_MODEL, D_MODEL), layer, resident=True), _const_spec((1, D_MODEL)),
            _layer_spec((D_MODEL, 2 * D_FF), layer, resident=True),
            _const_spec((SUBLANE, 2 * D_FF)),
            _layer_spec((D_FF, D_MODEL), layer, resident=True), _const_spec((1, D_MODEL))]


def _out_ffn_seq(x2d, mix, bsz, seq, layer, wo, g2, wup, cwb, wd, fg, final):
    tm = min(FFN_TM, seq)
    nt = seq // tm
    rows = FFN_ROWS if bsz % FFN_ROWS == 0 else 1
    tile = pl.BlockSpec((rows, tm, D_MODEL), lambda b, t: (b, t, 0))
    out, st = pl.pallas_call(
        functools.partial(_out_ffn_seq_kernel, final=final),
        grid=(bsz // rows, nt),
        in_specs=[tile, tile] + _ffn_w_specs(layer),
        out_specs=[tile, pl.BlockSpec((rows, SUBLANE, 2 * D_FF), lambda b, t: (b, 0, 0))],
        out_shape=[jax.ShapeDtypeStruct((bsz, seq, D_MODEL), F32),
                   jax.ShapeDtypeStruct((bsz, SUBLANE, 2 * D_FF), F32)],
        scratch_shapes=[pltpu.VMEM((rows, tm, D_MODEL), BF16), pltpu.VMEM((rows, tm, D_FF), BF16)],
        compiler_params=_cparams("arbitrary", "arbitrary"),
        name="out_ffn_seq",
    )(x2d.reshape(bsz, seq, D_MODEL), mix.reshape(bsz, seq, D_MODEL), wo, g2, wup, cwb, wd, fg)
    return out.reshape(bsz * seq, D_MODEL), st


def _out_ffn_step(x2d, mix, layer, wo, g2, wup, cwb, wd, fg, final, prev):
    m = x2d.shape[0]
    return pl.pallas_call(
        functools.partial(_out_ffn_step_kernel, final=final),
        grid=(1,),
        in_specs=[_const_spec((m, D_MODEL)), _const_spec((D_MODEL, m))] + _ffn_w_specs(layer)
        + [_const_spec((FFN_CONV - 1, m, 2 * D_FF))],
        out_specs=[_const_spec((m, D_MODEL)), _const_spec((m, 2 * D_FF))],
        out_shape=[jax.ShapeDtypeStruct((m, D_MODEL), F32),
                   jax.ShapeDtypeStruct((m, 2 * D_FF), F32)],
        scratch_shapes=[pltpu.VMEM((m, D_MODEL), BF16), pltpu.VMEM((m, D_FF), BF16)],
        compiler_params=_cparams("arbitrary"),
        name="out_ffn_step",
    )(x2d, mix, wo, g2, wup, cwb, wd, fg, prev)


def _norm_proj_t_kernel(x_ref, g_ref, w_ref, o_ref):
    h = _rmsnorm_rows(x_ref[...], g_ref[...]).astype(BF16)
    for n0 in range(0, NP, 512):
        n1 = min(n0 + 512, NP)
        blk = _dot(h, w_ref[:, n0:n1])
        for c0 in range(0, n1 - n0, LANE):
            o_ref[n0 + c0:n0 + c0 + LANE, :] = blk[:, c0:c0 + LANE].T


def _norm_proj_t(x2d, g, w_all, layer):
    m = x2d.shape[0]
    return pl.pallas_call(
        _norm_proj_t_kernel,
        grid=(1,),
        in_specs=[_const_spec((m, D_MODEL)), _const_spec((1, D_MODEL)), _layer_spec((D_MODEL, NP), layer)],
        out_specs=_const_spec((NP, m)),
        out_shape=jax.ShapeDtypeStruct((NP, m), F32),
        compiler_params=_cparams("arbitrary"),
        name="norm_proj_t",
    )(x2d, g, w_all)


def _pick_row(x, r):
    row = lax.broadcasted_iota(I32, x.shape, 0)
    return jnp.sum(jnp.where(row == r, x, 0.0), axis=0, keepdims=True)


def _rows(ref, start, size):
    return ref[pl.ds(pl.multiple_of(start, size), size), :]


def _step_mlstm_kernel(proj_ref, bias_ref, mlg_ref, c_ref, n_ref, m_ref, c_out, n_out, m_out, y_ref):
    h = pl.program_id(0)
    gates = proj_ref[GT:GT + LANE, :] + bias_ref[...]
    li = _pick_row(gates, GATE_LI + h)
    lf = _log_sigmoid(_pick_row(gates, GATE_LF + h))
    m0 = m_ref[pl.ds(h, 1), :]
    mt = jnp.maximum(lf + m0, li)
    w_old = jnp.exp(lf + m0 - mt)
    q = _rows(proj_ref, MQ + ML_DK * h, ML_DK) * 0.125
    kw = _rows(proj_ref, MK + ML_DK * h, ML_DK) * jnp.exp(li - mt)
    v = _rows(proj_ref, MV + 64 * h, 64)
    acc = jnp.zeros_like(v)
    for d in range(ML_DK):
        cn = w_old * c_ref[d] + kw[d:d + 1, :] * v
        c_out[d] = cn
        acc = acc + q[d:d + 1, :] * cn
    nn = w_old * n_ref[...] + kw
    n_out[...] = nn
    den = jnp.sum(q * nn, axis=0, keepdims=True)
    hn = acc / jnp.maximum(jnp.abs(den), jnp.exp(-mt))
    yn = hn * lax.rsqrt(jnp.mean(hn * hn, axis=0, keepdims=True) + EPS)
    y_ref[...] = _sigmoid(_rows(proj_ref, MO + 64 * h, 64)) * (yn * _rows(mlg_ref, 64 * h, 64))
    m_out[pl.ds(h, 1), :] = mt


def _step_mlstm(proj_t, layer, bias_col, mlg_col, c_t, n_t, m_t):
    bsz = proj_t.shape[1]
    return pl.pallas_call(
        _step_mlstm_kernel,
        grid=(ML_HEADS,),
        in_specs=[_const_spec((NP, bsz)), _const_spec((LANE, 1)), _const_spec((256, 1)),
                  pl.BlockSpec((None, None, ML_DK, 64, bsz), lambda h: (layer, h, 0, 0, 0)),
                  pl.BlockSpec((None, None, ML_DK, bsz), lambda h: (layer, h, 0, 0)),
                  pl.BlockSpec((None, ML_HEADS, bsz), lambda h: (layer, 0, 0))],
        out_specs=[pl.BlockSpec((None, ML_DK, 64, bsz), lambda h: (h, 0, 0, 0)),
                   pl.BlockSpec((None, ML_DK, bsz), lambda h: (h, 0, 0)),
                   _const_spec((ML_HEADS, bsz)),
                   pl.BlockSpec((64, bsz), lambda h: (h, 0))],
        out_shape=[jax.ShapeDtypeStruct((ML_HEADS, ML_DK, 64, bsz), F32),
                   jax.ShapeDtypeStruct((ML_HEADS, ML_DK, bsz), F32),
                   jax.ShapeDtypeStruct((ML_HEADS, bsz), F32),
                   jax.ShapeDtypeStruct((256, bsz), F32)],
        compiler_params=_cparams("arbitrary"),
        name="step_mlstm",
    )(proj_t, bias_col, mlg_col, c_t, n_t, m_t)


def _step_ssd_kernel(proj_ref, buf_ref, bias_ref, alog_ref, cw_ref, dskip_ref, ssdg_ref, s_ref,
                     s_out, y_ref, raw_out, xbc_s, yh_s, ys_s):
    h = pl.program_id(0)
    g = h // (SSD_HEADS // 2)

    @pl.when(h == 0)
    def _():
        for c0 in range(0, SSD_CONV_DIM, LANE):
            raw_t = proj_ref[SXBC + c0:SXBC + c0 + LANE, :]
            cw = cw_ref[c0:c0 + LANE, :]
            acc = cw[:, 4:5] + raw_t * cw[:, 3:4]
            for j in range(SSD_CONV - 1):
                acc = acc + buf_ref[j, :, c0:c0 + LANE].T * cw[:, j:j + 1]
            xbc_s[c0:c0 + LANE, :] = _silu(acc)
            raw_out[:, c0:c0 + LANE] = raw_t.T

    dt = _pick_row(_softplus(proj_ref[GT:GT + LANE, :] + bias_ref[...]), GATE_DT + h)
    d_a = jnp.exp(dt * _pick_row(-jnp.exp(alog_ref[...]), GATE_DT + h))
    x_h = _rows(xbc_s, 64 * h, 64)
    b_g = _rows(xbc_s, SSD_W + 64 * g, 64)
    c_g = _rows(xbc_s, SSD_W + LANE + 64 * g, 64)
    u = dt * x_h
    for p in range(64):
        sn = d_a * s_ref[p] + u[p:p + 1, :] * b_g
        s_out[p] = sn
        yh_s[p:p + 1, :] = jnp.sum(sn * c_g, axis=0, keepdims=True)
    z = _rows(proj_ref, SZ + 64 * h, 64)
    ys_s[pl.ds(pl.multiple_of(64 * h, 64), 64), :] = (yh_s[...] + _rows(dskip_ref, 64 * h, 64) * x_h) * _silu(z)

    @pl.when(h % (SSD_HEADS // 2) == SSD_HEADS // 2 - 1)
    def _():
        grp = _rows(ys_s, 256 * g, 256)
        y_ref[...] = (grp * lax.rsqrt(jnp.mean(grp * grp, axis=0, keepdims=True) + EPS)
                      * _rows(ssdg_ref, 256 * g, 256))


def _step_ssd(proj_t, layer, buf_t, bias_col, alog_col, cw_t, dskip_col, ssdg_col, s_t):
    bsz = proj_t.shape[1]
    return pl.pallas_call(
        _step_ssd_kernel,
        grid=(SSD_HEADS,),
        in_specs=[_const_spec((NP, bsz)),
                  pl.BlockSpec((None, SSD_CONV - 1, bsz, SSD_CONV_DIM), lambda h: (layer, 0, 0, 0)),
                  _const_spec((LANE, 1)), _const_spec((LANE, 1)), _const_spec((SSD_CONV_DIM, SUBLANE)),
                  _const_spec((SSD_W, 1)), _const_spec((SSD_W, 1)),
                  pl.BlockSpec((None, None, 64, 64, bsz), lambda h: (layer, h, 0, 0, 0))],
        out_specs=[pl.BlockSpec((None, 64, 64, bsz), lambda h: (h, 0, 0, 0)),
                   pl.BlockSpec((256, bsz), lambda h: (h // (SSD_HEADS // 2), 0)),
                   _const_spec((bsz, SSD_CONV_DIM))],
        out_shape=[jax.ShapeDtypeStruct((SSD_HEADS, 64, 64, bsz), F32),
                   jax.ShapeDtypeStruct((SSD_W, bsz), F32),
                   jax.ShapeDtypeStruct((bsz, SSD_CONV_DIM), F32)],
        scratch_shapes=[pltpu.VMEM((SSD_CONV_DIM, bsz), F32), pltpu.VMEM((64, bsz), F32),
                        pltpu.VMEM((SSD_W, bsz), F32)],
        compiler_params=_cparams("arbitrary"),
        name="step_ssd",
    )(proj_t, buf_t, bias_col, alog_col, cw_t, dskip_col, ssdg_col, s_t)


def _step_gla_kernel(proj_ref, wf2t_ref, bf_ref, glag_ref, s_ref, s_out, y_ref, dec_s):
    h = pl.program_id(0)

    @pl.when(h == 0)
    def _():
        la = _log_sigmoid(_dot(wf2t_ref[...], proj_ref[GT:GT + LANE, :].astype(BF16)) + bf_ref[...]) / GLA_TAU
        dec_s[...] = jnp.exp(la)

    dec = _rows(dec_s, GLA_DK * h, GLA_DK)
    q = _rows(proj_ref, GQ + GLA_DK * h, GLA_DK) * (GLA_DK ** -0.5)
    k = _rows(proj_ref, GK + GLA_DK * h, GLA_DK)
    v = _rows(proj_ref, GV + 64 * h, 64)
    acc = jnp.zeros_like(v)
    for j in range(GLA_DK):
        sn = dec[j:j + 1, :] * s_ref[j] + k[j:j + 1, :] * v
        s_out[j] = sn
        acc = acc + q[j:j + 1, :] * sn
    gg = _rows(proj_ref, GG + 64 * h, 64)
    y_ref[...] = (acc * lax.rsqrt(jnp.mean(acc * acc, axis=0, keepdims=True) + EPS)
                  * _rows(glag_ref, 64 * h, 64) * _silu(gg))


def _step_gla(proj_t, layer, wf2_t, bf_col, glag_col, s_t):
    bsz = proj_t.shape[1]
    return pl.pallas_call(
        _step_gla_kernel,
        grid=(GLA_HEADS,),
        in_specs=[_const_spec((NP, bsz)), _const_spec((LANE, LANE)), _const_spec((LANE, 1)),
                  _const_spec((256, 1)),
                  pl.BlockSpec((None, None, GLA_DK, 64, bsz), lambda h: (layer, h, 0, 0, 0))],
        out_specs=[pl.BlockSpec((None, GLA_DK, 64, bsz), lambda h: (h, 0, 0, 0)),
                   pl.BlockSpec((64, bsz), lambda h: (h, 0))],
        out_shape=[jax.ShapeDtypeStruct((GLA_HEADS, GLA_DK, 64, bsz), F32),
                   jax.ShapeDtypeStruct((256, bsz), F32)],
        scratch_shapes=[pltpu.VMEM((LANE, bsz), F32)],
        compiler_params=_cparams("arbitrary"),
        name="step_gla",
    )(proj_t, wf2_t, bf_col, glag_col, s_t)


def _prep_w_in(w_in):
    parts, acc = [], 0
    for s in IN_SIZES:
        parts.append(w_in[..., acc:acc + s])
        acc += s
    mq, mk, mv, mo, mi, mf, sz, sxbc, sdt, gq, gk, gv, gg, gf = parts
    gates = jnp.concatenate([mi, mf, sdt, gf, jnp.zeros(w_in.shape[:-1] + (LANE - 32,), w_in.dtype)], axis=-1)
    return jnp.concatenate([mq, mk, mv, mo, sz, sxbc, gq, gk, gv, gg, gates], axis=-1).astype(BF16)


def _pad_row(x, width):
    return jnp.pad(x, [(0, 0), (0, width - x.shape[-1])])


def kernel(x_prompt, x_sample, state_mlstm_c, state_mlstm_n, state_mlstm_m, state_ssd, state_ssd_conv,
           state_gla, state_ffn_conv, norm1_g, w_in, mlstm_b_i, mlstm_b_f, mlstm_norm_g, ssd_conv_w,
           ssd_conv_b, ssd_dt_bias, ssd_a_log, ssd_d, ssd_norm_g, gla_w_f2, gla_b_f, gla_norm_g, w_out,
           norm2_g, w_up, ffn_conv_w, ffn_conv_b, w_down, final_norm_g):
    bsz, seq, _ = x_prompt.shape
    dbs = x_sample.shape[0]
    depth = w_in.shape[0]

    w_in_p = _prep_w_in(w_in)
    w_out_b = w_out.astype(BF16)
    w_up_p = w_up.astype(BF16)
    w_down_p = w_down.astype(BF16)
    ffn_cwb = jnp.concatenate([ffn_conv_w, ffn_conv_b[:, None, :],
                               jnp.zeros((depth, SUBLANE - FFN_CONV - 1, 2 * D_FF), F32)], axis=1)
    zeros8 = jnp.zeros((depth, 8), F32)
    bias_rows = _pad_row(jnp.concatenate([mlstm_b_i, mlstm_b_f, ssd_dt_bias], axis=-1), LANE)[:, None, :]
    alog_rows = _pad_row(jnp.concatenate([zeros8, ssd_a_log], axis=-1), LANE)[:, None, :]
    ssd_cwb = jnp.concatenate([ssd_conv_w, ssd_conv_b[:, None, :],
                               jnp.zeros((depth, SUBLANE - SSD_CONV - 1, SSD_CONV_DIM), F32)], axis=1)
    dskip_rows = jnp.repeat(ssd_d, 64, axis=-1)[:, None, :]
    wf2_p = jnp.pad(gla_w_f2, [(0, 0), (GATE_GF, LANE - GATE_GF - GLA_RANK), (0, 0)]).astype(BF16)

    c_t = state_mlstm_c.transpose(0, 2, 3, 4, 1)
    n_t = state_mlstm_n.transpose(0, 2, 3, 1)
    m_t = state_mlstm_m.transpose(0, 2, 1)
    s_t = state_ssd.transpose(0, 2, 3, 4, 1)
    g_t = state_gla.transpose(0, 2, 3, 4, 1)
    conv_t = state_ssd_conv.transpose(0, 2, 1, 3)

    xp = x_prompt.reshape(bsz * seq, D_MODEL)
    xs = x_sample.reshape(dbs, D_MODEL)
    fg = final_norm_g[None, :]
    p_states, s_states = [], []
    for i in range(depth):
        g1 = norm1_g[i][None, :]
        g2 = norm2_g[i][None, :]
        mlg = mlstm_norm_g[i][None, :]
        ssdg = ssd_norm_g[i][None, :]
        glag = gla_norm_g[i][None, :]
        bfr = gla_b_f[i][None, :]
        last = i == depth - 1

        proj = _norm_proj(xp, g1, w_in_p, i)
        mix, caug, m_o, ssd_o, gla_o = _mixer(proj, bsz, seq, bias_rows[i], alog_rows[i], mlg, ssd_cwb[i],
                                              dskip_rows[i], ssdg, wf2_p[i], bfr, glag)
        xp, ffn_st = _out_ffn_seq(xp, mix, bsz, seq, i, w_out_b, g2, w_up_p, ffn_cwb[i], w_down_p, fg, last)
        caug4 = caug.reshape(bsz, ML_HEADS, ML_DK, LANE)
        ssd4 = ssd_o.reshape(bsz, SSD_HEADS, 64, LANE)
        gla4 = gla_o.reshape(bsz, GLA_HEADS, GLA_DK, 256)
        p_states.append((
            caug4[..., :64], caug4[..., 64], m_o[:, 0, :ML_HEADS],
            jnp.where(jnp.arange(SSD_HEADS)[None, :, None, None] < 4, ssd4[..., :64], ssd4[..., 64:]),
            proj.reshape(bsz, seq, NP)[:, seq - (SSD_CONV - 1):, SXBC:SXBC + SSD_CONV_DIM],
            jnp.stack([gla4[:, h, :, 64 * h:64 * (h + 1)] for h in range(GLA_HEADS)], axis=1),
            ffn_st[:, SUBLANE - (FFN_CONV - 1):, :],
        ))

        proj_t = _norm_proj_t(xs, g1, w_in_p, i)
        c_n, n_n, m_n, y_ml = _step_mlstm(proj_t, i, bias_rows[i].T, mlg.T, c_t, n_t, m_t)
        s_n, y_ssd, raw_s = _step_ssd(proj_t, i, conv_t, bias_rows[i].T, alog_rows[i].T, ssd_cwb[i].T,
                                      dskip_rows[i].T, ssdg.T, s_t)
        g_n, y_gla = _step_gla(proj_t, i, wf2_p[i].T, bfr.T, glag.T, g_t)
        mix_t = jnp.concatenate([y_ml, y_ssd, y_gla], axis=0)
        prev = state_ffn_conv[i].transpose(1, 0, 2)
        xs, up_s = _out_ffn_step(xs, mix_t, i, w_out_b, g2, w_up_p, ffn_cwb[i], w_down_p, fg, last, prev)
        s_states.append((
            c_n, n_n, m_n, s_n,
            jnp.concatenate([state_ssd_conv[i][:, 1:], raw_s[:, None, :]], axis=1),
            g_n,
            jnp.concatenate([state_ffn_conv[i][:, 1:], up_s[:, None, :]], axis=1),
        ))

    y_prompt = xp.reshape(bsz, seq, D_MODEL)
    y_sample = xs.reshape(dbs, 1, D_MODEL)

    def stk(sts, j):
        return jnp.stack([s[j] for s in sts], axis=0)

    s_perm = {0: (0, 4, 1, 2, 3), 1: (0, 3, 1, 2), 2: (0, 2, 1), 3: (0, 4, 1, 2, 3), 5: (0, 4, 1, 2, 3)}
    s_out = tuple(stk(s_states, j).transpose(s_perm[j]) if j in s_perm else stk(s_states, j) for j in range(7))
    return (y_prompt, y_sample) + tuple(stk(p_states, j) for j in range(7)) + s_out
```

```python
import functools

import jax
import jax.numpy as jnp
from jax import lax
from jax.experimental import pallas as pl
from jax.experimental.pallas import tpu as pltpu

F32 = jnp.float32
BF16 = jnp.bfloat16
I32 = jnp.int32

D_MODEL = 1024
DEPTH = 4
ML_HEADS = 4
ML_DK = 64
SSD_HEADS = 8
SSD_W = 512
SSD_CONV = 4
SSD_CONV_DIM = 768
GLA_HEADS = 4
GLA_DK = 32
GLA_RANK = 16
GLA_TAU = 16.0
D_FF = 2816
FFN_CONV = 3
EPS = 1e-6
IN_SIZES = (256, 256, 256, 256, 4, 4, 512, 768, 8, 128, 128, 256, 256, 16)

MQ, MK, MV, MO, SZ, SXBC, GQ, GK, GV, GG, GT, NP = (
    0, 256, 512, 768, 1024, 1536, 2304, 2432, 2560, 2816, 3072, 3200)
GATE_LI, GATE_LF, GATE_DT, GATE_GF = 0, 4, 8, 16

LANE = 128
SUBLANE = 8
MIX_T = 128
GLA_C = 8
FFN_TM = 512
FFN_ROWS = 1
FF_CHUNK = 256
N_FF_CHUNK = D_FF // FF_CHUNK
VMEM_LIMIT = 56 * 1024 * 1024
VMEM_LIMIT_FUSED = 62 * 1024 * 1024


def _cparams(*sem, vmem=VMEM_LIMIT):
    return pltpu.CompilerParams(dimension_semantics=sem if sem else None, vmem_limit_bytes=vmem)


def _const_spec(shape):
    nd = len(shape)
    return pl.BlockSpec(shape, lambda *_: (0,) * nd)


def _layer_spec(shape, layer, resident=False):
    nd = len(shape)
    return pl.BlockSpec((None,) + tuple(shape), lambda *_: (layer,) + (0,) * nd,
                        pipeline_mode=pl.Buffered(1) if resident else None)


def _sigmoid(x):
    return 1.0 / (1.0 + jnp.exp(-x))


def _silu(x):
    h = 0.5 * x
    return h + h * jnp.tanh(h)


def _softplus(x):
    return jnp.maximum(x, 0.0) + jnp.log(1.0 + jnp.exp(-jnp.abs(x)))


def _log_sigmoid(x):
    return -_softplus(-x)


def _rmsnorm_rows(x, g):
    ms = jnp.mean(x * x, axis=-1, keepdims=True)
    return x * lax.rsqrt(ms + EPS) * g


def _dot(a, b):
    return jnp.dot(a, b, preferred_element_type=F32)


def _dot_nt(a, b):
    return lax.dot_general(a, b, (((1,), (1,)), ((), ())), preferred_element_type=F32)


def _split3(x):
    x1 = x.astype(BF16)
    r1 = x - x1.astype(F32)
    x2 = r1.astype(BF16)
    x3 = (r1 - x2.astype(F32)).astype(BF16)
    return x1, x2, x3


def _cumsum_rows(x, tri):
    c = _dot(tri, jnp.concatenate(_split3(x), axis=1))
    return c[:, :LANE] + c[:, LANE:2 * LANE] + c[:, 2 * LANE:]


def _shift_rows(x, k, carry):
    r = pltpu.roll(x, k, 0)
    c = pltpu.roll(carry, k, 0)
    row = lax.broadcasted_iota(I32, c.shape, 0)
    head = jnp.where(row < k, c, r[:SUBLANE])
    return jnp.concatenate([head, r[SUBLANE:]], axis=0)


def _seg_rmsnorm(x, seg):
    w = x.shape[1]
    if seg % LANE == 0:
        parts = []
        for s0 in range(0, w, seg):
            xs = x[:, s0:s0 + seg]
            parts.append(xs * lax.rsqrt(jnp.mean(xs * xs, axis=-1, keepdims=True) + EPS))
        return jnp.concatenate(parts, axis=1)
    parts = []
    lane = lax.broadcasted_iota(I32, (x.shape[0], LANE), 1)
    for s0 in range(0, w, LANE):
        xs = x[:, s0:s0 + LANE]
        sq = xs * xs
        lo = jnp.sum(jnp.where(lane < seg, sq, 0.0), axis=-1, keepdims=True)
        hi = jnp.sum(jnp.where(lane >= seg, sq, 0.0), axis=-1, keepdims=True)
        r = jnp.where(lane < seg, lax.rsqrt(lo / seg + EPS), lax.rsqrt(hi / seg + EPS))
        parts.append(xs * r)
    return jnp.concatenate(parts, axis=1)


def _drive(chains):
    chains = list(chains)
    while chains:
        for c in list(chains):
            try:
                next(c)
            except StopIteration:
                chains.remove(c)


def _norm_proj_kernel(x_ref, g_ref, w_ref, o_ref):
    h = _rmsnorm_rows(x_ref[...], g_ref[...]).astype(BF16)
    for n0 in range(0, NP, 512):
        n1 = min(n0 + 512, NP)
        o_ref[:, n0:n1] = _dot(h, w_ref[:, n0:n1])


def _norm_proj(x2d, g, w_all, layer):
    m = x2d.shape[0]
    tm = min(512, m)
    return pl.pallas_call(
        _norm_proj_kernel,
        grid=(m // tm,),
        in_specs=[pl.BlockSpec((tm, D_MODEL), lambda i: (i, 0)),
                  _const_spec((1, D_MODEL)),
                  _layer_spec((D_MODEL, NP), layer)],
        out_specs=pl.BlockSpec((tm, NP), lambda i: (i, 0)),
        out_shape=jax.ShapeDtypeStruct((m, NP), F32),
        compiler_params=_cparams("arbitrary"),
        name="norm_proj",
    )(x2d, g, w_all)


class _MixerRow:
    def __init__(self, proj_ref, bias_ref, alog_ref, mlg_ref, cw_ref, dskip_ref, ssdg_ref, wf2_ref,
                 bf_ref, glag_ref, mix_ref, caug_ref, m_ref, ssd_ref, gla_ref, conv_s, cum_s, live):
        self.proj_ref, self.bias_ref, self.alog_ref, self.mlg_ref = proj_ref, bias_ref, alog_ref, mlg_ref
        self.cw_ref, self.dskip_ref, self.ssdg_ref, self.wf2_ref = cw_ref, dskip_ref, ssdg_ref, wf2_ref
        self.bf_ref, self.glag_ref, self.mix_ref, self.caug_ref = bf_ref, glag_ref, mix_ref, caug_ref
        self.m_ref, self.ssd_ref, self.gla_ref, self.conv_s, self.cum_s = m_ref, ssd_ref, gla_ref, conv_s, cum_s
        self.live = live
        self.out = {}
        self.levels = {}

    def keep(self, new, old):
        return jnp.where(self.live, new, old)

    def setup(self):
        T = MIX_T
        proj_ref = self.proj_ref
        self.lane = lane = lax.broadcasted_iota(I32, (T, LANE), 1)
        self.rowi = lax.broadcasted_iota(I32, (T, T), 0)
        self.coli = lax.broadcasted_iota(I32, (T, T), 1)
        self.causal = self.rowi >= self.coli
        tri = jnp.where(self.causal, 1.0, 0.0).astype(BF16)
        self.row128 = lax.broadcasted_iota(I32, (LANE, LANE), 0)
        self.g_blk = g_blk = proj_ref[:, GT:GT + LANE] + self.bias_ref[...]
        is_lf = (lane >= GATE_LF) & (lane < GATE_DT)
        is_dt = (lane >= GATE_DT) & (lane < GATE_GF)
        self.dt = dt = _softplus(jnp.where(is_lf, -g_blk, g_blk))
        a_row = -jnp.exp(self.alog_ref[...])
        self.cum_g = _cumsum_rows(jnp.where(is_lf, -dt, jnp.where(is_dt, dt * a_row, 0.0)), tri)
        yield
        self.p1_t = jnp.where(lane < GATE_LF, g_blk, self.cum_g).T
        self.dt_t = dt.T
        self.k_ml_t = [proj_ref[:, MK + LANE * p:MK + LANE * (p + 1)].T for p in range(2)]
        raw = proj_ref[:, SXBC:SXBC + SSD_CONV_DIM]
        carry = self.conv_s[...]
        cw_ref = self.cw_ref
        conv = (cw_ref[4:5, :] + _shift_rows(raw, 3, carry) * cw_ref[0:1, :]
                + _shift_rows(raw, 2, carry) * cw_ref[1:2, :]
                + _shift_rows(raw, 1, carry) * cw_ref[2:3, :] + raw * cw_ref[3:4, :])
        self.conv_s[...] = self.keep(raw[T - SUBLANE:T, :], carry)
        self.xbc = xbc = _silu(conv)
        self.x_ssd_t = [xbc[:, LANE * j:LANE * (j + 1)].T for j in range(SSD_HEADS // 2)]
        self.b_p = b_p = xbc[:, SSD_W:SSD_W + LANE]
        c_p = xbc[:, SSD_W + LANE:SSD_W + 2 * LANE]
        self.c_pb = c_p.astype(BF16)
        self.cb = []
        for g in range(2):
            in_g = (lane >= 64 * g) & (lane < 64 * (g + 1))
            self.cb.append(_dot_nt(jnp.where(in_g, c_p, 0.0).astype(BF16), b_p.astype(BF16)))
            yield
        la = _log_sigmoid(_dot(g_blk.astype(BF16), self.wf2_ref[...]) + self.bf_ref[...]) / GLA_TAU
        yield
        self.cum = cum = _cumsum_rows(la, tri)
        yield
        self.cum_s[...] = cum
        self.cum_t = cum.T
        self.q = proj_ref[:, GQ:GQ + LANE] * (GLA_DK ** -0.5)
        self.k = proj_ref[:, GK:GK + LANE]
        self.k_gla_t = self.k.T
        self.v = proj_ref[:, GV:GV + 256]
        r_s = lax.broadcasted_iota(I32, (LANE, 256), 0)
        c_s = lax.broadcasted_iota(I32, (LANE, 256), 1)
        self.same_head = (r_s >> 5) == (c_s >> 6)
        self.tpos = lax.broadcasted_iota(I32, (T, LANE), 0)

    def chains(self):
        return ([self.ml_head(h) for h in range(ML_HEADS)] + [self.ssd_pair(j) for j in range(SSD_HEADS // 2)]
                + [self.gla_state(), self.gla_diag()] + [self.gla_head(h) for h in range(GLA_HEADS)])

    def ml_head(self, h):
        T = MIX_T
        proj_ref, lane, cum_g, p1_t = self.proj_ref, self.lane, self.cum_g, self.p1_t
        pair, lo = h // 2, 64 * (h % 2)
        hm = (lane >= lo) & (lane < lo + 64)
        qm = jnp.where(hm, proj_ref[:, MQ + LANE * pair:MQ + LANE * (pair + 1)] * 0.125, 0.0).astype(BF16)
        km = jnp.where(hm, proj_ref[:, MK + LANE * pair:MK + LANE * (pair + 1)], 0.0)
        qk = _dot_nt(qm, km.astype(BF16))
        yield
        c_pair = self.caug_ref[LANE * pair:LANE * (pair + 1), :]
        qc = _dot(qm, c_pair.astype(BF16))
        yield
        v_pair = proj_ref[:, MV + LANE * pair:MV + LANE * (pair + 1)]
        v_lo = pltpu.roll(v_pair, 64, 1) if h % 2 else v_pair
        vaug = jnp.where(lane < 64, v_lo, jnp.where(lane == 64, 1.0, 0.0)).astype(BF16)
        b_col = cum_g[:, GATE_LF + h:GATE_LF + h + 1]
        b_row = p1_t[GATE_LF + h:GATE_LF + h + 1, :]
        li_row = p1_t[GATE_LI + h:GATE_LI + h + 1, :]
        m_prev = self.m_ref[:, h:h + 1]
        d_mat = jnp.where(self.causal, b_col - b_row + li_row, -jnp.inf)
        yield
        row_max = jnp.max(d_mat, axis=1, keepdims=True)
        yield
        a_col = b_col + m_prev
        mt = jnp.maximum(a_col, row_max)
        s_mat = (qk * jnp.exp(d_mat - mt)).astype(BF16)
        numaug = _dot(s_mat, vaug) + jnp.exp(a_col - mt) * qc
        yield
        den = numaug[:, 64:65]
        hn = numaug / jnp.maximum(jnp.abs(den), jnp.exp(-mt))
        hv = jnp.where(lane < 64, hn, 0.0)
        yield
        self.out["ml", h] = hv * lax.rsqrt(jnp.sum(hv * hv, axis=-1, keepdims=True) / 64.0 + EPS)
        b_last = b_col[T - 1:T, :]
        m_new = mt[T - 1:T, :]
        g_state = jnp.exp(b_last + m_prev - m_new)
        g_s_row = jnp.exp(b_last - b_row + li_row - m_new)
        u = _dot((self.k_ml_t[pair][lo:lo + 64, :] * g_s_row).astype(BF16), vaug)
        yield
        c_old = c_pair[lo:lo + 64]
        self.caug_ref[LANE * pair + lo:LANE * pair + lo + 64, :] = self.keep(g_state * c_old + u, c_old)
        self.out["m", h] = m_new

    def ssd_pair(self, j):
        T = MIX_T
        lane, cum_g, p1_t = self.lane, self.cum_g, self.p1_t
        g = j // 2
        x_pair = self.xbc[:, LANE * j:LANE * (j + 1)]
        s_pair = self.ssd_ref[LANE * j:LANE * (j + 1), :]
        inter = _dot_nt(self.c_pb, s_pair.astype(BF16))
        yield
        y_pair = None
        ecs, wrow, dec = [], [], []
        for hh in range(2):
            h = 2 * j + hh
            cs_col = cum_g[:, GATE_DT + h:GATE_DT + h + 1]
            cs_row = p1_t[GATE_DT + h:GATE_DT + h + 1, :]
            dt_row = self.dt_t[GATE_DT + h:GATE_DT + h + 1, :]
            seg = jnp.where(self.causal, cs_col - cs_row, -jnp.inf)
            yield
            sc = (self.cb[g] * jnp.exp(seg) * dt_row).astype(BF16)
            in_h = (lane >= 64 * hh) & (lane < 64 * (hh + 1))
            part = _dot(sc, jnp.where(in_h, x_pair, 0.0).astype(BF16))
            yield
            y_pair = part if y_pair is None else y_pair + part
            cs_last = cs_col[T - 1:T, :]
            ecs.append(jnp.exp(cs_col))
            wrow.append(jnp.exp(cs_last - cs_row) * dt_row)
            dec.append(jnp.exp(cs_last))
        self.out["ssd", j] = y_pair + jnp.where(lane < 64, ecs[0], ecs[1]) * inter
        wx_t = (self.x_ssd_t[j] * jnp.where(self.row128 < 64, wrow[0], wrow[1])).astype(BF16)
        in_g = (lane >= 64 * g) & (lane < 64 * (g + 1))
        u = _dot(wx_t, jnp.where(in_g, self.b_p, 0.0).astype(BF16))
        yield
        self.ssd_ref[LANE * j:LANE * (j + 1), :] = self.keep(
            jnp.where(self.row128 < 64, dec[0], dec[1]) * s_pair + u, s_pair)

    def gla_state(self):
        T = MIX_T
        cum = self.cum
        s_gla = self.gla_ref[...]
        self.out["gla_inter"] = _dot((self.q * jnp.exp(cum)).astype(BF16), s_gla.astype(BF16))
        yield
        cum_t = self.cum_t
        u = _dot((self.k_gla_t * jnp.exp(cum_t[:, T - 1:T] - cum_t)).astype(BF16), self.v.astype(BF16))
        yield
        self.gla_ref[...] = self.keep(jnp.exp(self.cum_t[:, T - 1:T]) * s_gla + jnp.where(self.same_head, u, 0.0), s_gla)

    def gla_level(self, w):
        if w not in self.levels:
            T = MIX_T
            is_t = (self.tpos & (2 * w - 1)) >= w
            mid = jnp.concatenate(
                [jnp.broadcast_to(self.cum_s[2 * w * blk + w - 1:2 * w * blk + w, :], (2 * w, LANE))
                 for blk in range(T // (2 * w))], axis=0)
            e = jnp.exp(jnp.where(is_t, self.cum - mid, mid - self.cum))
            ql = jnp.where(is_t, self.q * e, 0.0)
            kl = jnp.where(is_t, 0.0, self.k * e).astype(BF16)
            shift = (2 * w).bit_length() - 1
            self.levels[w] = (ql, kl, (self.rowi >> shift) == (self.coli >> shift))
        return self.levels[w]

    def gla_head(self, h):
        lane = self.lane
        in_h = (lane >= GLA_DK * h) & (lane < GLA_DK * (h + 1))
        att = None
        w = GLA_C
        while w < MIX_T:
            ql, kl, same_blk = self.gla_level(w)
            a = jnp.where(same_blk, _dot_nt(jnp.where(in_h, ql, 0.0).astype(BF16), kl), 0.0)
            yield
            att = a if att is None else att + a
            w *= 2
        v_pair = self.v[:, LANE * (h // 2):LANE * (h // 2 + 1)]
        in_half = (lane >= 64) if h % 2 else (lane < 64)
        self.out["gla_o", h] = _dot(att.astype(BF16), jnp.where(in_half, v_pair, 0.0).astype(BF16))
        yield

    def gla_diag(self):
        expand = jnp.where(self.same_head, 1.0, 0.0).astype(BF16)
        nblk = MIX_T // GLA_C
        o = None
        for jj in range(GLA_C):
            def rows(ref, c0, c1):
                return jnp.concatenate(
                    [jnp.broadcast_to(ref[GLA_C * i + jj:GLA_C * i + jj + 1, c0:c1], (GLA_C, c1 - c0))
                     for i in range(nblk)], axis=0)
            k_s = rows(self.proj_ref, GK, GK + LANE)
            c_srow = rows(self.cum_s, 0, LANE)
            v_s = rows(self.proj_ref, GV, GV + 256)
            valid = (self.tpos & (GLA_C - 1)) >= jj
            e = jnp.exp(jnp.where(valid, self.cum - c_srow, -jnp.inf))
            part = _dot((self.q * k_s * e).astype(BF16), expand) * v_s
            yield
            o = part if o is None else o + part
        self.out["gla_diag"] = o

    def finish(self):
        proj_ref, mix_ref, lane, out = self.proj_ref, self.mix_ref, self.lane, self.out
        m_row = self.m_ref[...]
        m_lane = lax.broadcasted_iota(I32, m_row.shape, 1)
        for h in range(ML_HEADS):
            m_row = jnp.where((m_lane == h) & self.live, out["m", h], m_row)
        self.m_ref[...] = m_row
        y_ml = jnp.concatenate(
            [jnp.where(lane < 64, out["ml", 2 * p], pltpu.roll(out["ml", 2 * p + 1], 64, 1)) for p in range(2)],
            axis=1)
        mix_ref[:, 0:256] = (_sigmoid(proj_ref[:, MO:MO + 256]) * (y_ml * self.mlg_ref[...])).astype(BF16)
        y_s = (jnp.concatenate([out["ssd", j] for j in range(SSD_HEADS // 2)], axis=1)
               + self.dskip_ref[...] * self.xbc[:, :SSD_W])
        z = proj_ref[:, SZ:SZ + SSD_W]
        mix_ref[:, 256:768] = (_seg_rmsnorm(y_s * _silu(z), 256) * self.ssdg_ref[...]).astype(BF16)
        o = (out["gla_inter"] + out["gla_diag"]
             + jnp.concatenate([out["gla_o", 2 * p] + out["gla_o", 2 * p + 1] for p in range(2)], axis=1))
        gg = proj_ref[:, GG:GG + 256]
        mix_ref[:, 768:1024] = (_seg_rmsnorm(o, 64) * self.glag_ref[...] * _silu(gg)).astype(BF16)


def _ffn_row(x_ref, mix_fn, wo_ref, g2_ref, wup_ref, cwb_ref, wd_ref, fg_ref, o_ref, h2_s, act_s,
             prev_fn, keep_fn, final):
    xn = x_ref[...] + _dot(mix_fn(), wo_ref[...])
    yield
    o_ref[...] = xn
    h2_s[...] = _rmsnorm_rows(xn, g2_ref[...]).astype(BF16)
    for c in range(N_FF_CHUNK):
        h2 = h2_s[...]
        halves = []
        for ug in range(2):
            col = slice(ug * D_FF + c * FF_CHUNK, ug * D_FF + (c + 1) * FF_CHUNK)
            up = _dot(h2, wup_ref[:, col])
            yield
            x1, x2 = prev_fn(col, up)
            keep_fn(col, up)
            halves.append(cwb_ref[3:4, col] + x2 * cwb_ref[0:1, col] + x1 * cwb_ref[1:2, col]
                          + up * cwb_ref[2:3, col])
        u, gate = halves
        act_s[:, c * FF_CHUNK:(c + 1) * FF_CHUNK] = (_silu(gate) * u).astype(BF16)
    y = o_ref[...] + _dot(act_s[...], wd_ref[...])
    yield
    o_ref[...] = _rmsnorm_rows(y, fg_ref[...]) if final else y


def _out_ffn_step_kernel(x_ref, mix_ref, wo_ref, g2_ref, wup_ref, cwb_ref, wd_ref, fg_ref, prev_ref, o_ref, up_ref,
                         h2_s, act_s, *, final):
    def mix_fn():
        mix = jnp.concatenate([mix_ref[c0:c0 + LANE, :].T for c0 in range(0, D_MODEL, LANE)], axis=1)
        return mix.astype(BF16)

    def prev_fn(col, up):
        return prev_ref[1, :, col], prev_ref[0, :, col]

    def keep_fn(col, up):
        up_ref[:, col] = up

    _drive([_ffn_row(x_ref, mix_fn, wo_ref, g2_ref, wup_ref, cwb_ref, wd_ref, fg_ref, o_ref, h2_s, act_s,
                     prev_fn, keep_fn, final)])


def _ffn_w_specs(layer):
    return [_layer_spec((D_MODEL, D_MODEL), layer, resident=True), _const_spec((1, D_MODEL)),
            _layer_spec((D_MODEL, 2 * D_FF), layer, resident=True),
            _const_spec((SUBLANE, 2 * D_FF)),
            _layer_spec((D_FF, D_MODEL), layer, resident=True), _const_spec((1, D_MODEL))]


def _rounds(chains, per_yield=1):
    chains = list(chains)
    n = 0
    while chains:
        for c in list(chains):
            try:
                next(c)
            except StopIteration:
                chains.remove(c)
        n += 1
        if n % per_yield == 0:
            yield


def _mix_ffn_kernel(proj_ref, x_ref, bias_ref, alog_ref, mlg_ref, cw_ref, dskip_ref, ssdg_ref, wf2_ref, bf_ref,
                    glag_ref, wo_ref, g2_ref, wup_ref, cwb_ref, wd_ref, fg_ref,
                    o_ref, st_ref, caug_ref, m_ref, ssd_ref, gla_ref,
                    conv_s, cum_s, mix_s, h2_s, act_s, *, n_steps, nt, final):
    g = pl.program_id(0)
    live = g < n_steps
    t_m = jnp.minimum(g, n_steps - 1) % nt
    t_f = jnp.maximum(g - 1, 0) % nt
    slot = g % 2

    @pl.when((t_m == 0) & live)
    def _():
        caug_ref[...] = jnp.zeros_like(caug_ref)
        m_ref[...] = jnp.zeros_like(m_ref)
        ssd_ref[...] = jnp.zeros_like(ssd_ref)
        gla_ref[...] = jnp.zeros_like(gla_ref)
        conv_s[...] = jnp.zeros_like(conv_s)

    @pl.when(t_f == 0)
    def _():
        st_ref[...] = jnp.zeros_like(st_ref)

    @pl.when(g == 0)
    def _():
        mix_s[...] = jnp.zeros_like(mix_s)

    n_rows, tm = x_ref.shape[0], x_ref.shape[1]

    def ffn_chain(r):
        st = st_ref.at[r]

        def prev_fn(col, up):
            carry = st[:, col]
            return _shift_rows(up, 1, carry), _shift_rows(up, 2, carry)

        def keep_fn(col, up):
            st[:, col] = up[tm - SUBLANE:tm, :]

        return _ffn_row(x_ref.at[r], lambda: mix_s[1 - slot, r], wo_ref, g2_ref, wup_ref, cwb_ref, wd_ref, fg_ref,
                        o_ref.at[r], h2_s.at[r], act_s.at[r], prev_fn, keep_fn, final)

    def mixer_program():
        for k in range(tm // MIX_T):
            tok = pl.ds(k * MIX_T, MIX_T)
            for r in range(n_rows):
                row = _MixerRow(proj_ref.at[r, tok], bias_ref, alog_ref, mlg_ref, cw_ref, dskip_ref, ssdg_ref,
                                wf2_ref, bf_ref, glag_ref, mix_s.at[slot, r, tok], caug_ref.at[r], m_ref.at[r],
                                ssd_ref.at[r], gla_ref.at[r], conv_s.at[r], cum_s.at[r], live)
                yield from _rounds([row.setup()], 2)
                yield from _rounds(row.chains(), 2)
                row.finish()

    _drive([ffn_chain(r) for r in range(n_rows)] + [mixer_program()])


def _mix_ffn(x2d, proj, bsz, seq, layer, bias_row, alog_row, mlg, cwb_ssd, dskip, ssdg, wf2p, bfr, glag,
             wo, g2, wup, cwb, wd, fg, final):
    tm = min(FFN_TM, seq)
    nt = seq // tm
    rows = FFN_ROWS if bsz % FFN_ROWS == 0 else 1
    n_steps = (bsz // rows) * nt
    row_spec = lambda w: _const_spec((1, w))
    mix_idx = lambda g: (jnp.minimum(g, n_steps - 1) // nt, jnp.minimum(g, n_steps - 1) % nt, 0)
    ffn_idx = lambda g: (jnp.maximum(g - 1, 0) // nt, jnp.maximum(g - 1, 0) % nt, 0)
    mix_pair = lambda g: (jnp.minimum(g, n_steps - 1) // nt, 0, 0)
    ffn_pair = lambda g: (jnp.maximum(g - 1, 0) // nt, 0, 0)
    out, st, caug, m_o, ssd_o, gla_o = pl.pallas_call(
        functools.partial(_mix_ffn_kernel, n_steps=n_steps, nt=nt, final=final),
        grid=(n_steps + 1,),
        in_specs=[pl.BlockSpec((rows, tm, NP), mix_idx), pl.BlockSpec((rows, tm, D_MODEL), ffn_idx),
                  row_spec(LANE), row_spec(LANE), row_spec(256),
                  _const_spec((SUBLANE, SSD_CONV_DIM)), row_spec(SSD_W), row_spec(SSD_W),
                  _const_spec((LANE, LANE)), row_spec(LANE), row_spec(256)] + _ffn_w_specs(layer),
        out_specs=[pl.BlockSpec((rows, tm, D_MODEL), ffn_idx),
                   pl.BlockSpec((rows, SUBLANE, 2 * D_FF), ffn_pair),
                   pl.BlockSpec((rows, 256, LANE), mix_pair),
                   pl.BlockSpec((rows, 1, LANE), mix_pair),
                   pl.BlockSpec((rows, 512, LANE), mix_pair),
                   pl.BlockSpec((rows, LANE, 256), mix_pair)],
        out_shape=[jax.ShapeDtypeStruct((bsz, seq, D_MODEL), F32),
                   jax.ShapeDtypeStruct((bsz, SUBLANE, 2 * D_FF), F32),
                   jax.ShapeDtypeStruct((bsz, 256, LANE), F32),
                   jax.ShapeDtypeStruct((bsz, 1, LANE), F32),
                   jax.ShapeDtypeStruct((bsz, 512, LANE), F32),
                   jax.ShapeDtypeStruct((bsz, LANE, 256), F32)],
        scratch_shapes=[pltpu.VMEM((rows, SUBLANE, SSD_CONV_DIM), F32), pltpu.VMEM((rows, MIX_T, LANE), F32),
                        pltpu.VMEM((2, rows, tm, D_MODEL), BF16),
                        pltpu.VMEM((rows, tm, D_MODEL), BF16), pltpu.VMEM((rows, tm, D_FF), BF16)],
        compiler_params=_cparams("arbitrary", vmem=VMEM_LIMIT_FUSED),
        name="mix_ffn",
    )(proj.reshape(bsz, seq, NP), x2d.reshape(bsz, seq, D_MODEL), bias_row, alog_row, mlg, cwb_ssd, dskip, ssdg,
      wf2p, bfr, glag, wo, g2, wup, cwb, wd, fg)
    return out.reshape(bsz * seq, D_MODEL), st, caug, m_o, ssd_o, gla_o


def _out_ffn_step(x2d, mix, layer, wo, g2, wup, cwb, wd, fg, final, prev):
    m = x2d.shape[0]
    return pl.pallas_call(
        functools.partial(_out_ffn_step_kernel, final=final),
        grid=(1,),
        in_specs=[_const_spec((m, D_MODEL)), _const_spec((D_MODEL, m))] + _ffn_w_specs(layer)
        + [_const_spec((FFN_CONV - 1, m, 2 * D_FF))],
        out_specs=[_const_spec((m, D_MODEL)), _const_spec((m, 2 * D_FF))],
        out_shape=[jax.ShapeDtypeStruct((m, D_MODEL), F32),
                   jax.ShapeDtypeStruct((m, 2 * D_FF), F32)],
        scratch_shapes=[pltpu.VMEM((m, D_MODEL), BF16), pltpu.VMEM((m, D_FF), BF16)],
        compiler_params=_cparams("arbitrary"),
        name="out_ffn_step",
    )(x2d, mix, wo, g2, wup, cwb, wd, fg, prev)


def _norm_proj_t_kernel(x_ref, g_ref, w_ref, o_ref):
    h = _rmsnorm_rows(x_ref[...], g_ref[...]).astype(BF16)
    for n0 in range(0, NP, 512):
        n1 = min(n0 + 512, NP)
        blk = _dot(h, w_ref[:, n0:n1])
        for c0 in range(0, n1 - n0, LANE):
            o_ref[n0 + c0:n0 + c0 + LANE, :] = blk[:, c0:c0 + LANE].T


def _norm_proj_t(x2d, g, w_all, layer):
    m = x2d.shape[0]
    return pl.pallas_call(
        _norm_proj_t_kernel,
        grid=(1,),
        in_specs=[_const_spec((m, D_MODEL)), _const_spec((1, D_MODEL)), _layer_spec((D_MODEL, NP), layer)],
        out_specs=_const_spec((NP, m)),
        out_shape=jax.ShapeDtypeStruct((NP, m), F32),
        compiler_params=_cparams("arbitrary"),
        name="norm_proj_t",
    )(x2d, g, w_all)


def _pick_row(x, r):
    row = lax.broadcasted_iota(I32, x.shape, 0)
    return jnp.sum(jnp.where(row == r, x, 0.0), axis=0, keepdims=True)


def _rows(ref, start, size):
    return ref[pl.ds(pl.multiple_of(start, size), size), :]


def _step_mlstm_kernel(proj_ref, bias_ref, mlg_ref, c_ref, n_ref, m_ref, c_out, n_out, m_out, y_ref):
    h = pl.program_id(0)
    gates = proj_ref[GT:GT + LANE, :] + bias_ref[...]
    li = _pick_row(gates, GATE_LI + h)
    lf = _log_sigmoid(_pick_row(gates, GATE_LF + h))
    m0 = m_ref[pl.ds(h, 1), :]
    mt = jnp.maximum(lf + m0, li)
    w_old = jnp.exp(lf + m0 - mt)
    q = _rows(proj_ref, MQ + ML_DK * h, ML_DK) * 0.125
    kw = _rows(proj_ref, MK + ML_DK * h, ML_DK) * jnp.exp(li - mt)
    v = _rows(proj_ref, MV + 64 * h, 64)
    acc = jnp.zeros_like(v)
    for d in range(ML_DK):
        cn = w_old * c_ref[d] + kw[d:d + 1, :] * v
        c_out[d] = cn
        acc = acc + q[d:d + 1, :] * cn
    nn = w_old * n_ref[...] + kw
    n_out[...] = nn
    den = jnp.sum(q * nn, axis=0, keepdims=True)
    hn = acc / jnp.maximum(jnp.abs(den), jnp.exp(-mt))
    yn = hn * lax.rsqrt(jnp.mean(hn * hn, axis=0, keepdims=True) + EPS)
    y_ref[...] = _sigmoid(_rows(proj_ref, MO + 64 * h, 64)) * (yn * _rows(mlg_ref, 64 * h, 64))
    m_out[pl.ds(h, 1), :] = mt


def _step_mlstm(proj_t, layer, bias_col, mlg_col, c_t, n_t, m_t):
    bsz = proj_t.shape[1]
    return pl.pallas_call(
        _step_mlstm_kernel,
        grid=(ML_HEADS,),
        in_specs=[_const_spec((NP, bsz)), _const_spec((LANE, 1)), _const_spec((256, 1)),
                  pl.BlockSpec((None, None, ML_DK, 64, bsz), lambda h: (layer, h, 0, 0, 0)),
                  pl.BlockSpec((None, None, ML_DK, bsz), lambda h: (layer, h, 0, 0)),
                  pl.BlockSpec((None, ML_HEADS, bsz), lambda h: (layer, 0, 0))],
        out_specs=[pl.BlockSpec((None, ML_DK, 64, bsz), lambda h: (h, 0, 0, 0)),
                   pl.BlockSpec((None, ML_DK, bsz), lambda h: (h, 0, 0)),
                   _const_spec((ML_HEADS, bsz)),
                   pl.BlockSpec((64, bsz), lambda h: (h, 0))],
        out_shape=[jax.ShapeDtypeStruct((ML_HEADS, ML_DK, 64, bsz), F32),
                   jax.ShapeDtypeStruct((ML_HEADS, ML_DK, bsz), F32),
                   jax.ShapeDtypeStruct((ML_HEADS, bsz), F32),
                   jax.ShapeDtypeStruct((256, bsz), F32)],
        compiler_params=_cparams("arbitrary"),
        name="step_mlstm",
    )(proj_t, bias_col, mlg_col, c_t, n_t, m_t)


def _step_ssd_kernel(proj_ref, buf_ref, bias_ref, alog_ref, cw_ref, dskip_ref, ssdg_ref, s_ref,
                     s_out, y_ref, raw_out, xbc_s, yh_s, ys_s):
    h = pl.program_id(0)
    g = h // (SSD_HEADS // 2)

    @pl.when(h == 0)
    def _():
        for c0 in range(0, SSD_CONV_DIM, LANE):
            raw_t = proj_ref[SXBC + c0:SXBC + c0 + LANE, :]
            cw = cw_ref[c0:c0 + LANE, :]
            acc = cw[:, 4:5] + raw_t * cw[:, 3:4]
            for j in range(SSD_CONV - 1):
                acc = acc + buf_ref[j, :, c0:c0 + LANE].T * cw[:, j:j + 1]
            xbc_s[c0:c0 + LANE, :] = _silu(acc)
            raw_out[:, c0:c0 + LANE] = raw_t.T

    dt = _pick_row(_softplus(proj_ref[GT:GT + LANE, :] + bias_ref[...]), GATE_DT + h)
    d_a = jnp.exp(dt * _pick_row(-jnp.exp(alog_ref[...]), GATE_DT + h))
    x_h = _rows(xbc_s, 64 * h, 64)
    b_g = _rows(xbc_s, SSD_W + 64 * g, 64)
    c_g = _rows(xbc_s, SSD_W + LANE + 64 * g, 64)
    u = dt * x_h
    for p in range(64):
        sn = d_a * s_ref[p] + u[p:p + 1, :] * b_g
        s_out[p] = sn
        yh_s[p:p + 1, :] = jnp.sum(sn * c_g, axis=0, keepdims=True)
    z = _rows(proj_ref, SZ + 64 * h, 64)
    ys_s[pl.ds(pl.multiple_of(64 * h, 64), 64), :] = (yh_s[...] + _rows(dskip_ref, 64 * h, 64) * x_h) * _silu(z)

    @pl.when(h % (SSD_HEADS // 2) == SSD_HEADS // 2 - 1)
    def _():
        grp = _rows(ys_s, 256 * g, 256)
        y_ref[...] = (grp * lax.rsqrt(jnp.mean(grp * grp, axis=0, keepdims=True) + EPS)
                      * _rows(ssdg_ref, 256 * g, 256))


def _step_ssd(proj_t, layer, buf_t, bias_col, alog_col, cw_t, dskip_col, ssdg_col, s_t):
    bsz = proj_t.shape[1]
    return pl.pallas_call(
        _step_ssd_kernel,
        grid=(SSD_HEADS,),
        in_specs=[_const_spec((NP, bsz)),
                  pl.BlockSpec((None, SSD_CONV - 1, bsz, SSD_CONV_DIM), lambda h: (layer, 0, 0, 0)),
                  _const_spec((LANE, 1)), _const_spec((LANE, 1)), _const_spec((SSD_CONV_DIM, SUBLANE)),
                  _const_spec((SSD_W, 1)), _const_spec((SSD_W, 1)),
                  pl.BlockSpec((None, None, 64, 64, bsz), lambda h: (layer, h, 0, 0, 0))],
        out_specs=[pl.BlockSpec((None, 64, 64, bsz), lambda h: (h, 0, 0, 0)),
                   pl.BlockSpec((256, bsz), lambda h: (h // (SSD_HEADS // 2), 0)),
                   _const_spec((bsz, SSD_CONV_DIM))],
        out_shape=[jax.ShapeDtypeStruct((SSD_HEADS, 64, 64, bsz), F32),
                   jax.ShapeDtypeStruct((SSD_W, bsz), F32),
                   jax.ShapeDtypeStruct((bsz, SSD_CONV_DIM), F32)],
        scratch_shapes=[pltpu.VMEM((SSD_CONV_DIM, bsz), F32), pltpu.VMEM((64, bsz), F32),
                        pltpu.VMEM((SSD_W, bsz), F32)],
        compiler_params=_cparams("arbitrary"),
        name="step_ssd",
    )(proj_t, buf_t, bias_col, alog_col, cw_t, dskip_col, ssdg_col, s_t)


def _step_gla_kernel(proj_ref, wf2t_ref, bf_ref, glag_ref, s_ref, s_out, y_ref, dec_s):
    h = pl.program_id(0)

    @pl.when(h == 0)
    def _():
        la = _log_sigmoid(_dot(wf2t_ref[...], proj_ref[GT:GT + LANE, :].astype(BF16)) + bf_ref[...]) / GLA_TAU
        dec_s[...] = jnp.exp(la)

    dec = _rows(dec_s, GLA_DK * h, GLA_DK)
    q = _rows(proj_ref, GQ + GLA_DK * h, GLA_DK) * (GLA_DK ** -0.5)
    k = _rows(proj_ref, GK + GLA_DK * h, GLA_DK)
    v = _rows(proj_ref, GV + 64 * h, 64)
    acc = jnp.zeros_like(v)
    for j in range(GLA_DK):
        sn = dec[j:j + 1, :] * s_ref[j] + k[j:j + 1, :] * v
        s_out[j] = sn
        acc = acc + q[j:j + 1, :] * sn
    gg = _rows(proj_ref, GG + 64 * h, 64)
    y_ref[...] = (acc * lax.rsqrt(jnp.mean(acc * acc, axis=0, keepdims=True) + EPS)
                  * _rows(glag_ref, 64 * h, 64) * _silu(gg))


def _step_gla(proj_t, layer, wf2_t, bf_col, glag_col, s_t):
    bsz = proj_t.shape[1]
    return pl.pallas_call(
        _step_gla_kernel,
        grid=(GLA_HEADS,),
        in_specs=[_const_spec((NP, bsz)), _const_spec((LANE, LANE)), _const_spec((LANE, 1)),
                  _const_spec((256, 1)),
                  pl.BlockSpec((None, None, GLA_DK, 64, bsz), lambda h: (layer, h, 0, 0, 0))],
        out_specs=[pl.BlockSpec((None, GLA_DK, 64, bsz), lambda h: (h, 0, 0, 0)),
                   pl.BlockSpec((64, bsz), lambda h: (h, 0))],
        out_shape=[jax.ShapeDtypeStruct((GLA_HEADS, GLA_DK, 64, bsz), F32),
                   jax.ShapeDtypeStruct((256, bsz), F32)],
        scratch_shapes=[pltpu.VMEM((LANE, bsz), F32)],
        compiler_params=_cparams("arbitrary"),
        name="step_gla",
    )(proj_t, wf2_t, bf_col, glag_col, s_t)


def _prep_w_in(w_in):
    parts, acc = [], 0
    for s in IN_SIZES:
        parts.append(w_in[..., acc:acc + s])
        acc += s
    mq, mk, mv, mo, mi, mf, sz, sxbc, sdt, gq, gk, gv, gg, gf = parts
    gates = jnp.concatenate([mi, mf, sdt, gf, jnp.zeros(w_in.shape[:-1] + (LANE - 32,), w_in.dtype)], axis=-1)
    return jnp.concatenate([mq, mk, mv, mo, sz, sxbc, gq, gk, gv, gg, gates], axis=-1).astype(BF16)


def _pad_row(x, width):
    return jnp.pad(x, [(0, 0), (0, width - x.shape[-1])])


def kernel(x_prompt, x_sample, state_mlstm_c, state_mlstm_n, state_mlstm_m, state_ssd, state_ssd_conv,
           state_gla, state_ffn_conv, norm1_g, w_in, mlstm_b_i, mlstm_b_f, mlstm_norm_g, ssd_conv_w,
           ssd_conv_b, ssd_dt_bias, ssd_a_log, ssd_d, ssd_norm_g, gla_w_f2, gla_b_f, gla_norm_g, w_out,
           norm2_g, w_up, ffn_conv_w, ffn_conv_b, w_down, final_norm_g):
    bsz, seq, _ = x_prompt.shape
    dbs = x_sample.shape[0]
    depth = w_in.shape[0]

    w_in_p = _prep_w_in(w_in)
    w_out_b = w_out.astype(BF16)
    w_up_p = w_up.astype(BF16)
    w_down_p = w_down.astype(BF16)
    ffn_cwb = jnp.concatenate([ffn_conv_w, ffn_conv_b[:, None, :],
                               jnp.zeros((depth, SUBLANE - FFN_CONV - 1, 2 * D_FF), F32)], axis=1)
    zeros8 = jnp.zeros((depth, 8), F32)
    bias_rows = _pad_row(jnp.concatenate([mlstm_b_i, mlstm_b_f, ssd_dt_bias], axis=-1), LANE)[:, None, :]
    alog_rows = _pad_row(jnp.concatenate([zeros8, ssd_a_log], axis=-1), LANE)[:, None, :]
    ssd_cwb = jnp.concatenate([ssd_conv_w, ssd_conv_b[:, None, :],
                               jnp.zeros((depth, SUBLANE - SSD_CONV - 1, SSD_CONV_DIM), F32)], axis=1)
    dskip_rows = jnp.repeat(ssd_d, 64, axis=-1)[:, None, :]
    wf2_p = jnp.pad(gla_w_f2, [(0, 0), (GATE_GF, LANE - GATE_GF - GLA_RANK), (0, 0)]).astype(BF16)

    c_t = state_mlstm_c.transpose(0, 2, 3, 4, 1)
    n_t = state_mlstm_n.transpose(0, 2, 3, 1)
    m_t = state_mlstm_m.transpose(0, 2, 1)
    s_t = state_ssd.transpose(0, 2, 3, 4, 1)
    g_t = state_gla.transpose(0, 2, 3, 4, 1)
    conv_t = state_ssd_conv.transpose(0, 2, 1, 3)

    xp = x_prompt.reshape(bsz * seq, D_MODEL)
    xs = x_sample.reshape(dbs, D_MODEL)
    fg = final_norm_g[None, :]
    p_states, s_states = [], []
    for i in range(depth):
        g1 = norm1_g[i][None, :]
        g2 = norm2_g[i][None, :]
        mlg = mlstm_norm_g[i][None, :]
        ssdg = ssd_norm_g[i][None, :]
        glag = gla_norm_g[i][None, :]
        bfr = gla_b_f[i][None, :]
        last = i == depth - 1

        proj = _norm_proj(xp, g1, w_in_p, i)
        xp, ffn_st, caug, m_o, ssd_o, gla_o = _mix_ffn(
            xp, proj, bsz, seq, i, bias_rows[i], alog_rows[i], mlg, ssd_cwb[i], dskip_rows[i], ssdg, wf2_p[i], bfr,
            glag, w_out_b, g2, w_up_p, ffn_cwb[i], w_down_p, fg, last)
        caug4 = caug.reshape(bsz, ML_HEADS, ML_DK, LANE)
        ssd4 = ssd_o.reshape(bsz, SSD_HEADS, 64, LANE)
        gla4 = gla_o.reshape(bsz, GLA_HEADS, GLA_DK, 256)
        p_states.append((
            caug4[..., :64], caug4[..., 64], m_o[:, 0, :ML_HEADS],
            jnp.where(jnp.arange(SSD_HEADS)[None, :, None, None] < 4, ssd4[..., :64], ssd4[..., 64:]),
            proj.reshape(bsz, seq, NP)[:, seq - (SSD_CONV - 1):, SXBC:SXBC + SSD_CONV_DIM],
            jnp.stack([gla4[:, h, :, 64 * h:64 * (h + 1)] for h in range(GLA_HEADS)], axis=1),
            ffn_st[:, SUBLANE - (FFN_CONV - 1):, :],
        ))

        proj_t = _norm_proj_t(xs, g1, w_in_p, i)
        c_n, n_n, m_n, y_ml = _step_mlstm(proj_t, i, bias_rows[i].T, mlg.T, c_t, n_t, m_t)
        s_n, y_ssd, raw_s = _step_ssd(proj_t, i, conv_t, bias_rows[i].T, alog_rows[i].T, ssd_cwb[i].T,
                                      dskip_rows[i].T, ssdg.T, s_t)
        g_n, y_gla = _step_gla(proj_t, i, wf2_p[i].T, bfr.T, glag.T, g_t)
        mix_t = jnp.concatenate([y_ml, y_ssd, y_gla], axis=0)
        prev = state_ffn_conv[i].transpose(1, 0, 2)
        xs, up_s = _out_ffn_step(xs, mix_t, i, w_out_b, g2, w_up_p, ffn_cwb[i], w_down_p, fg, last, prev)
        s_states.append((
            c_n, n_n, m_n, s_n,
            jnp.concatenate([state_ssd_conv[i][:, 1:], raw_s[:, None, :]], axis=1),
            g_n,
            jnp.concatenate([state_ffn_conv[i][:, 1:], up_s[:, None, :]], axis=1),
        ))

    y_prompt = xp.reshape(bsz, seq, D_MODEL)
    y_sample = xs.reshape(dbs, 1, D_MODEL)

    def stk(sts, j):
        return jnp.stack([s[j] for s in sts], axis=0)

    s_perm = {0: (0, 4, 1, 2, 3), 1: (0, 3, 1, 2), 2: (0, 2, 1), 3: (0, 4, 1, 2, 3), 5: (0, 4, 1, 2, 3)}
    s_out = tuple(stk(s_states, j).transpose(s_perm[j]) if j in s_perm else stk(s_states, j) for j in range(7))
    return (y_prompt, y_sample) + tuple(stk(p_states, j) for j in range(7)) + s_out
```

```python
import functools

import jax
import jax.numpy as jnp
from jax import lax
from jax.experimental import pallas as pl
from jax.experimental.pallas import tpu as pltpu

F32 = jnp.float32
BF16 = jnp.bfloat16
I32 = jnp.int32

D_MODEL = 1024
DEPTH = 4
ML_HEADS = 4
ML_DK = 64
SSD_HEADS = 8
SSD_W = 512
SSD_CONV = 4
SSD_CONV_DIM = 768
GLA_HEADS = 4
GLA_DK = 32
GLA_RANK = 16
GLA_TAU = 16.0
D_FF = 2816
FFN_CONV = 3
EPS = 1e-6
IN_SIZES = (256, 256, 256, 256, 4, 4, 512, 768, 8, 128, 128, 256, 256, 16)

MQ, MK, MV, MO, SZ, SXBC, GQ, GK, GV, GG, GT, NP = (
    0, 256, 512, 768, 1024, 1536, 2304, 2432, 2560, 2816, 3072, 3200)
GATE_LI, GATE_LF, GATE_DT, GATE_GF = 0, 4, 8, 16

LANE = 128
SUBLANE = 8
MIX_T = 128
GLA_C = 8
FFN_TM = 512
FFN_ROWS = 1
FF_CHUNK = 256
N_FF_CHUNK = D_FF // FF_CHUNK
VMEM_LIMIT = 56 * 1024 * 1024
VMEM_LIMIT_FUSED = 62 * 1024 * 1024


def _cparams(*sem, vmem=VMEM_LIMIT):
    return pltpu.CompilerParams(dimension_semantics=sem if sem else None, vmem_limit_bytes=vmem)


def _const_spec(shape):
    nd = len(shape)
    return pl.BlockSpec(shape, lambda *_: (0,) * nd)


def _layer_spec(shape, layer, resident=False):
    nd = len(shape)
    return pl.BlockSpec((None,) + tuple(shape), lambda *_: (layer,) + (0,) * nd,
                        pipeline_mode=pl.Buffered(1) if resident else None)


def _sigmoid(x):
    return 1.0 / (1.0 + jnp.exp(-x))


def _silu(x):
    h = 0.5 * x
    return h + h * jnp.tanh(h)


def _softplus(x):
    return jnp.maximum(x, 0.0) + jnp.log(1.0 + jnp.exp(-jnp.abs(x)))


def _log_sigmoid(x):
    return -_softplus(-x)


def _rmsnorm_rows(x, g):
    ms = jnp.mean(x * x, axis=-1, keepdims=True)
    return x * lax.rsqrt(ms + EPS) * g


def _dot(a, b):
    return jnp.dot(a, b, preferred_element_type=F32)


def _dot_nt(a, b):
    return lax.dot_general(a, b, (((1,), (1,)), ((), ())), preferred_element_type=F32)


def _split3(x):
    x1 = x.astype(BF16)
    r1 = x - x1.astype(F32)
    x2 = r1.astype(BF16)
    x3 = (r1 - x2.astype(F32)).astype(BF16)
    return x1, x2, x3


def _cumsum_rows(x, tri):
    c = _dot(tri, jnp.concatenate(_split3(x), axis=1))
    return c[:, :LANE] + c[:, LANE:2 * LANE] + c[:, 2 * LANE:]


def _shift_rows(x, k, carry):
    r = pltpu.roll(x, k, 0)
    c = pltpu.roll(carry, k, 0)
    row = lax.broadcasted_iota(I32, c.shape, 0)
    head = jnp.where(row < k, c, r[:SUBLANE])
    return jnp.concatenate([head, r[SUBLANE:]], axis=0)


def _seg_rmsnorm(x, seg):
    w = x.shape[1]
    if seg % LANE == 0:
        parts = []
        for s0 in range(0, w, seg):
            xs = x[:, s0:s0 + seg]
            parts.append(xs * lax.rsqrt(jnp.mean(xs * xs, axis=-1, keepdims=True) + EPS))
        return jnp.concatenate(parts, axis=1)
    parts = []
    lane = lax.broadcasted_iota(I32, (x.shape[0], LANE), 1)
    for s0 in range(0, w, LANE):
        xs = x[:, s0:s0 + LANE]
        sq = xs * xs
        lo = jnp.sum(jnp.where(lane < seg, sq, 0.0), axis=-1, keepdims=True)
        hi = jnp.sum(jnp.where(lane >= seg, sq, 0.0), axis=-1, keepdims=True)
        r = jnp.where(lane < seg, lax.rsqrt(lo / seg + EPS), lax.rsqrt(hi / seg + EPS))
        parts.append(xs * r)
    return jnp.concatenate(parts, axis=1)


def _drive(chains):
    chains = list(chains)
    while chains:
        for c in list(chains):
            try:
                next(c)
            except StopIteration:
                chains.remove(c)


def _in_segments():
    dst = (MQ, MK, MV, MO, GT + GATE_LI, GT + GATE_LF, SZ, SXBC, GT + GATE_DT, GQ, GK, GV, GG, GT + GATE_GF)
    segs, src = [], 0
    for width, d in zip(IN_SIZES, dst):
        segs.append((src, d, width))
        src += width
    return segs


def _norm_proj_kernel(x_ref, g_ref, w_ref, o_ref, wp_ref):
    @pl.when(pl.program_id(0) == 0)
    def _():
        wp_ref[:, GT:GT + LANE] = jnp.zeros((D_MODEL, LANE), BF16)
        for src, dst, width in _in_segments():
            wp_ref[:, dst:dst + width] = w_ref[:, src:src + width].astype(BF16)

    h = _rmsnorm_rows(x_ref[...], g_ref[...]).astype(BF16)
    for n0 in range(0, NP, 512):
        n1 = min(n0 + 512, NP)
        o_ref[:, n0:n1] = _dot(h, wp_ref[:, n0:n1])


def _norm_proj(x2d, g, w_all, layer):
    m = x2d.shape[0]
    tm = min(512, m)
    n_in = w_all.shape[-1]
    return pl.pallas_call(
        _norm_proj_kernel,
        grid=(m // tm,),
        in_specs=[pl.BlockSpec((tm, D_MODEL), lambda i: (i, 0)),
                  _const_spec((1, D_MODEL)),
                  _layer_spec((D_MODEL, n_in), layer, resident=True)],
        out_specs=[pl.BlockSpec((tm, NP), lambda i: (i, 0)), _const_spec((D_MODEL, NP))],
        out_shape=[jax.ShapeDtypeStruct((m, NP), F32), jax.ShapeDtypeStruct((D_MODEL, NP), BF16)],
        compiler_params=_cparams("arbitrary"),
        name="norm_proj",
    )(x2d, g, w_all)


class _MixerRow:
    def __init__(self, proj_ref, bias_ref, alog_ref, mlg_ref, cw_ref, dskip_ref, ssdg_ref, wf2_ref,
                 bf_ref, glag_ref, mix_ref, caug_ref, m_ref, ssd_ref, gla_ref, conv_s, cum_s, live):
        self.proj_ref, self.bias_ref, self.alog_ref, self.mlg_ref = proj_ref, bias_ref, alog_ref, mlg_ref
        self.cw_ref, self.dskip_ref, self.ssdg_ref, self.wf2_ref = cw_ref, dskip_ref, ssdg_ref, wf2_ref
        self.bf_ref, self.glag_ref, self.mix_ref, self.caug_ref = bf_ref, glag_ref, mix_ref, caug_ref
        self.m_ref, self.ssd_ref, self.gla_ref, self.conv_s, self.cum_s = m_ref, ssd_ref, gla_ref, conv_s, cum_s
        self.live = live
        self.out = {}
        self.levels = {}

    def keep(self, new, old):
        return jnp.where(self.live, new, old)

    def setup(self):
        T = MIX_T
        proj_ref = self.proj_ref
        self.lane = lane = lax.broadcasted_iota(I32, (T, LANE), 1)
        self.rowi = lax.broadcasted_iota(I32, (T, T), 0)
        self.coli = lax.broadcasted_iota(I32, (T, T), 1)
        self.causal = self.rowi >= self.coli
        tri = jnp.where(self.causal, 1.0, 0.0).astype(BF16)
        self.row128 = lax.broadcasted_iota(I32, (LANE, LANE), 0)
        self.g_blk = g_blk = proj_ref[:, GT:GT + LANE] + self.bias_ref[...]
        is_lf = (lane >= GATE_LF) & (lane < GATE_DT)
        is_dt = (lane >= GATE_DT) & (lane < GATE_GF)
        self.dt = dt = _softplus(jnp.where(is_lf, -g_blk, g_blk))
        a_row = -jnp.exp(self.alog_ref[...])
        self.cum_g = _cumsum_rows(jnp.where(is_lf, -dt, jnp.where(is_dt, dt * a_row, 0.0)), tri)
        yield
        self.p1_t = jnp.where(lane < GATE_LF, g_blk, self.cum_g).T
        self.dt_t = dt.T
        self.k_ml_t = [proj_ref[:, MK + LANE * p:MK + LANE * (p + 1)].T for p in range(2)]
        raw = proj_ref[:, SXBC:SXBC + SSD_CONV_DIM]
        carry = self.conv_s[...]
        cw_ref = self.cw_ref
        conv = (cw_ref[4:5, :] + _shift_rows(raw, 3, carry) * cw_ref[0:1, :]
                + _shift_rows(raw, 2, carry) * cw_ref[1:2, :]
                + _shift_rows(raw, 1, carry) * cw_ref[2:3, :] + raw * cw_ref[3:4, :])
        self.conv_s[...] = self.keep(raw[T - SUBLANE:T, :], carry)
        self.xbc = xbc = _silu(conv)
        self.x_ssd_t = [xbc[:, LANE * j:LANE * (j + 1)].T for j in range(SSD_HEADS // 2)]
        self.b_p = b_p = xbc[:, SSD_W:SSD_W + LANE]
        c_p = xbc[:, SSD_W + LANE:SSD_W + 2 * LANE]
        self.c_pb = c_p.astype(BF16)
        self.cb = []
        for g in range(2):
            in_g = (lane >= 64 * g) & (lane < 64 * (g + 1))
            self.cb.append(_dot_nt(jnp.where(in_g, c_p, 0.0).astype(BF16), b_p.astype(BF16)))
            yield
        la = _log_sigmoid(_dot(g_blk.astype(BF16), self.wf2_ref[...]) + self.bf_ref[...]) / GLA_TAU
        yield
        self.cum = cum = _cumsum_rows(la, tri)
        yield
        self.cum_s[...] = cum
        self.cum_t = cum.T
        self.q = proj_ref[:, GQ:GQ + LANE] * (GLA_DK ** -0.5)
        self.k = proj_ref[:, GK:GK + LANE]
        self.k_gla_t = self.k.T
        self.v = proj_ref[:, GV:GV + 256]
        r_s = lax.broadcasted_iota(I32, (LANE, 256), 0)
        c_s = lax.broadcasted_iota(I32, (LANE, 256), 1)
        self.same_head = (r_s >> 5) == (c_s >> 6)
        self.tpos = lax.broadcasted_iota(I32, (T, LANE), 0)

    def chains(self):
        return ([self.ml_head(h) for h in range(ML_HEADS)] + [self.ssd_pair(j) for j in range(SSD_HEADS // 2)]
                + [self.gla_state(), self.gla_diag()] + [self.gla_head(h) for h in range(GLA_HEADS)])

    def ml_head(self, h):
        T = MIX_T
        proj_ref, lane, cum_g, p1_t = self.proj_ref, self.lane, self.cum_g, self.p1_t
        pair, lo = h // 2, 64 * (h % 2)
        hm = (lane >= lo) & (lane < lo + 64)
        qm = jnp.where(hm, proj_ref[:, MQ + LANE * pair:MQ + LANE * (pair + 1)] * 0.125, 0.0).astype(BF16)
        km = jnp.where(hm, proj_ref[:, MK + LANE * pair:MK + LANE * (pair + 1)], 0.0)
        qk = _dot_nt(qm, km.astype(BF16))
        yield
        c_pair = self.caug_ref[LANE * pair:LANE * (pair + 1), :]
        qc = _dot(qm, c_pair.astype(BF16))
        yield
        v_pair = proj_ref[:, MV + LANE * pair:MV + LANE * (pair + 1)]
        v_lo = pltpu.roll(v_pair, 64, 1) if h % 2 else v_pair
        vaug = jnp.where(lane < 64, v_lo, jnp.where(lane == 64, 1.0, 0.0)).astype(BF16)
        b_col = cum_g[:, GATE_LF + h:GATE_LF + h + 1]
        b_row = p1_t[GATE_LF + h:GATE_LF + h + 1, :]
        li_row = p1_t[GATE_LI + h:GATE_LI + h + 1, :]
        m_prev = self.m_ref[:, h:h + 1]
        d_mat = jnp.where(self.causal, b_col - b_row + li_row, -jnp.inf)
        yield
        row_max = jnp.max(d_mat, axis=1, keepdims=True)
        yield
        a_col = b_col + m_prev
        mt = jnp.maximum(a_col, row_max)
        s_mat = (qk * jnp.exp(d_mat - mt)).astype(BF16)
        numaug = _dot(s_mat, vaug) + jnp.exp(a_col - mt) * qc
        yield
        den = numaug[:, 64:65]
        hn = numaug / jnp.maximum(jnp.abs(den), jnp.exp(-mt))
        hv = jnp.where(lane < 64, hn, 0.0)
        yield
        self.out["ml", h] = hv * lax.rsqrt(jnp.sum(hv * hv, axis=-1, keepdims=True) / 64.0 + EPS)
        b_last = b_col[T - 1:T, :]
        m_new = mt[T - 1:T, :]
        g_state = jnp.exp(b_last + m_prev - m_new)
        g_s_row = jnp.exp(b_last - b_row + li_row - m_new)
        u = _dot((self.k_ml_t[pair][lo:lo + 64, :] * g_s_row).astype(BF16), vaug)
        yield
        c_old = c_pair[lo:lo + 64]
        self.caug_ref[LANE * pair + lo:LANE * pair + lo + 64, :] = self.keep(g_state * c_old + u, c_old)
        self.out["m", h] = m_new

    def ssd_pair(self, j):
        T = MIX_T
        lane, cum_g, p1_t = self.lane, self.cum_g, self.p1_t
        g = j // 2
        x_pair = self.xbc[:, LANE * j:LANE * (j + 1)]
        s_pair = self.ssd_ref[LANE * j:LANE * (j + 1), :]
        inter = _dot_nt(self.c_pb, s_pair.astype(BF16))
        yield
        y_pair = None
        ecs, wrow, dec = [], [], []
        for hh in range(2):
            h = 2 * j + hh
            cs_col = cum_g[:, GATE_DT + h:GATE_DT + h + 1]
            cs_row = p1_t[GATE_DT + h:GATE_DT + h + 1, :]
            dt_row = self.dt_t[GATE_DT + h:GATE_DT + h + 1, :]
            seg = jnp.where(self.causal, cs_col - cs_row, -jnp.inf)
            yield
            sc = (self.cb[g] * jnp.exp(seg) * dt_row).astype(BF16)
            in_h = (lane >= 64 * hh) & (lane < 64 * (hh + 1))
            part = _dot(sc, jnp.where(in_h, x_pair, 0.0).astype(BF16))
            yield
            y_pair = part if y_pair is None else y_pair + part
            cs_last = cs_col[T - 1:T, :]
            ecs.append(jnp.exp(cs_col))
            wrow.append(jnp.exp(cs_last - cs_row) * dt_row)
            dec.append(jnp.exp(cs_last))
        self.out["ssd", j] = y_pair + jnp.where(lane < 64, ecs[0], ecs[1]) * inter
        wx_t = (self.x_ssd_t[j] * jnp.where(self.row128 < 64, wrow[0], wrow[1])).astype(BF16)
        in_g = (lane >= 64 * g) & (lane < 64 * (g + 1))
        u = _dot(wx_t, jnp.where(in_g, self.b_p, 0.0).astype(BF16))
        yield
        self.ssd_ref[LANE * j:LANE * (j + 1), :] = self.keep(
            jnp.where(self.row128 < 64, dec[0], dec[1]) * s_pair + u, s_pair)

    def gla_state(self):
        T = MIX_T
        cum = self.cum
        s_gla = self.gla_ref[...]
        self.out["gla_inter"] = _dot((self.q * jnp.exp(cum)).astype(BF16), s_gla.astype(BF16))
        yield
        cum_t = self.cum_t
        u = _dot((self.k_gla_t * jnp.exp(cum_t[:, T - 1:T] - cum_t)).astype(BF16), self.v.astype(BF16))
        yield
        self.gla_ref[...] = self.keep(jnp.exp(self.cum_t[:, T - 1:T]) * s_gla + jnp.where(self.same_head, u, 0.0), s_gla)

    def gla_level(self, w):
        if w not in self.levels:
            T = MIX_T
            is_t = (self.tpos & (2 * w - 1)) >= w
            mid = jnp.concatenate(
                [jnp.broadcast_to(self.cum_s[2 * w * blk + w - 1:2 * w * blk + w, :], (2 * w, LANE))
                 for blk in range(T // (2 * w))], axis=0)
            e = jnp.exp(jnp.where(is_t, self.cum - mid, mid - self.cum))
            ql = jnp.where(is_t, self.q * e, 0.0)
            kl = jnp.where(is_t, 0.0, self.k * e).astype(BF16)
            shift = (2 * w).bit_length() - 1
            self.levels[w] = (ql, kl, (self.rowi >> shift) == (self.coli >> shift))
        return self.levels[w]

    def gla_head(self, h):
        lane = self.lane
        in_h = (lane >= GLA_DK * h) & (lane < GLA_DK * (h + 1))
        att = None
        w = GLA_C
        while w < MIX_T:
            ql, kl, same_blk = self.gla_level(w)
            a = jnp.where(same_blk, _dot_nt(jnp.where(in_h, ql, 0.0).astype(BF16), kl), 0.0)
            yield
            att = a if att is None else att + a
            w *= 2
        v_pair = self.v[:, LANE * (h // 2):LANE * (h // 2 + 1)]
        in_half = (lane >= 64) if h % 2 else (lane < 64)
        self.out["gla_o", h] = _dot(att.astype(BF16), jnp.where(in_half, v_pair, 0.0).astype(BF16))
        yield

    def gla_diag(self):
        expand = jnp.where(self.same_head, 1.0, 0.0).astype(BF16)
        nblk = MIX_T // GLA_C
        o = None
        for jj in range(GLA_C):
            def rows(ref, c0, c1):
                return jnp.concatenate(
                    [jnp.broadcast_to(ref[GLA_C * i + jj:GLA_C * i + jj + 1, c0:c1], (GLA_C, c1 - c0))
                     for i in range(nblk)], axis=0)
            k_s = rows(self.proj_ref, GK, GK + LANE)
            c_srow = rows(self.cum_s, 0, LANE)
            v_s = rows(self.proj_ref, GV, GV + 256)
            valid = (self.tpos & (GLA_C - 1)) >= jj
            e = jnp.exp(jnp.where(valid, self.cum - c_srow, -jnp.inf))
            part = _dot((self.q * k_s * e).astype(BF16), expand) * v_s
            yield
            o = part if o is None else o + part
        self.out["gla_diag"] = o

    def finish(self):
        proj_ref, mix_ref, lane, out = self.proj_ref, self.mix_ref, self.lane, self.out
        m_row = self.m_ref[...]
        m_lane = lax.broadcasted_iota(I32, m_row.shape, 1)
        for h in range(ML_HEADS):
            m_row = jnp.where((m_lane == h) & self.live, out["m", h], m_row)
        self.m_ref[...] = m_row
        y_ml = jnp.concatenate(
            [jnp.where(lane < 64, out["ml", 2 * p], pltpu.roll(out["ml", 2 * p + 1], 64, 1)) for p in range(2)],
            axis=1)
        mix_ref[:, 0:256] = (_sigmoid(proj_ref[:, MO:MO + 256]) * (y_ml * self.mlg_ref[...])).astype(BF16)
        y_s = (jnp.concatenate([out["ssd", j] for j in range(SSD_HEADS // 2)], axis=1)
               + self.dskip_ref[...] * self.xbc[:, :SSD_W])
        z = proj_ref[:, SZ:SZ + SSD_W]
        mix_ref[:, 256:768] = (_seg_rmsnorm(y_s * _silu(z), 256) * self.ssdg_ref[...]).astype(BF16)
        o = (out["gla_inter"] + out["gla_diag"]
             + jnp.concatenate([out["gla_o", 2 * p] + out["gla_o", 2 * p + 1] for p in range(2)], axis=1))
        gg = proj_ref[:, GG:GG + 256]
        mix_ref[:, 768:1024] = (_seg_rmsnorm(o, 64) * self.glag_ref[...] * _silu(gg)).astype(BF16)


def _ffn_row(x_ref, mix_fn, wo_ref, g2_ref, wup_ref, cwb_ref, wd_ref, fg_ref, o_ref, h2_s, act_s,
             prev_fn, keep_fn, final):
    xn = x_ref[...] + _dot(mix_fn(), wo_ref[...])
    yield
    o_ref[...] = xn
    h2_s[...] = _rmsnorm_rows(xn, g2_ref[...]).astype(BF16)
    for c in range(N_FF_CHUNK):
        h2 = h2_s[...]
        halves = []
        for ug in range(2):
            col = slice(ug * D_FF + c * FF_CHUNK, ug * D_FF + (c + 1) * FF_CHUNK)
            up = _dot(h2, wup_ref[:, col])
            yield
            x1, x2 = prev_fn(col, up)
            keep_fn(col, up)
            halves.append(cwb_ref[3:4, col] + x2 * cwb_ref[0:1, col] + x1 * cwb_ref[1:2, col]
                          + up * cwb_ref[2:3, col])
        u, gate = halves
        act_s[:, c * FF_CHUNK:(c + 1) * FF_CHUNK] = (_silu(gate) * u).astype(BF16)
    y = o_ref[...] + _dot(act_s[...], wd_ref[...])
    yield
    o_ref[...] = _rmsnorm_rows(y, fg_ref[...]) if final else y


def _out_ffn_step_kernel(x_ref, mix_ref, wo_ref, g2_ref, wup_ref, cwb_ref, wd_ref, fg_ref, prev_ref, o_ref, up_ref,
                         h2_s, act_s, *, final):
    def mix_fn():
        mix = jnp.concatenate([mix_ref[c0:c0 + LANE, :].T for c0 in range(0, D_MODEL, LANE)], axis=1)
        return mix.astype(BF16)

    def prev_fn(col, up):
        return prev_ref[1, :, col], prev_ref[0, :, col]

    def keep_fn(col, up):
        up_ref[:, col] = up

    _drive([_ffn_row(x_ref, mix_fn, wo_ref, g2_ref, wup_ref, cwb_ref, wd_ref, fg_ref, o_ref, h2_s, act_s,
                     prev_fn, keep_fn, final)])


def _ffn_w_specs(layer):
    return [_layer_spec((D_MODEL, D_MODEL), layer, resident=True), _const_spec((1, D_MODEL)),
            _layer_spec((D_MODEL, 2 * D_FF), layer, resident=True),
            _const_spec((SUBLANE, 2 * D_FF)),
            _layer_spec((D_FF, D_MODEL), layer, resident=True), _const_spec((1, D_MODEL))]


def _rounds(chains, per_yield=1):
    chains = list(chains)
    n = 0
    while chains:
        for c in list(chains):
            try:
                next(c)
            except StopIteration:
                chains.remove(c)
        n += 1
        if n % per_yield == 0:
            yield


def _mix_ffn_kernel(proj_ref, x_ref, bias_ref, alog_ref, mlg_ref, cw_ref, dskip_ref, ssdg_ref, wf2_ref, bf_ref,
                    glag_ref, wo_ref, g2_ref, wup_ref, cwb_ref, wd_ref, fg_ref,
                    o_ref, st_ref, caug_ref, m_ref, ssd_ref, gla_ref,
                    conv_s, cum_s, mix_s, h2_s, act_s, *, n_steps, nt, final):
    g = pl.program_id(0)
    live = g < n_steps
    t_m = jnp.minimum(g, n_steps - 1) % nt
    t_f = jnp.maximum(g - 1, 0) % nt
    slot = g % 2

    @pl.when((t_m == 0) & live)
    def _():
        caug_ref[...] = jnp.zeros_like(caug_ref)
        m_ref[...] = jnp.zeros_like(m_ref)
        ssd_ref[...] = jnp.zeros_like(ssd_ref)
        gla_ref[...] = jnp.zeros_like(gla_ref)
        conv_s[...] = jnp.zeros_like(conv_s)

    @pl.when(t_f == 0)
    def _():
        st_ref[...] = jnp.zeros_like(st_ref)

    @pl.when(g == 0)
    def _():
        mix_s[...] = jnp.zeros_like(mix_s)

    n_rows, tm = x_ref.shape[0], x_ref.shape[1]

    def ffn_chain(r):
        st = st_ref.at[r]

        def prev_fn(col, up):
            carry = st[:, col]
            return _shift_rows(up, 1, carry), _shift_rows(up, 2, carry)

        def keep_fn(col, up):
            st[:, col] = up[tm - SUBLANE:tm, :]

        return _ffn_row(x_ref.at[r], lambda: mix_s[1 - slot, r], wo_ref, g2_ref, wup_ref, cwb_ref, wd_ref, fg_ref,
                        o_ref.at[r], h2_s.at[r], act_s.at[r], prev_fn, keep_fn, final)

    def mixer_program():
        for k in range(tm // MIX_T):
            tok = pl.ds(k * MIX_T, MIX_T)
            for r in range(n_rows):
                row = _MixerRow(proj_ref.at[r, tok], bias_ref, alog_ref, mlg_ref, cw_ref, dskip_ref, ssdg_ref,
                                wf2_ref, bf_ref, glag_ref, mix_s.at[slot, r, tok], caug_ref.at[r], m_ref.at[r],
                                ssd_ref.at[r], gla_ref.at[r], conv_s.at[r], cum_s.at[r], live)
                yield from _rounds([row.setup()], 2)
                yield from _rounds(row.chains(), 2)
                row.finish()

    _drive([ffn_chain(r) for r in range(n_rows)] + [mixer_program()])


def _mix_ffn(x2d, proj, bsz, seq, layer, bias_row, alog_row, mlg, cwb_ssd, dskip, ssdg, wf2p, bfr, glag,
             wo, g2, wup, cwb, wd, fg, final):
    tm = min(FFN_TM, seq)
    nt = seq // tm
    rows = FFN_ROWS if bsz % FFN_ROWS == 0 else 1
    n_steps = (bsz // rows) * nt
    row_spec = lambda w: _const_spec((1, w))
    mix_idx = lambda g: (jnp.minimum(g, n_steps - 1) // nt, jnp.minimum(g, n_steps - 1) % nt, 0)
    ffn_idx = lambda g: (jnp.maximum(g - 1, 0) // nt, jnp.maximum(g - 1, 0) % nt, 0)
    mix_pair = lambda g: (jnp.minimum(g, n_steps - 1) // nt, 0, 0)
    ffn_pair = lambda g: (jnp.maximum(g - 1, 0) // nt, 0, 0)
    out, st, caug, m_o, ssd_o, gla_o = pl.pallas_call(
        functools.partial(_mix_ffn_kernel, n_steps=n_steps, nt=nt, final=final),
        grid=(n_steps + 1,),
        in_specs=[pl.BlockSpec((rows, tm, NP), mix_idx), pl.BlockSpec((rows, tm, D_MODEL), ffn_idx),
                  row_spec(LANE), row_spec(LANE), row_spec(256),
                  _const_spec((SUBLANE, SSD_CONV_DIM)), row_spec(SSD_W), row_spec(SSD_W),
                  _const_spec((LANE, LANE)), row_spec(LANE), row_spec(256)] + _ffn_w_specs(layer),
        out_specs=[pl.BlockSpec((rows, tm, D_MODEL), ffn_idx),
                   pl.BlockSpec((rows, SUBLANE, 2 * D_FF), ffn_pair),
                   pl.BlockSpec((rows, 256, LANE), mix_pair),
                   pl.BlockSpec((rows, 1, LANE), mix_pair),
                   pl.BlockSpec((rows, 512, LANE), mix_pair),
                   pl.BlockSpec((rows, LANE, 256), mix_pair)],
        out_shape=[jax.ShapeDtypeStruct((bsz, seq, D_MODEL), F32),
                   jax.ShapeDtypeStruct((bsz, SUBLANE, 2 * D_FF), F32),
                   jax.ShapeDtypeStruct((bsz, 256, LANE), F32),
                   jax.ShapeDtypeStruct((bsz, 1, LANE), F32),
                   jax.ShapeDtypeStruct((bsz, 512, LANE), F32),
                   jax.ShapeDtypeStruct((bsz, LANE, 256), F32)],
        scratch_shapes=[pltpu.VMEM((rows, SUBLANE, SSD_CONV_DIM), F32), pltpu.VMEM((rows, MIX_T, LANE), F32),
                        pltpu.VMEM((2, rows, tm, D_MODEL), BF16),
                        pltpu.VMEM((rows, tm, D_MODEL), BF16), pltpu.VMEM((rows, tm, D_FF), BF16)],
        compiler_params=_cparams("arbitrary", vmem=VMEM_LIMIT_FUSED),
        name="mix_ffn",
    )(proj.reshape(bsz, seq, NP), x2d.reshape(bsz, seq, D_MODEL), bias_row, alog_row, mlg, cwb_ssd, dskip, ssdg,
      wf2p, bfr, glag, wo, g2, wup, cwb, wd, fg)
    return out.reshape(bsz * seq, D_MODEL), st, caug, m_o, ssd_o, gla_o


def _out_ffn_step(x2d, mix, layer, wo, g2, wup, cwb, wd, fg, final, prev):
    m = x2d.shape[0]
    return pl.pallas_call(
        functools.partial(_out_ffn_step_kernel, final=final),
        grid=(1,),
        in_specs=[_const_spec((m, D_MODEL)), _const_spec((D_MODEL, m))] + _ffn_w_specs(layer)
        + [_const_spec((FFN_CONV - 1, m, 2 * D_FF))],
        out_specs=[_const_spec((m, D_MODEL)), _const_spec((m, 2 * D_FF))],
        out_shape=[jax.ShapeDtypeStruct((m, D_MODEL), F32),
                   jax.ShapeDtypeStruct((m, 2 * D_FF), F32)],
        scratch_shapes=[pltpu.VMEM((m, D_MODEL), BF16), pltpu.VMEM((m, D_FF), BF16)],
        compiler_params=_cparams("arbitrary"),
        name="out_ffn_step",
    )(x2d, mix, wo, g2, wup, cwb, wd, fg, prev)


def _norm_proj_t_kernel(x_ref, g_ref, w_ref, o_ref):
    h = _rmsnorm_rows(x_ref[...], g_ref[...]).astype(BF16)
    for n0 in range(0, NP, 512):
        n1 = min(n0 + 512, NP)
        blk = _dot(h, w_ref[:, n0:n1])
        for c0 in range(0, n1 - n0, LANE):
            o_ref[n0 + c0:n0 + c0 + LANE, :] = blk[:, c0:c0 + LANE].T


def _norm_proj_t(x2d, g, w_perm):
    m = x2d.shape[0]
    return pl.pallas_call(
        _norm_proj_t_kernel,
        grid=(1,),
        in_specs=[_const_spec((m, D_MODEL)), _const_spec((1, D_MODEL)), _const_spec((D_MODEL, NP))],
        out_specs=_const_spec((NP, m)),
        out_shape=jax.ShapeDtypeStruct((NP, m), F32),
        compiler_params=_cparams("arbitrary"),
        name="norm_proj_t",
    )(x2d, g, w_perm)


def _pick_row(x, r):
    row = lax.broadcasted_iota(I32, x.shape, 0)
    return jnp.sum(jnp.where(row == r, x, 0.0), axis=0, keepdims=True)


def _rows(ref, start, size):
    return ref[pl.ds(pl.multiple_of(start, size), size), :]


def _step_mlstm_kernel(proj_ref, bias_ref, mlg_ref, c_ref, n_ref, m_ref, c_out, n_out, m_out, y_ref):
    h = pl.program_id(0)
    gates = proj_ref[GT:GT + LANE, :] + bias_ref[...]
    li = _pick_row(gates, GATE_LI + h)
    lf = _log_sigmoid(_pick_row(gates, GATE_LF + h))
    m0 = m_ref[pl.ds(h, 1), :]
    mt = jnp.maximum(lf + m0, li)
    w_old = jnp.exp(lf + m0 - mt)
    q = _rows(proj_ref, MQ + ML_DK * h, ML_DK) * 0.125
    kw = _rows(proj_ref, MK + ML_DK * h, ML_DK) * jnp.exp(li - mt)
    v = _rows(proj_ref, MV + 64 * h, 64)
    acc = jnp.zeros_like(v)
    for d in range(ML_DK):
        cn = w_old * c_ref[d] + kw[d:d + 1, :] * v
        c_out[d] = cn
        acc = acc + q[d:d + 1, :] * cn
    nn = w_old * n_ref[...] + kw
    n_out[...] = nn
    den = jnp.sum(q * nn, axis=0, keepdims=True)
    hn = acc / jnp.maximum(jnp.abs(den), jnp.exp(-mt))
    yn = hn * lax.rsqrt(jnp.mean(hn * hn, axis=0, keepdims=True) + EPS)
    y_ref[...] = _sigmoid(_rows(proj_ref, MO + 64 * h, 64)) * (yn * _rows(mlg_ref, 64 * h, 64))
    m_out[pl.ds(h, 1), :] = mt


def _step_mlstm(proj_t, layer, bias_col, mlg_col, c_t, n_t, m_t):
    bsz = proj_t.shape[1]
    return pl.pallas_call(
        _step_mlstm_kernel,
        grid=(ML_HEADS,),
        in_specs=[_const_spec((NP, bsz)), _const_spec((LANE, 1)), _const_spec((256, 1)),
                  pl.BlockSpec((None, None, ML_DK, 64, bsz), lambda h: (layer, h, 0, 0, 0)),
                  pl.BlockSpec((None, None, ML_DK, bsz), lambda h: (layer, h, 0, 0)),
                  pl.BlockSpec((None, ML_HEADS, bsz), lambda h: (layer, 0, 0))],
        out_specs=[pl.BlockSpec((None, ML_DK, 64, bsz), lambda h: (h, 0, 0, 0)),
                   pl.BlockSpec((None, ML_DK, bsz), lambda h: (h, 0, 0)),
                   _const_spec((ML_HEADS, bsz)),
                   pl.BlockSpec((64, bsz), lambda h: (h, 0))],
        out_shape=[jax.ShapeDtypeStruct((ML_HEADS, ML_DK, 64, bsz), F32),
                   jax.ShapeDtypeStruct((ML_HEADS, ML_DK, bsz), F32),
                   jax.ShapeDtypeStruct((ML_HEADS, bsz), F32),
                   jax.ShapeDtypeStruct((256, bsz), F32)],
        compiler_params=_cparams("arbitrary"),
        name="step_mlstm",
    )(proj_t, bias_col, mlg_col, c_t, n_t, m_t)


def _step_ssd_kernel(proj_ref, buf_ref, bias_ref, alog_ref, cw_ref, dskip_ref, ssdg_ref, s_ref,
                     s_out, y_ref, raw_out, xbc_s, yh_s, ys_s):
    h = pl.program_id(0)
    g = h // (SSD_HEADS // 2)

    @pl.when(h == 0)
    def _():
        for c0 in range(0, SSD_CONV_DIM, LANE):
            raw_t = proj_ref[SXBC + c0:SXBC + c0 + LANE, :]
            cw = cw_ref[c0:c0 + LANE, :]
            acc = cw[:, 4:5] + raw_t * cw[:, 3:4]
            for j in range(SSD_CONV - 1):
                acc = acc + buf_ref[j, :, c0:c0 + LANE].T * cw[:, j:j + 1]
            xbc_s[c0:c0 + LANE, :] = _silu(acc)
            raw_out[:, c0:c0 + LANE] = raw_t.T

    dt = _pick_row(_softplus(proj_ref[GT:GT + LANE, :] + bias_ref[...]), GATE_DT + h)
    d_a = jnp.exp(dt * _pick_row(-jnp.exp(alog_ref[...]), GATE_DT + h))
    x_h = _rows(xbc_s, 64 * h, 64)
    b_g = _rows(xbc_s, SSD_W + 64 * g, 64)
    c_g = _rows(xbc_s, SSD_W + LANE + 64 * g, 64)
    u = dt * x_h
    for p in range(64):
        sn = d_a * s_ref[p] + u[p:p + 1, :] * b_g
        s_out[p] = sn
        yh_s[p:p + 1, :] = jnp.sum(sn * c_g, axis=0, keepdims=True)
    z = _rows(proj_ref, SZ + 64 * h, 64)
    ys_s[pl.ds(pl.multiple_of(64 * h, 64), 64), :] = (yh_s[...] + _rows(dskip_ref, 64 * h, 64) * x_h) * _silu(z)

    @pl.when(h % (SSD_HEADS // 2) == SSD_HEADS // 2 - 1)
    def _():
        grp = _rows(ys_s, 256 * g, 256)
        y_ref[...] = (grp * lax.rsqrt(jnp.mean(grp * grp, axis=0, keepdims=True) + EPS)
                      * _rows(ssdg_ref, 256 * g, 256))


def _step_ssd(proj_t, layer, buf_t, bias_col, alog_col, cw_t, dskip_col, ssdg_col, s_t):
    bsz = proj_t.shape[1]
    return pl.pallas_call(
        _step_ssd_kernel,
        grid=(SSD_HEADS,),
        in_specs=[_const_spec((NP, bsz)),
                  pl.BlockSpec((None, SSD_CONV - 1, bsz, SSD_CONV_DIM), lambda h: (layer, 0, 0, 0)),
                  _const_spec((LANE, 1)), _const_spec((LANE, 1)), _const_spec((SSD_CONV_DIM, SUBLANE)),
                  _const_spec((SSD_W, 1)), _const_spec((SSD_W, 1)),
                  pl.BlockSpec((None, None, 64, 64, bsz), lambda h: (layer, h, 0, 0, 0))],
        out_specs=[pl.BlockSpec((None, 64, 64, bsz), lambda h: (h, 0, 0, 0)),
                   pl.BlockSpec((256, bsz), lambda h: (h // (SSD_HEADS // 2), 0)),
                   _const_spec((bsz, SSD_CONV_DIM))],
        out_shape=[jax.ShapeDtypeStruct((SSD_HEADS, 64, 64, bsz), F32),
                   jax.ShapeDtypeStruct((SSD_W, bsz), F32),
                   jax.ShapeDtypeStruct((bsz, SSD_CONV_DIM), F32)],
        scratch_shapes=[pltpu.VMEM((SSD_CONV_DIM, bsz), F32), pltpu.VMEM((64, bsz), F32),
                        pltpu.VMEM((SSD_W, bsz), F32)],
        compiler_params=_cparams("arbitrary"),
        name="step_ssd",
    )(proj_t, buf_t, bias_col, alog_col, cw_t, dskip_col, ssdg_col, s_t)


def _step_gla_kernel(proj_ref, wf2t_ref, bf_ref, glag_ref, s_ref, s_out, y_ref, dec_s):
    h = pl.program_id(0)

    @pl.when(h == 0)
    def _():
        la = _log_sigmoid(_dot(wf2t_ref[...], proj_ref[GT:GT + LANE, :].astype(BF16)) + bf_ref[...]) / GLA_TAU
        dec_s[...] = jnp.exp(la)

    dec = _rows(dec_s, GLA_DK * h, GLA_DK)
    q = _rows(proj_ref, GQ + GLA_DK * h, GLA_DK) * (GLA_DK ** -0.5)
    k = _rows(proj_ref, GK + GLA_DK * h, GLA_DK)
    v = _rows(proj_ref, GV + 64 * h, 64)
    acc = jnp.zeros_like(v)
    for j in range(GLA_DK):
        sn = dec[j:j + 1, :] * s_ref[j] + k[j:j + 1, :] * v
        s_out[j] = sn
        acc = acc + q[j:j + 1, :] * sn
    gg = _rows(proj_ref, GG + 64 * h, 64)
    y_ref[...] = (acc * lax.rsqrt(jnp.mean(acc * acc, axis=0, keepdims=True) + EPS)
                  * _rows(glag_ref, 64 * h, 64) * _silu(gg))


def _step_gla(proj_t, layer, wf2_t, bf_col, glag_col, s_t):
    bsz = proj_t.shape[1]
    return pl.pallas_call(
        _step_gla_kernel,
        grid=(GLA_HEADS,),
        in_specs=[_const_spec((NP, bsz)), _const_spec((LANE, LANE)), _const_spec((LANE, 1)),
                  _const_spec((256, 1)),
                  pl.BlockSpec((None, None, GLA_DK, 64, bsz), lambda h: (layer, h, 0, 0, 0))],
        out_specs=[pl.BlockSpec((None, GLA_DK, 64, bsz), lambda h: (h, 0, 0, 0)),
                   pl.BlockSpec((64, bsz), lambda h: (h, 0))],
        out_shape=[jax.ShapeDtypeStruct((GLA_HEADS, GLA_DK, 64, bsz), F32),
                   jax.ShapeDtypeStruct((256, bsz), F32)],
        scratch_shapes=[pltpu.VMEM((LANE, bsz), F32)],
        compiler_params=_cparams("arbitrary"),
        name="step_gla",
    )(proj_t, wf2_t, bf_col, glag_col, s_t)


def _pad_row(x, width):
    return jnp.pad(x, [(0, 0), (0, width - x.shape[-1])])


def kernel(x_prompt, x_sample, state_mlstm_c, state_mlstm_n, state_mlstm_m, state_ssd, state_ssd_conv,
           state_gla, state_ffn_conv, norm1_g, w_in, mlstm_b_i, mlstm_b_f, mlstm_norm_g, ssd_conv_w,
           ssd_conv_b, ssd_dt_bias, ssd_a_log, ssd_d, ssd_norm_g, gla_w_f2, gla_b_f, gla_norm_g, w_out,
           norm2_g, w_up, ffn_conv_w, ffn_conv_b, w_down, final_norm_g):
    bsz, seq, _ = x_prompt.shape
    dbs = x_sample.shape[0]
    depth = w_in.shape[0]

    w_out_b = w_out.astype(BF16)
    w_up_p = w_up.astype(BF16)
    w_down_p = w_down.astype(BF16)
    ffn_cwb = jnp.concatenate([ffn_conv_w, ffn_conv_b[:, None, :],
                               jnp.zeros((depth, SUBLANE - FFN_CONV - 1, 2 * D_FF), F32)], axis=1)
    zeros8 = jnp.zeros((depth, 8), F32)
    bias_rows = _pad_row(jnp.concatenate([mlstm_b_i, mlstm_b_f, ssd_dt_bias], axis=-1), LANE)[:, None, :]
    alog_rows = _pad_row(jnp.concatenate([zeros8, ssd_a_log], axis=-1), LANE)[:, None, :]
    ssd_cwb = jnp.concatenate([ssd_conv_w, ssd_conv_b[:, None, :],
                               jnp.zeros((depth, SUBLANE - SSD_CONV - 1, SSD_CONV_DIM), F32)], axis=1)
    dskip_rows = jnp.repeat(ssd_d, 64, axis=-1)[:, None, :]
    wf2_p = jnp.pad(gla_w_f2, [(0, 0), (GATE_GF, LANE - GATE_GF - GLA_RANK), (0, 0)]).astype(BF16)

    c_t = state_mlstm_c.transpose(0, 2, 3, 4, 1)
    n_t = state_mlstm_n.transpose(0, 2, 3, 1)
    m_t = state_mlstm_m.transpose(0, 2, 1)
    s_t = state_ssd.transpose(0, 2, 3, 4, 1)
    g_t = state_gla.transpose(0, 2, 3, 4, 1)
    conv_t = state_ssd_conv.transpose(0, 2, 1, 3)

    xp = x_prompt.reshape(bsz * seq, D_MODEL)
    xs = x_sample.reshape(dbs, D_MODEL)
    fg = final_norm_g[None, :]
    p_states, s_states = [], []
    for i in range(depth):
        g1 = norm1_g[i][None, :]
        g2 = norm2_g[i][None, :]
        mlg = mlstm_norm_g[i][None, :]
        ssdg = ssd_norm_g[i][None, :]
        glag = gla_norm_g[i][None, :]
        bfr = gla_b_f[i][None, :]
        last = i == depth - 1

        proj, w_perm = _norm_proj(xp, g1, w_in, i)
        xp, ffn_st, caug, m_o, ssd_o, gla_o = _mix_ffn(
            xp, proj, bsz, seq, i, bias_rows[i], alog_rows[i], mlg, ssd_cwb[i], dskip_rows[i], ssdg, wf2_p[i], bfr,
            glag, w_out_b, g2, w_up_p, ffn_cwb[i], w_down_p, fg, last)
        caug4 = caug.reshape(bsz, ML_HEADS, ML_DK, LANE)
        ssd4 = ssd_o.reshape(bsz, SSD_HEADS, 64, LANE)
        gla4 = gla_o.reshape(bsz, GLA_HEADS, GLA_DK, 256)
        p_states.append((
            caug4[..., :64], caug4[..., 64], m_o[:, 0, :ML_HEADS],
            jnp.where(jnp.arange(SSD_HEADS)[None, :, None, None] < 4, ssd4[..., :64], ssd4[..., 64:]),
            proj.reshape(bsz, seq, NP)[:, seq - (SSD_CONV - 1):, SXBC:SXBC + SSD_CONV_DIM],
            jnp.stack([gla4[:, h, :, 64 * h:64 * (h + 1)] for h in range(GLA_HEADS)], axis=1),
            ffn_st[:, SUBLANE - (FFN_CONV - 1):, :],
        ))

        proj_t = _norm_proj_t(xs, g1, w_perm)
        c_n, n_n, m_n, y_ml = _step_mlstm(proj_t, i, bias_rows[i].T, mlg.T, c_t, n_t, m_t)
        s_n, y_ssd, raw_s = _step_ssd(proj_t, i, conv_t, bias_rows[i].T, alog_rows[i].T, ssd_cwb[i].T,
                                      dskip_rows[i].T, ssdg.T, s_t)
        g_n, y_gla = _step_gla(proj_t, i, wf2_p[i].T, bfr.T, glag.T, g_t)
        mix_t = jnp.concatenate([y_ml, y_ssd, y_gla], axis=0)
        prev = state_ffn_conv[i].transpose(1, 0, 2)
        xs, up_s = _out_ffn_step(xs, mix_t, i, w_out_b, g2, w_up_p, ffn_cwb[i], w_down_p, fg, last, prev)
        s_states.append((
            c_n, n_n, m_n, s_n,
            jnp.concatenate([state_ssd_conv[i][:, 1:], raw_s[:, None, :]], axis=1),
            g_n,
            jnp.concatenate([state_ffn_conv[i][:, 1:], up_s[:, None, :]], axis=1),
        ))

    y_prompt = xp.reshape(bsz, seq, D_MODEL)
    y_sample = xs.reshape(dbs, 1, D_MODEL)

    def stk(sts, j):
        return jnp.stack([s[j] for s in sts], axis=0)

    s_perm = {0: (0, 4, 1, 2, 3), 1: (0, 3, 1, 2), 2: (0, 2, 1), 3: (0, 4, 1, 2, 3), 5: (0, 4, 1, 2, 3)}
    s_out = tuple(stk(s_states, j).transpose(s_perm[j]) if j in s_perm else stk(s_states, j) for j in range(7))
    return (y_prompt, y_sample) + tuple(stk(p_states, j) for j in range(7)) + s_out
```

```python
import functools

import jax
import jax.numpy as jnp
from jax import lax
from jax.experimental import pallas as pl
from jax.experimental.pallas import tpu as pltpu

F32 = jnp.float32
BF16 = jnp.bfloat16
I32 = jnp.int32

D_MODEL = 1024
DEPTH = 4
ML_HEADS = 4
ML_DK = 64
SSD_HEADS = 8
SSD_W = 512
SSD_CONV = 4
SSD_CONV_DIM = 768
GLA_HEADS = 4
GLA_DK = 32
GLA_RANK = 16
GLA_TAU = 16.0
D_FF = 2816
FFN_CONV = 3
EPS = 1e-6
IN_SIZES = (256, 256, 256, 256, 4, 4, 512, 768, 8, 128, 128, 256, 256, 16)

MQ, MK, MV, MO, SZ, SXBC, GQ, GK, GV, GG, GT, NP = (
    0, 256, 512, 768, 1024, 1536, 2304, 2432, 2560, 2816, 3072, 3200)
GATE_LI, GATE_LF, GATE_DT, GATE_GF = 0, 4, 8, 16

LANE = 128
SUBLANE = 8
MIX_T = 128
GLA_C = 8
FFN_TM = 512
FFN_ROWS = 1
FF_CHUNK = 256
N_FF_CHUNK = D_FF // FF_CHUNK
VMEM_LIMIT = 56 * 1024 * 1024
VMEM_LIMIT_FUSED = 62 * 1024 * 1024


def _cparams(*sem, vmem=VMEM_LIMIT):
    return pltpu.CompilerParams(dimension_semantics=sem if sem else None, vmem_limit_bytes=vmem)


def _const_spec(shape):
    nd = len(shape)
    return pl.BlockSpec(shape, lambda *_: (0,) * nd)


def _layer_spec(shape, layer, resident=False):
    nd = len(shape)
    return pl.BlockSpec((None,) + tuple(shape), lambda *_: (layer,) + (0,) * nd,
                        pipeline_mode=pl.Buffered(1) if resident else None)


def _sigmoid(x):
    return 1.0 / (1.0 + jnp.exp(-x))


def _silu(x):
    h = 0.5 * x
    return h + h * jnp.tanh(h)


def _softplus(x):
    return jnp.maximum(x, 0.0) + jnp.log(1.0 + jnp.exp(-jnp.abs(x)))


def _log_sigmoid(x):
    return -_softplus(-x)


def _rmsnorm_rows(x, g):
    ms = jnp.mean(x * x, axis=-1, keepdims=True)
    return x * lax.rsqrt(ms + EPS) * g


def _dot(a, b):
    return jnp.dot(a, b, preferred_element_type=F32)


def _dot_nt(a, b):
    return lax.dot_general(a, b, (((1,), (1,)), ((), ())), preferred_element_type=F32)


def _split3(x):
    x1 = x.astype(BF16)
    r1 = x - x1.astype(F32)
    x2 = r1.astype(BF16)
    x3 = (r1 - x2.astype(F32)).astype(BF16)
    return x1, x2, x3


def _cumsum_rows(x, tri):
    c = _dot(tri, jnp.concatenate(_split3(x), axis=1))
    return c[:, :LANE] + c[:, LANE:2 * LANE] + c[:, 2 * LANE:]


def _shift_rows(x, k, carry):
    r = pltpu.roll(x, k, 0)
    c = pltpu.roll(carry, k, 0)
    row = lax.broadcasted_iota(I32, c.shape, 0)
    head = jnp.where(row < k, c, r[:SUBLANE])
    return jnp.concatenate([head, r[SUBLANE:]], axis=0)


def _seg_rmsnorm(x, seg):
    w = x.shape[1]
    if seg % LANE == 0:
        parts = []
        for s0 in range(0, w, seg):
            xs = x[:, s0:s0 + seg]
            parts.append(xs * lax.rsqrt(jnp.mean(xs * xs, axis=-1, keepdims=True) + EPS))
        return jnp.concatenate(parts, axis=1)
    parts = []
    lane = lax.broadcasted_iota(I32, (x.shape[0], LANE), 1)
    for s0 in range(0, w, LANE):
        xs = x[:, s0:s0 + LANE]
        sq = xs * xs
        lo = jnp.sum(jnp.where(lane < seg, sq, 0.0), axis=-1, keepdims=True)
        hi = jnp.sum(jnp.where(lane >= seg, sq, 0.0), axis=-1, keepdims=True)
        r = jnp.where(lane < seg, lax.rsqrt(lo / seg + EPS), lax.rsqrt(hi / seg + EPS))
        parts.append(xs * r)
    return jnp.concatenate(parts, axis=1)


def _drive(chains):
    chains = list(chains)
    while chains:
        for c in list(chains):
            try:
                next(c)
            except StopIteration:
                chains.remove(c)


def _in_segments():
    dst = (MQ, MK, MV, MO, GT + GATE_LI, GT + GATE_LF, SZ, SXBC, GT + GATE_DT, GQ, GK, GV, GG, GT + GATE_GF)
    segs, src = [], 0
    for width, d in zip(IN_SIZES, dst):
        segs.append((src, d, width))
        src += width
    return segs


def _norm_proj_kernel(x_ref, g_ref, w_ref, wo_ref, wup_ref, o_ref, wp_ref, wo_b_ref, wup_b_ref):
    @pl.when(pl.program_id(0) == 0)
    def _():
        wp_ref[:, GT:GT + LANE] = jnp.zeros((D_MODEL, LANE), BF16)
        for src, dst, width in _in_segments():
            wp_ref[:, dst:dst + width] = w_ref[:, src:src + width].astype(BF16)

    wo_b_ref[...] = wo_ref[...].astype(BF16)
    wup_b_ref[...] = wup_ref[...].astype(BF16)
    h = _rmsnorm_rows(x_ref[...], g_ref[...]).astype(BF16)
    for n0 in range(0, NP, 512):
        n1 = min(n0 + 512, NP)
        o_ref[:, n0:n1] = _dot(h, wp_ref[:, n0:n1])


def _norm_proj(x2d, g, w_all, wo_all, wup_all, layer):
    m = x2d.shape[0]
    tm = min(512, m)
    n_in = w_all.shape[-1]
    steps = m // tm
    slab = D_MODEL // steps
    assert D_MODEL % steps == 0 and slab % 16 == 0
    w_tile = lambda width: pl.BlockSpec((None, slab, width), lambda i: (layer, i, 0))
    return pl.pallas_call(
        _norm_proj_kernel,
        grid=(steps,),
        in_specs=[pl.BlockSpec((tm, D_MODEL), lambda i: (i, 0)),
                  _const_spec((1, D_MODEL)),
                  _layer_spec((D_MODEL, n_in), layer, resident=True),
                  w_tile(D_MODEL), w_tile(2 * D_FF)],
        out_specs=[pl.BlockSpec((tm, NP), lambda i: (i, 0)), _const_spec((D_MODEL, NP)),
                   pl.BlockSpec((slab, D_MODEL), lambda i: (i, 0)), pl.BlockSpec((slab, 2 * D_FF), lambda i: (i, 0))],
        out_shape=[jax.ShapeDtypeStruct((m, NP), F32), jax.ShapeDtypeStruct((D_MODEL, NP), BF16),
                   jax.ShapeDtypeStruct((D_MODEL, D_MODEL), BF16), jax.ShapeDtypeStruct((D_MODEL, 2 * D_FF), BF16)],
        compiler_params=_cparams("arbitrary"),
        name="norm_proj",
    )(x2d, g, w_all, wo_all, wup_all)


class _MixerRow:
    def __init__(self, proj_ref, bias_ref, alog_ref, mlg_ref, cw_ref, dskip_ref, ssdg_ref, wf2_ref,
                 bf_ref, glag_ref, mix_ref, caug_ref, m_ref, ssd_ref, gla_ref, conv_s, cum_s, live):
        self.proj_ref, self.bias_ref, self.alog_ref, self.mlg_ref = proj_ref, bias_ref, alog_ref, mlg_ref
        self.cw_ref, self.dskip_ref, self.ssdg_ref, self.wf2_ref = cw_ref, dskip_ref, ssdg_ref, wf2_ref
        self.bf_ref, self.glag_ref, self.mix_ref, self.caug_ref = bf_ref, glag_ref, mix_ref, caug_ref
        self.m_ref, self.ssd_ref, self.gla_ref, self.conv_s, self.cum_s = m_ref, ssd_ref, gla_ref, conv_s, cum_s
        self.live = live
        self.out = {}
        self.levels = {}

    def keep(self, new, old):
        return jnp.where(self.live, new, old)

    def setup(self):
        T = MIX_T
        proj_ref = self.proj_ref
        self.lane = lane = lax.broadcasted_iota(I32, (T, LANE), 1)
        self.rowi = lax.broadcasted_iota(I32, (T, T), 0)
        self.coli = lax.broadcasted_iota(I32, (T, T), 1)
        self.causal = self.rowi >= self.coli
        tri = jnp.where(self.causal, 1.0, 0.0).astype(BF16)
        self.row128 = lax.broadcasted_iota(I32, (LANE, LANE), 0)
        self.g_blk = g_blk = proj_ref[:, GT:GT + LANE] + self.bias_ref[...]
        is_lf = (lane >= GATE_LF) & (lane < GATE_DT)
        is_dt = (lane >= GATE_DT) & (lane < GATE_GF)
        self.dt = dt = _softplus(jnp.where(is_lf, -g_blk, g_blk))
        a_row = -jnp.exp(self.alog_ref[...])
        self.cum_g = _cumsum_rows(jnp.where(is_lf, -dt, jnp.where(is_dt, dt * a_row, 0.0)), tri)
        yield
        self.p1_t = jnp.where(lane < GATE_LF, g_blk, self.cum_g).T
        self.dt_t = dt.T
        self.k_ml_t = [proj_ref[:, MK + LANE * p:MK + LANE * (p + 1)].T for p in range(2)]
        raw = proj_ref[:, SXBC:SXBC + SSD_CONV_DIM]
        carry = self.conv_s[...]
        cw_ref = self.cw_ref
        conv = (cw_ref[4:5, :] + _shift_rows(raw, 3, carry) * cw_ref[0:1, :]
                + _shift_rows(raw, 2, carry) * cw_ref[1:2, :]
                + _shift_rows(raw, 1, carry) * cw_ref[2:3, :] + raw * cw_ref[3:4, :])
        self.conv_s[...] = self.keep(raw[T - SUBLANE:T, :], carry)
        self.xbc = xbc = _silu(conv)
        self.x_ssd_t = [xbc[:, LANE * j:LANE * (j + 1)].T for j in range(SSD_HEADS // 2)]
        self.b_p = b_p = xbc[:, SSD_W:SSD_W + LANE]
        c_p = xbc[:, SSD_W + LANE:SSD_W + 2 * LANE]
        self.c_pb = c_p.astype(BF16)
        self.cb = []
        for g in range(2):
            in_g = (lane >= 64 * g) & (lane < 64 * (g + 1))
            self.cb.append(_dot_nt(jnp.where(in_g, c_p, 0.0).astype(BF16), b_p.astype(BF16)))
            yield
        la = _log_sigmoid(_dot(g_blk.astype(BF16), self.wf2_ref[...]) + self.bf_ref[...]) / GLA_TAU
        yield
        self.cum = cum = _cumsum_rows(la, tri)
        yield
        self.cum_s[...] = cum
        self.cum_t = cum.T
        self.q = proj_ref[:, GQ:GQ + LANE] * (GLA_DK ** -0.5)
        self.k = proj_ref[:, GK:GK + LANE]
        self.k_gla_t = self.k.T
        self.v = proj_ref[:, GV:GV + 256]
        r_s = lax.broadcasted_iota(I32, (LANE, 256), 0)
        c_s = lax.broadcasted_iota(I32, (LANE, 256), 1)
        self.same_head = (r_s >> 5) == (c_s >> 6)
        self.tpos = lax.broadcasted_iota(I32, (T, LANE), 0)

    def chains(self):
        return ([self.ml_head(h) for h in range(ML_HEADS)] + [self.ssd_pair(j) for j in range(SSD_HEADS // 2)]
                + [self.gla_state(), self.gla_diag()] + [self.gla_head(h) for h in range(GLA_HEADS)])

    def ml_head(self, h):
        T = MIX_T
        proj_ref, lane, cum_g, p1_t = self.proj_ref, self.lane, self.cum_g, self.p1_t
        pair, lo = h // 2, 64 * (h % 2)
        hm = (lane >= lo) & (lane < lo + 64)
        qm = jnp.where(hm, proj_ref[:, MQ + LANE * pair:MQ + LANE * (pair + 1)] * 0.125, 0.0).astype(BF16)
        km = jnp.where(hm, proj_ref[:, MK + LANE * pair:MK + LANE * (pair + 1)], 0.0)
        qk = _dot_nt(qm, km.astype(BF16))
        yield
        c_pair = self.caug_ref[LANE * pair:LANE * (pair + 1), :]
        qc = _dot(qm, c_pair.astype(BF16))
        yield
        v_pair = proj_ref[:, MV + LANE * pair:MV + LANE * (pair + 1)]
        v_lo = pltpu.roll(v_pair, 64, 1) if h % 2 else v_pair
        vaug = jnp.where(lane < 64, v_lo, jnp.where(lane == 64, 1.0, 0.0)).astype(BF16)
        b_col = cum_g[:, GATE_LF + h:GATE_LF + h + 1]
        b_row = p1_t[GATE_LF + h:GATE_LF + h + 1, :]
        li_row = p1_t[GATE_LI + h:GATE_LI + h + 1, :]
        m_prev = self.m_ref[:, h:h + 1]
        d_mat = jnp.where(self.causal, b_col - b_row + li_row, -jnp.inf)
        yield
        row_max = jnp.max(d_mat, axis=1, keepdims=True)
        yield
        a_col = b_col + m_prev
        mt = jnp.maximum(a_col, row_max)
        s_mat = (qk * jnp.exp(d_mat - mt)).astype(BF16)
        numaug = _dot(s_mat, vaug) + jnp.exp(a_col - mt) * qc
        yield
        den = numaug[:, 64:65]
        hn = numaug / jnp.maximum(jnp.abs(den), jnp.exp(-mt))
        hv = jnp.where(lane < 64, hn, 0.0)
        yield
        self.out["ml", h] = hv * lax.rsqrt(jnp.sum(hv * hv, axis=-1, keepdims=True) / 64.0 + EPS)
        b_last = b_col[T - 1:T, :]
        m_new = mt[T - 1:T, :]
        g_state = jnp.exp(b_last + m_prev - m_new)
        g_s_row = jnp.exp(b_last - b_row + li_row - m_new)
        u = _dot((self.k_ml_t[pair][lo:lo + 64, :] * g_s_row).astype(BF16), vaug)
        yield
        c_old = c_pair[lo:lo + 64]
        self.caug_ref[LANE * pair + lo:LANE * pair + lo + 64, :] = self.keep(g_state * c_old + u, c_old)
        self.out["m", h] = m_new

    def ssd_pair(self, j):
        T = MIX_T
        lane, cum_g, p1_t = self.lane, self.cum_g, self.p1_t
        g = j // 2
        x_pair = self.xbc[:, LANE * j:LANE * (j + 1)]
        s_pair = self.ssd_ref[LANE * j:LANE * (j + 1), :]
        inter = _dot_nt(self.c_pb, s_pair.astype(BF16))
        yield
        y_pair = None
        ecs, wrow, dec = [], [], []
        for hh in range(2):
            h = 2 * j + hh
            cs_col = cum_g[:, GATE_DT + h:GATE_DT + h + 1]
            cs_row = p1_t[GATE_DT + h:GATE_DT + h + 1, :]
            dt_row = self.dt_t[GATE_DT + h:GATE_DT + h + 1, :]
            seg = jnp.where(self.causal, cs_col - cs_row, -jnp.inf)
            yield
            sc = (self.cb[g] * jnp.exp(seg) * dt_row).astype(BF16)
            in_h = (lane >= 64 * hh) & (lane < 64 * (hh + 1))
            part = _dot(sc, jnp.where(in_h, x_pair, 0.0).astype(BF16))
            yield
            y_pair = part if y_pair is None else y_pair + part
            cs_last = cs_col[T - 1:T, :]
            ecs.append(jnp.exp(cs_col))
            wrow.append(jnp.exp(cs_last - cs_row) * dt_row)
            dec.append(jnp.exp(cs_last))
        self.out["ssd", j] = y_pair + jnp.where(lane < 64, ecs[0], ecs[1]) * inter
        wx_t = (self.x_ssd_t[j] * jnp.where(self.row128 < 64, wrow[0], wrow[1])).astype(BF16)
        in_g = (lane >= 64 * g) & (lane < 64 * (g + 1))
        u = _dot(wx_t, jnp.where(in_g, self.b_p, 0.0).astype(BF16))
        yield
        self.ssd_ref[LANE * j:LANE * (j + 1), :] = self.keep(
            jnp.where(self.row128 < 64, dec[0], dec[1]) * s_pair + u, s_pair)

    def gla_state(self):
        T = MIX_T
        cum = self.cum
        s_gla = self.gla_ref[...]
        self.out["gla_inter"] = _dot((self.q * jnp.exp(cum)).astype(BF16), s_gla.astype(BF16))
        yield
        cum_t = self.cum_t
        u = _dot((self.k_gla_t * jnp.exp(cum_t[:, T - 1:T] - cum_t)).astype(BF16), self.v.astype(BF16))
        yield
        self.gla_ref[...] = self.keep(jnp.exp(self.cum_t[:, T - 1:T]) * s_gla + jnp.where(self.same_head, u, 0.0), s_gla)

    def gla_level(self, w):
        if w not in self.levels:
            T = MIX_T
            is_t = (self.tpos & (2 * w - 1)) >= w
            mid = jnp.concatenate(
                [jnp.broadcast_to(self.cum_s[2 * w * blk + w - 1:2 * w * blk + w, :], (2 * w, LANE))
                 for blk in range(T // (2 * w))], axis=0)
            e = jnp.exp(jnp.where(is_t, self.cum - mid, mid - self.cum))
            ql = jnp.where(is_t, self.q * e, 0.0)
            kl = jnp.where(is_t, 0.0, self.k * e).astype(BF16)
            shift = (2 * w).bit_length() - 1
            self.levels[w] = (ql, kl, (self.rowi >> shift) == (self.coli >> shift))
        return self.levels[w]

    def gla_head(self, h):
        lane = self.lane
        in_h = (lane >= GLA_DK * h) & (lane < GLA_DK * (h + 1))
        att = None
        w = GLA_C
        while w < MIX_T:
            ql, kl, same_blk = self.gla_level(w)
            a = jnp.where(same_blk, _dot_nt(jnp.where(in_h, ql, 0.0).astype(BF16), kl), 0.0)
            yield
            att = a if att is None else att + a
            w *= 2
        v_pair = self.v[:, LANE * (h // 2):LANE * (h // 2 + 1)]
        in_half = (lane >= 64) if h % 2 else (lane < 64)
        self.out["gla_o", h] = _dot(att.astype(BF16), jnp.where(in_half, v_pair, 0.0).astype(BF16))
        yield

    def gla_diag(self):
        expand = jnp.where(self.same_head, 1.0, 0.0).astype(BF16)
        nblk = MIX_T // GLA_C
        o = None
        for jj in range(GLA_C):
            def rows(ref, c0, c1):
                return jnp.concatenate(
                    [jnp.broadcast_to(ref[GLA_C * i + jj:GLA_C * i + jj + 1, c0:c1], (GLA_C, c1 - c0))
                     for i in range(nblk)], axis=0)
            k_s = rows(self.proj_ref, GK, GK + LANE)
            c_srow = rows(self.cum_s, 0, LANE)
            v_s = rows(self.proj_ref, GV, GV + 256)
            valid = (self.tpos & (GLA_C - 1)) >= jj
            e = jnp.exp(jnp.where(valid, self.cum - c_srow, -jnp.inf))
            part = _dot((self.q * k_s * e).astype(BF16), expand) * v_s
            yield
            o = part if o is None else o + part
        self.out["gla_diag"] = o

    def finish(self):
        proj_ref, mix_ref, lane, out = self.proj_ref, self.mix_ref, self.lane, self.out
        m_row = self.m_ref[...]
        m_lane = lax.broadcasted_iota(I32, m_row.shape, 1)
        for h in range(ML_HEADS):
            m_row = jnp.where((m_lane == h) & self.live, out["m", h], m_row)
        self.m_ref[...] = m_row
        y_ml = jnp.concatenate(
            [jnp.where(lane < 64, out["ml", 2 * p], pltpu.roll(out["ml", 2 * p + 1], 64, 1)) for p in range(2)],
            axis=1)
        mix_ref[:, 0:256] = (_sigmoid(proj_ref[:, MO:MO + 256]) * (y_ml * self.mlg_ref[...])).astype(BF16)
        y_s = (jnp.concatenate([out["ssd", j] for j in range(SSD_HEADS // 2)], axis=1)
               + self.dskip_ref[...] * self.xbc[:, :SSD_W])
        z = proj_ref[:, SZ:SZ + SSD_W]
        mix_ref[:, 256:768] = (_seg_rmsnorm(y_s * _silu(z), 256) * self.ssdg_ref[...]).astype(BF16)
        o = (out["gla_inter"] + out["gla_diag"]
             + jnp.concatenate([out["gla_o", 2 * p] + out["gla_o", 2 * p + 1] for p in range(2)], axis=1))
        gg = proj_ref[:, GG:GG + 256]
        mix_ref[:, 768:1024] = (_seg_rmsnorm(o, 64) * self.glag_ref[...] * _silu(gg)).astype(BF16)


def _ffn_row(x_ref, mix_fn, wo_ref, g2_ref, wup_ref, cwb_ref, wd_ref, fg_ref, o_ref, h2_s, act_s,
             prev_fn, keep_fn, final):
    xn = x_ref[...] + _dot(mix_fn(), wo_ref[...])
    yield
    o_ref[...] = xn
    h2_s[...] = _rmsnorm_rows(xn, g2_ref[...]).astype(BF16)
    for c in range(N_FF_CHUNK):
        h2 = h2_s[...]
        halves = []
        for ug in range(2):
            col = slice(ug * D_FF + c * FF_CHUNK, ug * D_FF + (c + 1) * FF_CHUNK)
            up = _dot(h2, wup_ref[:, col])
            yield
            x1, x2 = prev_fn(col, up)
            keep_fn(col, up)
            halves.append(cwb_ref[3:4, col] + x2 * cwb_ref[0:1, col] + x1 * cwb_ref[1:2, col]
                          + up * cwb_ref[2:3, col])
        u, gate = halves
        act_s[:, c * FF_CHUNK:(c + 1) * FF_CHUNK] = (_silu(gate) * u).astype(BF16)
    y = o_ref[...] + _dot(act_s[...], wd_ref[...])
    yield
    o_ref[...] = _rmsnorm_rows(y, fg_ref[...]) if final else y


def _out_ffn_step_kernel(x_ref, mix_ref, wo_ref, g2_ref, wup_ref, cwb_ref, wd_ref, fg_ref, prev_ref, o_ref, up_ref,
                         h2_s, act_s, *, final):
    def mix_fn():
        mix = jnp.concatenate([mix_ref[c0:c0 + LANE, :].T for c0 in range(0, D_MODEL, LANE)], axis=1)
        return mix.astype(BF16)

    def prev_fn(col, up):
        return prev_ref[1, :, col], prev_ref[0, :, col]

    def keep_fn(col, up):
        up_ref[:, col] = up

    _drive([_ffn_row(x_ref, mix_fn, wo_ref, g2_ref, wup_ref, cwb_ref, wd_ref, fg_ref, o_ref, h2_s, act_s,
                     prev_fn, keep_fn, final)])


def _resident_spec(shape):
    nd = len(shape)
    return pl.BlockSpec(shape, lambda *_: (0,) * nd, pipeline_mode=pl.Buffered(1))


def _ffn_w_specs(layer):
    return [_resident_spec((D_MODEL, D_MODEL)), _const_spec((1, D_MODEL)),
            _resident_spec((D_MODEL, 2 * D_FF)),
            _const_spec((SUBLANE, 2 * D_FF)),
            _layer_spec((D_FF, D_MODEL), layer, resident=True), _const_spec((1, D_MODEL))]


def _rounds(chains, per_yield=1):
    chains = list(chains)
    n = 0
    while chains:
        for c in list(chains):
            try:
                next(c)
            except StopIteration:
                chains.remove(c)
        n += 1
        if n % per_yield == 0:
            yield


def _mix_ffn_kernel(proj_ref, x_ref, bias_ref, alog_ref, mlg_ref, cw_ref, dskip_ref, ssdg_ref, wf2_ref, bf_ref,
                    glag_ref, wo_ref, g2_ref, wup_ref, cwb_ref, wd_ref, fg_ref,
                    o_ref, st_ref, caug_ref, m_ref, ssd_ref, gla_ref,
                    conv_s, cum_s, mix_s, h2_s, act_s, *, n_steps, nt, final):
    g = pl.program_id(0)
    live = g < n_steps
    t_m = jnp.minimum(g, n_steps - 1) % nt
    t_f = jnp.maximum(g - 1, 0) % nt
    slot = g % 2

    @pl.when((t_m == 0) & live)
    def _():
        caug_ref[...] = jnp.zeros_like(caug_ref)
        m_ref[...] = jnp.zeros_like(m_ref)
        ssd_ref[...] = jnp.zeros_like(ssd_ref)
        gla_ref[...] = jnp.zeros_like(gla_ref)
        conv_s[...] = jnp.zeros_like(conv_s)

    @pl.when(t_f == 0)
    def _():
        st_ref[...] = jnp.zeros_like(st_ref)

    @pl.when(g == 0)
    def _():
        mix_s[...] = jnp.zeros_like(mix_s)

    n_rows, tm = x_ref.shape[0], x_ref.shape[1]

    def ffn_chain(r):
        st = st_ref.at[r]

        def prev_fn(col, up):
            carry = st[:, col]
            return _shift_rows(up, 1, carry), _shift_rows(up, 2, carry)

        def keep_fn(col, up):
            st[:, col] = up[tm - SUBLANE:tm, :]

        return _ffn_row(x_ref.at[r], lambda: mix_s[1 - slot, r], wo_ref, g2_ref, wup_ref, cwb_ref, wd_ref, fg_ref,
                        o_ref.at[r], h2_s.at[r], act_s.at[r], prev_fn, keep_fn, final)

    def mixer_program():
        for k in range(tm // MIX_T):
            tok = pl.ds(k * MIX_T, MIX_T)
            for r in range(n_rows):
                row = _MixerRow(proj_ref.at[r, tok], bias_ref, alog_ref, mlg_ref, cw_ref, dskip_ref, ssdg_ref,
                                wf2_ref, bf_ref, glag_ref, mix_s.at[slot, r, tok], caug_ref.at[r], m_ref.at[r],
                                ssd_ref.at[r], gla_ref.at[r], conv_s.at[r], cum_s.at[r], live)
                yield from _rounds([row.setup()], 2)
                yield from _rounds(row.chains(), 2)
                row.finish()

    _drive([ffn_chain(r) for r in range(n_rows)] + [mixer_program()])


def _mix_ffn(x2d, proj, bsz, seq, layer, bias_row, alog_row, mlg, cwb_ssd, dskip, ssdg, wf2p, bfr, glag,
             wo, g2, wup, cwb, wd, fg, final):
    tm = min(FFN_TM, seq)
    nt = seq // tm
    rows = FFN_ROWS if bsz % FFN_ROWS == 0 else 1
    n_steps = (bsz // rows) * nt
    row_spec = lambda w: _const_spec((1, w))
    mix_idx = lambda g: (jnp.minimum(g, n_steps - 1) // nt, jnp.minimum(g, n_steps - 1) % nt, 0)
    ffn_idx = lambda g: (jnp.maximum(g - 1, 0) // nt, jnp.maximum(g - 1, 0) % nt, 0)
    mix_pair = lambda g: (jnp.minimum(g, n_steps - 1) // nt, 0, 0)
    ffn_pair = lambda g: (jnp.maximum(g - 1, 0) // nt, 0, 0)
    out, st, caug, m_o, ssd_o, gla_o = pl.pallas_call(
        functools.partial(_mix_ffn_kernel, n_steps=n_steps, nt=nt, final=final),
        grid=(n_steps + 1,),
        in_specs=[pl.BlockSpec((rows, tm, NP), mix_idx), pl.BlockSpec((rows, tm, D_MODEL), ffn_idx),
                  row_spec(LANE), row_spec(LANE), row_spec(256),
                  _const_spec((SUBLANE, SSD_CONV_DIM)), row_spec(SSD_W), row_spec(SSD_W),
                  _const_spec((LANE, LANE)), row_spec(LANE), row_spec(256)] + _ffn_w_specs(layer),
        out_specs=[pl.BlockSpec((rows, tm, D_MODEL), ffn_idx),
                   pl.BlockSpec((rows, SUBLANE, 2 * D_FF), ffn_pair),
                   pl.BlockSpec((rows, 256, LANE), mix_pair),
                   pl.BlockSpec((rows, 1, LANE), mix_pair),
                   pl.BlockSpec((rows, 512, LANE), mix_pair),
                   pl.BlockSpec((rows, LANE, 256), mix_pair)],
        out_shape=[jax.ShapeDtypeStruct((bsz, seq, D_MODEL), F32),
                   jax.ShapeDtypeStruct((bsz, SUBLANE, 2 * D_FF), F32),
                   jax.ShapeDtypeStruct((bsz, 256, LANE), F32),
                   jax.ShapeDtypeStruct((bsz, 1, LANE), F32),
                   jax.ShapeDtypeStruct((bsz, 512, LANE), F32),
                   jax.ShapeDtypeStruct((bsz, LANE, 256), F32)],
        scratch_shapes=[pltpu.VMEM((rows, SUBLANE, SSD_CONV_DIM), F32), pltpu.VMEM((rows, MIX_T, LANE), F32),
                        pltpu.VMEM((2, rows, tm, D_MODEL), BF16),
                        pltpu.VMEM((rows, tm, D_MODEL), BF16), pltpu.VMEM((rows, tm, D_FF), BF16)],
        compiler_params=_cparams("arbitrary", vmem=VMEM_LIMIT_FUSED),
        name="mix_ffn",
    )(proj.reshape(bsz, seq, NP), x2d.reshape(bsz, seq, D_MODEL), bias_row, alog_row, mlg, cwb_ssd, dskip, ssdg,
      wf2p, bfr, glag, wo, g2, wup, cwb, wd, fg)
    return out.reshape(bsz * seq, D_MODEL), st, caug, m_o, ssd_o, gla_o


def _out_ffn_step(x2d, mix, layer, wo, g2, wup, cwb, wd, fg, final, prev):
    m = x2d.shape[0]
    return pl.pallas_call(
        functools.partial(_out_ffn_step_kernel, final=final),
        grid=(1,),
        in_specs=[_const_spec((m, D_MODEL)), _const_spec((D_MODEL, m))] + _ffn_w_specs(layer)
        + [_const_spec((FFN_CONV - 1, m, 2 * D_FF))],
        out_specs=[_const_spec((m, D_MODEL)), _const_spec((m, 2 * D_FF))],
        out_shape=[jax.ShapeDtypeStruct((m, D_MODEL), F32),
                   jax.ShapeDtypeStruct((m, 2 * D_FF), F32)],
        scratch_shapes=[pltpu.VMEM((m, D_MODEL), BF16), pltpu.VMEM((m, D_FF), BF16)],
        compiler_params=_cparams("arbitrary"),
        name="out_ffn_step",
    )(x2d, mix, wo, g2, wup, cwb, wd, fg, prev)


def _norm_proj_t_kernel(x_ref, g_ref, w_ref, o_ref):
    h = _rmsnorm_rows(x_ref[...], g_ref[...]).astype(BF16)
    for n0 in range(0, NP, 512):
        n1 = min(n0 + 512, NP)
        blk = _dot(h, w_ref[:, n0:n1])
        for c0 in range(0, n1 - n0, LANE):
            o_ref[n0 + c0:n0 + c0 + LANE, :] = blk[:, c0:c0 + LANE].T


def _norm_proj_t(x2d, g, w_perm):
    m = x2d.shape[0]
    return pl.pallas_call(
        _norm_proj_t_kernel,
        grid=(1,),
        in_specs=[_const_spec((m, D_MODEL)), _const_spec((1, D_MODEL)), _const_spec((D_MODEL, NP))],
        out_specs=_const_spec((NP, m)),
        out_shape=jax.ShapeDtypeStruct((NP, m), F32),
        compiler_params=_cparams("arbitrary"),
        name="norm_proj_t",
    )(x2d, g, w_perm)


def _pick_row(x, r):
    row = lax.broadcasted_iota(I32, x.shape, 0)
    return jnp.sum(jnp.where(row == r, x, 0.0), axis=0, keepdims=True)


def _rows(ref, start, size):
    return ref[pl.ds(pl.multiple_of(start, size), size), :]


def _step_mlstm_kernel(proj_ref, bias_ref, mlg_ref, c_ref, n_ref, m_ref, c_out, n_out, m_out, y_ref):
    h = pl.program_id(0)
    gates = proj_ref[GT:GT + LANE, :] + bias_ref[...]
    li = _pick_row(gates, GATE_LI + h)
    lf = _log_sigmoid(_pick_row(gates, GATE_LF + h))
    m0 = m_ref[pl.ds(h, 1), :]
    mt = jnp.maximum(lf + m0, li)
    w_old = jnp.exp(lf + m0 - mt)
    q = _rows(proj_ref, MQ + ML_DK * h, ML_DK) * 0.125
    kw = _rows(proj_ref, MK + ML_DK * h, ML_DK) * jnp.exp(li - mt)
    v = _rows(proj_ref, MV + 64 * h, 64)
    acc = jnp.zeros_like(v)
    for d in range(ML_DK):
        cn = w_old * c_ref[d] + kw[d:d + 1, :] * v
        c_out[d] = cn
        acc = acc + q[d:d + 1, :] * cn
    nn = w_old * n_ref[...] + kw
    n_out[...] = nn
    den = jnp.sum(q * nn, axis=0, keepdims=True)
    hn = acc / jnp.maximum(jnp.abs(den), jnp.exp(-mt))
    yn = hn * lax.rsqrt(jnp.mean(hn * hn, axis=0, keepdims=True) + EPS)
    y_ref[...] = _sigmoid(_rows(proj_ref, MO + 64 * h, 64)) * (yn * _rows(mlg_ref, 64 * h, 64))
    m_out[pl.ds(h, 1), :] = mt


def _step_mlstm(proj_t, layer, bias_col, mlg_col, c_t, n_t, m_t):
    bsz = proj_t.shape[1]
    return pl.pallas_call(
        _step_mlstm_kernel,
        grid=(ML_HEADS,),
        in_specs=[_const_spec((NP, bsz)), _const_spec((LANE, 1)), _const_spec((256, 1)),
                  pl.BlockSpec((None, None, ML_DK, 64, bsz), lambda h: (layer, h, 0, 0, 0)),
                  pl.BlockSpec((None, None, ML_DK, bsz), lambda h: (layer, h, 0, 0)),
                  pl.BlockSpec((None, ML_HEADS, bsz), lambda h: (layer, 0, 0))],
        out_specs=[pl.BlockSpec((None, ML_DK, 64, bsz), lambda h: (h, 0, 0, 0)),
                   pl.BlockSpec((None, ML_DK, bsz), lambda h: (h, 0, 0)),
                   _const_spec((ML_HEADS, bsz)),
                   pl.BlockSpec((64, bsz), lambda h: (h, 0))],
        out_shape=[jax.ShapeDtypeStruct((ML_HEADS, ML_DK, 64, bsz), F32),
                   jax.ShapeDtypeStruct((ML_HEADS, ML_DK, bsz), F32),
                   jax.ShapeDtypeStruct((ML_HEADS, bsz), F32),
                   jax.ShapeDtypeStruct((256, bsz), F32)],
        compiler_params=_cparams("arbitrary"),
        name="step_mlstm",
    )(proj_t, bias_col, mlg_col, c_t, n_t, m_t)


def _step_ssd_kernel(proj_ref, buf_ref, bias_ref, alog_ref, cw_ref, dskip_ref, ssdg_ref, s_ref,
                     s_out, y_ref, raw_out, xbc_s, yh_s, ys_s):
    h = pl.program_id(0)
    g = h // (SSD_HEADS // 2)

    @pl.when(h == 0)
    def _():
        for c0 in range(0, SSD_CONV_DIM, LANE):
            raw_t = proj_ref[SXBC + c0:SXBC + c0 + LANE, :]
            cw = cw_ref[c0:c0 + LANE, :]
            acc = cw[:, 4:5] + raw_t * cw[:, 3:4]
            for j in range(SSD_CONV - 1):
                acc = acc + buf_ref[j, :, c0:c0 + LANE].T * cw[:, j:j + 1]
            xbc_s[c0:c0 + LANE, :] = _silu(acc)
            raw_out[:, c0:c0 + LANE] = raw_t.T

    dt = _pick_row(_softplus(proj_ref[GT:GT + LANE, :] + bias_ref[...]), GATE_DT + h)
    d_a = jnp.exp(dt * _pick_row(-jnp.exp(alog_ref[...]), GATE_DT + h))
    x_h = _rows(xbc_s, 64 * h, 64)
    b_g = _rows(xbc_s, SSD_W + 64 * g, 64)
    c_g = _rows(xbc_s, SSD_W + LANE + 64 * g, 64)
    u = dt * x_h
    for p in range(64):
        sn = d_a * s_ref[p] + u[p:p + 1, :] * b_g
        s_out[p] = sn
        yh_s[p:p + 1, :] = jnp.sum(sn * c_g, axis=0, keepdims=True)
    z = _rows(proj_ref, SZ + 64 * h, 64)
    ys_s[pl.ds(pl.multiple_of(64 * h, 64), 64), :] = (yh_s[...] + _rows(dskip_ref, 64 * h, 64) * x_h) * _silu(z)

    @pl.when(h % (SSD_HEADS // 2) == SSD_HEADS // 2 - 1)
    def _():
        grp = _rows(ys_s, 256 * g, 256)
        y_ref[...] = (grp * lax.rsqrt(jnp.mean(grp * grp, axis=0, keepdims=True) + EPS)
                      * _rows(ssdg_ref, 256 * g, 256))


def _step_ssd(proj_t, layer, buf_t, bias_col, alog_col, cw_t, dskip_col, ssdg_col, s_t):
    bsz = proj_t.shape[1]
    return pl.pallas_call(
        _step_ssd_kernel,
        grid=(SSD_HEADS,),
        in_specs=[_const_spec((NP, bsz)),
                  pl.BlockSpec((None, SSD_CONV - 1, bsz, SSD_CONV_DIM), lambda h: (layer, 0, 0, 0)),
                  _const_spec((LANE, 1)), _const_spec((LANE, 1)), _const_spec((SSD_CONV_DIM, SUBLANE)),
                  _const_spec((SSD_W, 1)), _const_spec((SSD_W, 1)),
                  pl.BlockSpec((None, None, 64, 64, bsz), lambda h: (layer, h, 0, 0, 0))],
        out_specs=[pl.BlockSpec((None, 64, 64, bsz), lambda h: (h, 0, 0, 0)),
                   pl.BlockSpec((256, bsz), lambda h: (h // (SSD_HEADS // 2), 0)),
                   _const_spec((bsz, SSD_CONV_DIM))],
        out_shape=[jax.ShapeDtypeStruct((SSD_HEADS, 64, 64, bsz), F32),
                   jax.ShapeDtypeStruct((SSD_W, bsz), F32),
                   jax.ShapeDtypeStruct((bsz, SSD_CONV_DIM), F32)],
        scratch_shapes=[pltpu.VMEM((SSD_CONV_DIM, bsz), F32), pltpu.VMEM((64, bsz), F32),
                        pltpu.VMEM((SSD_W, bsz), F32)],
        compiler_params=_cparams("arbitrary"),
        name="step_ssd",
    )(proj_t, buf_t, bias_col, alog_col, cw_t, dskip_col, ssdg_col, s_t)


def _step_gla_kernel(proj_ref, wf2t_ref, bf_ref, glag_ref, s_ref, s_out, y_ref, dec_s):
    h = pl.program_id(0)

    @pl.when(h == 0)
    def _():
        la = _log_sigmoid(_dot(wf2t_ref[...], proj_ref[GT:GT + LANE, :].astype(BF16)) + bf_ref[...]) / GLA_TAU
        dec_s[...] = jnp.exp(la)

    dec = _rows(dec_s, GLA_DK * h, GLA_DK)
    q = _rows(proj_ref, GQ + GLA_DK * h, GLA_DK) * (GLA_DK ** -0.5)
    k = _rows(proj_ref, GK + GLA_DK * h, GLA_DK)
    v = _rows(proj_ref, GV + 64 * h, 64)
    acc = jnp.zeros_like(v)
    for j in range(GLA_DK):
        sn = dec[j:j + 1, :] * s_ref[j] + k[j:j + 1, :] * v
        s_out[j] = sn
        acc = acc + q[j:j + 1, :] * sn
    gg = _rows(proj_ref, GG + 64 * h, 64)
    y_ref[...] = (acc * lax.rsqrt(jnp.mean(acc * acc, axis=0, keepdims=True) + EPS)
                  * _rows(glag_ref, 64 * h, 64) * _silu(gg))


def _step_gla(proj_t, layer, wf2_t, bf_col, glag_col, s_t):
    bsz = proj_t.shape[1]
    return pl.pallas_call(
        _step_gla_kernel,
        grid=(GLA_HEADS,),
        in_specs=[_const_spec((NP, bsz)), _const_spec((LANE, LANE)), _const_spec((LANE, 1)),
                  _const_spec((256, 1)),
                  pl.BlockSpec((None, None, GLA_DK, 64, bsz), lambda h: (layer, h, 0, 0, 0))],
        out_specs=[pl.BlockSpec((None, GLA_DK, 64, bsz), lambda h: (h, 0, 0, 0)),
                   pl.BlockSpec((64, bsz), lambda h: (h, 0))],
        out_shape=[jax.ShapeDtypeStruct((GLA_HEADS, GLA_DK, 64, bsz), F32),
                   jax.ShapeDtypeStruct((256, bsz), F32)],
        scratch_shapes=[pltpu.VMEM((LANE, bsz), F32)],
        compiler_params=_cparams("arbitrary"),
        name="step_gla",
    )(proj_t, wf2_t, bf_col, glag_col, s_t)


def _pad_row(x, width):
    return jnp.pad(x, [(0, 0), (0, width - x.shape[-1])])


def kernel(x_prompt, x_sample, state_mlstm_c, state_mlstm_n, state_mlstm_m, state_ssd, state_ssd_conv,
           state_gla, state_ffn_conv, norm1_g, w_in, mlstm_b_i, mlstm_b_f, mlstm_norm_g, ssd_conv_w,
           ssd_conv_b, ssd_dt_bias, ssd_a_log, ssd_d, ssd_norm_g, gla_w_f2, gla_b_f, gla_norm_g, w_out,
           norm2_g, w_up, ffn_conv_w, ffn_conv_b, w_down, final_norm_g):
    bsz, seq, _ = x_prompt.shape
    dbs = x_sample.shape[0]
    depth = w_in.shape[0]

    w_down_p = w_down.astype(BF16)
    ffn_cwb = jnp.concatenate([ffn_conv_w, ffn_conv_b[:, None, :],
                               jnp.zeros((depth, SUBLANE - FFN_CONV - 1, 2 * D_FF), F32)], axis=1)
    zeros8 = jnp.zeros((depth, 8), F32)
    bias_rows = _pad_row(jnp.concatenate([mlstm_b_i, mlstm_b_f, ssd_dt_bias], axis=-1), LANE)[:, None, :]
    alog_rows = _pad_row(jnp.concatenate([zeros8, ssd_a_log], axis=-1), LANE)[:, None, :]
    ssd_cwb = jnp.concatenate([ssd_conv_w, ssd_conv_b[:, None, :],
                               jnp.zeros((depth, SUBLANE - SSD_CONV - 1, SSD_CONV_DIM), F32)], axis=1)
    dskip_rows = jnp.repeat(ssd_d, 64, axis=-1)[:, None, :]
    wf2_p = jnp.pad(gla_w_f2, [(0, 0), (GATE_GF, LANE - GATE_GF - GLA_RANK), (0, 0)]).astype(BF16)

    c_t = state_mlstm_c.transpose(0, 2, 3, 4, 1)
    n_t = state_mlstm_n.transpose(0, 2, 3, 1)
    m_t = state_mlstm_m.transpose(0, 2, 1)
    s_t = state_ssd.transpose(0, 2, 3, 4, 1)
    g_t = state_gla.transpose(0, 2, 3, 4, 1)
    conv_t = state_ssd_conv.transpose(0, 2, 1, 3)

    xp = x_prompt.reshape(bsz * seq, D_MODEL)
    xs = x_sample.reshape(dbs, D_MODEL)
    fg = final_norm_g[None, :]
    p_states, s_states = [], []
    for i in range(depth):
        g1 = norm1_g[i][None, :]
        g2 = norm2_g[i][None, :]
        mlg = mlstm_norm_g[i][None, :]
        ssdg = ssd_norm_g[i][None, :]
        glag = gla_norm_g[i][None, :]
        bfr = gla_b_f[i][None, :]
        last = i == depth - 1

        proj, w_perm, w_out_b, w_up_b = _norm_proj(xp, g1, w_in, w_out, w_up, i)
        xp, ffn_st, caug, m_o, ssd_o, gla_o = _mix_ffn(
            xp, proj, bsz, seq, i, bias_rows[i], alog_rows[i], mlg, ssd_cwb[i], dskip_rows[i], ssdg, wf2_p[i], bfr,
            glag, w_out_b, g2, w_up_b, ffn_cwb[i], w_down_p, fg, last)
        caug4 = caug.reshape(bsz, ML_HEADS, ML_DK, LANE)
        ssd4 = ssd_o.reshape(bsz, SSD_HEADS, 64, LANE)
        gla4 = gla_o.reshape(bsz, GLA_HEADS, GLA_DK, 256)
        p_states.append((
            caug4[..., :64], caug4[..., 64], m_o[:, 0, :ML_HEADS],
            jnp.where(jnp.arange(SSD_HEADS)[None, :, None, None] < 4, ssd4[..., :64], ssd4[..., 64:]),
            proj.reshape(bsz, seq, NP)[:, seq - (SSD_CONV - 1):, SXBC:SXBC + SSD_CONV_DIM],
            jnp.stack([gla4[:, h, :, 64 * h:64 * (h + 1)] for h in range(GLA_HEADS)], axis=1),
            ffn_st[:, SUBLANE - (FFN_CONV - 1):, :],
        ))

        proj_t = _norm_proj_t(xs, g1, w_perm)
        c_n, n_n, m_n, y_ml = _step_mlstm(proj_t, i, bias_rows[i].T, mlg.T, c_t, n_t, m_t)
        s_n, y_ssd, raw_s = _step_ssd(proj_t, i, conv_t, bias_rows[i].T, alog_rows[i].T, ssd_cwb[i].T,
                                      dskip_rows[i].T, ssdg.T, s_t)
        g_n, y_gla = _step_gla(proj_t, i, wf2_p[i].T, bfr.T, glag.T, g_t)
        mix_t = jnp.concatenate([y_ml, y_ssd, y_gla], axis=0)
        prev = state_ffn_conv[i].transpose(1, 0, 2)
        xs, up_s = _out_ffn_step(xs, mix_t, i, w_out_b, g2, w_up_b, ffn_cwb[i], w_down_p, fg, last, prev)
        s_states.append((
            c_n, n_n, m_n, s_n,
            jnp.concatenate([state_ssd_conv[i][:, 1:], raw_s[:, None, :]], axis=1),
            g_n,
            jnp.concatenate([state_ffn_conv[i][:, 1:], up_s[:, None, :]], axis=1),
        ))

    y_prompt = xp.reshape(bsz, seq, D_MODEL)
    y_sample = xs.reshape(dbs, 1, D_MODEL)

    def stk(sts, j):
        return jnp.stack([s[j] for s in sts], axis=0)

    s_perm = {0: (0, 4, 1, 2, 3), 1: (0, 3, 1, 2), 2: (0, 2, 1), 3: (0, 4, 1, 2, 3), 5: (0, 4, 1, 2, 3)}
    s_out = tuple(stk(s_states, j).transpose(s_perm[j]) if j in s_perm else stk(s_states, j) for j in range(7))
    return (y_prompt, y_sample) + tuple(stk(p_states, j) for j in range(7)) + s_out
```
